```python
import math
import jax, jax.numpy as jnp
from jax import lax
import numpy as np

D_MODEL = 1024
BATCH = 32
SEQ = 256
DEPTH = 4
DEC_BATCH = 2
DEC_SEQ = 4096
PAST_LEN = 512

GRID_W = 64
GLA_HEADS = 4
GLA_DK = 64
GLA_DV = 128
GLA_CHUNK = 64
GATE_TAU = 16.0
LR_RANK = 16
FNO_GROUPS = 4
FNO_GROUP_W = 128
D_FF = 4 * D_MODEL
EPS = 1e-6
QK_W = GLA_HEADS * GLA_DK
V_W = GLA_HEADS * GLA_DV
FNO_W = FNO_GROUPS * FNO_GROUP_W
MIX_W = V_W + FNO_W
D_IN = 2 * QK_W + 2 * V_W + 2 * LR_RANK + FNO_W
SPLITS = [QK_W, 2 * QK_W, 2 * QK_W + V_W, 2 * QK_W + 2 * V_W,
          2 * QK_W + 2 * V_W + LR_RANK, 2 * QK_W + 2 * V_W + 2 * LR_RANK]

kernel_name = "hybrid_gla_fnet_diffusion_step"


def _rmsnorm(x, g):
    x32 = x.astype(jnp.float32)
    y = x32 * lax.rsqrt(jnp.mean(x32 * x32, axis=-1, keepdims=True) + EPS)
    return (y * g.astype(jnp.float32)).astype(x.dtype)


def _pos_emb_2d(n_tokens, dtype):
    rows = n_tokens // GRID_W
    t = jnp.arange(rows * GRID_W)
    r = (t // GRID_W).astype(jnp.float32)[:, None]
    col = (t % GRID_W).astype(jnp.float32)[:, None]
    nf = D_MODEL // 4
    omega = 1.0 / (10000.0 ** (jnp.arange(nf, dtype=jnp.float32) / nf))
    emb = jnp.concatenate([jnp.sin(r * omega), jnp.cos(r * omega),
                           jnp.sin(col * omega), jnp.cos(col * omega)], axis=-1)
    return emb.astype(dtype)


def _gla_chunked(q, k, v, log_a, s0):
    B, N, H, DK = q.shape
    DV = v.shape[-1]
    nc = N // GLA_CHUNK

    def to_chunks(a):
        d = a.shape[-1]
        return a.astype(jnp.float32).reshape(B, nc, GLA_CHUNK, H, d).transpose(1, 0, 3, 2, 4)

    qc, kc, vc, lac = to_chunks(q), to_chunks(k), to_chunks(v), to_chunks(log_a)
    causal = jnp.tril(jnp.ones((GLA_CHUNK, GLA_CHUNK), dtype=bool))[:, :, None]

    def step(S, inp):
        qi, ki, vi, lai = inp
        b = jnp.cumsum(lai, axis=2)
        inter = jnp.einsum('bhtd,bhde->bhte', qi * jnp.exp(b), S)
        diff = b[:, :, :, None, :] - b[:, :, None, :, :]
        decay = jnp.exp(jnp.where(causal, diff, -jnp.inf))
        A = jnp.einsum('bhtd,bhsd,bhtsd->bhts', qi, ki, decay)
        intra = jnp.einsum('bhts,bhse->bhte', A, vi)
        b_last = b[:, :, -1:, :]
        S_new = jnp.exp(b_last[:, :, 0, :])[..., None] * S + jnp.einsum(
            'bhsd,bhse->bhde', ki * jnp.exp(b_last - b), vi)
        return S_new, inter + intra

    S_fin, out = lax.scan(step, s0.astype(jnp.float32), (qc, kc, vc, lac))
    out = out.transpose(1, 0, 3, 2, 4).reshape(B, N, H, DV)
    return out.astype(q.dtype), S_fin


def _log_decay(lr, w_a2, b_a2):
    logits = jnp.einsum('bnr,rk->bnk', lr.astype(jnp.float32), w_a2.astype(jnp.float32)) + b_a2.astype(jnp.float32)
    return jax.nn.log_sigmoid(logits) / GATE_TAU


def _fourier_mix(u):
    B, N, _ = u.shape
    ug = u.astype(jnp.float32).reshape(B, N, FNO_GROUPS, FNO_GROUP_W)
    y = jnp.real(jnp.fft.fftn(ug, axes=(1, 3), norm="ortho"))
    return y.reshape(B, N, FNO_W).astype(u.dtype)


def _layer(x, mod, s0_fwd, s0_bwd, w_in, w_out, w_a2, b_a2, g_head,
           g_pre_mix, g_post_mix, g_pre_mlp, g_post_mlp, w_fc1, w_fc2):
    B, N, _ = x.shape
    sh_m, sc_m, gt_m, sh_f, sc_f, gt_f = jnp.split(mod.astype(x.dtype), 6, axis=-1)
    h = _rmsnorm(x, g_pre_mix) * (1 + sc_m) + sh_m
    z = h @ w_in
    q, k, v, g, lr_f, lr_b, u = jnp.split(z, SPLITS, axis=-1)
    q = q.reshape(B, N, GLA_HEADS, GLA_DK) * (GLA_DK ** -0.5)
    k = k.reshape(B, N, GLA_HEADS, GLA_DK)
    v = v.reshape(B, N, GLA_HEADS, GLA_DV)
    la_f = _log_decay(lr_f, w_a2[0], b_a2[0]).reshape(B, N, GLA_HEADS, GLA_DK)
    la_b = _log_decay(lr_b, w_a2[1], b_a2[1]).reshape(B, N, GLA_HEADS, GLA_DK)
    o_f, s_f = _gla_chunked(q, k, v, la_f, s0_fwd)
    flip = lambda a: jnp.flip(a, axis=1)
    o_b, s_b = _gla_chunked(flip(q), flip(k), flip(v), flip(la_b), s0_bwd)
    o = _rmsnorm(o_f + flip(o_b), g_head).reshape(B, N, V_W) * jax.nn.silu(g)
    y_fno = _fourier_mix(u)
    y = jnp.concatenate([o, y_fno], axis=-1) @ w_out
    x = x + gt_m * _rmsnorm(y, g_post_mix)
    h = _rmsnorm(x, g_pre_mlp) * (1 + sc_f) + sh_f
    f = jnp.square(jax.nn.relu(h @ w_fc1)) @ w_fc2
    x = x + gt_f * _rmsnorm(f, g_post_mlp)
    return x, s_f, s_b


def setup_inputs(seed: int = 0) -> dict:
    key = jax.random.key(seed)
    ks = jax.random.split(key, 20)
    f32 = jnp.float32
    nrm = lambda k, shape, s: (jax.random.normal(k, shape, f32) * s)
    st_shape = (DEC_BATCH, DEPTH, GLA_HEADS, GLA_DK, GLA_DV)
    return {
        "x_prompt": nrm(ks[0], (BATCH, SEQ, D_MODEL), 1.0),
        "x_sample": nrm(ks[1], (DEC_BATCH, DEC_SEQ, D_MODEL), 1.0),
        "state_gla_fwd": nrm(ks[2], st_shape, 1.0),
        "state_gla_bwd": nrm(ks[3], st_shape, 1.0),
        "c": nrm(ks[4], (DEC_BATCH, D_MODEL), 1.0),
        "c_ctx": nrm(ks[5], (D_MODEL,), 1.0),
        "w_in": nrm(ks[6], (DEPTH, D_MODEL, D_IN), D_MODEL ** -0.5),
        "w_out": nrm(ks[7], (DEPTH, MIX_W, D_MODEL), MIX_W ** -0.5),
        "w_a2": nrm(ks[8], (DEPTH, 2, LR_RANK, QK_W), LR_RANK ** -0.5),
        "b_a2": 1.0 + nrm(ks[9], (DEPTH, 2, QK_W), 0.5),
        "g_head": 1.0 + nrm(ks[10], (DEPTH, GLA_HEADS, GLA_DV), 0.02),
        "w_ada": nrm(ks[11], (DEPTH, D_MODEL, 6 * D_MODEL), 0.5 * D_MODEL ** -0.5),
        "b_ada": nrm(ks[12], (DEPTH, 6 * D_MODEL), 0.02),
        "g_pre_mix": 1.0 + nrm(ks[13], (DEPTH, D_MODEL), 0.02),
        "g_post_mix": 1.0 + nrm(ks[14], (DEPTH, D_MODEL), 0.02),
        "g_pre_mlp": 1.0 + nrm(ks[15], (DEPTH, D_MODEL), 0.02),
        "g_post_mlp": 1.0 + nrm(ks[16], (DEPTH, D_MODEL), 0.02),
        "w_fc1": nrm(ks[17], (DEPTH, D_MODEL, D_FF), D_MODEL ** -0.5),
        "w_fc2": nrm(ks[18], (DEPTH, D_FF, D_MODEL), D_FF ** -0.5),
    }


def reference(x_prompt, x_sample, state_gla_fwd, state_gla_bwd, c, c_ctx,
              w_in, w_out, w_a2, b_a2, g_head, w_ada, b_ada,
              g_pre_mix, g_post_mix, g_pre_mlp, g_post_mlp, w_fc1, w_fc2):
    ctx_mod = jnp.einsum('d,ldk->lk', jax.nn.silu(c_ctx), w_ada) + b_ada
    lat_mod = jnp.einsum('bd,ldk->lbk', jax.nn.silu(c), w_ada) + b_ada[:, None, :]

    bp = x_prompt.shape[0]
    zero_state = jnp.zeros((bp, GLA_HEADS, GLA_DK, GLA_DV), jnp.float32)
    xp = x_prompt
    fwd_states, bwd_states = [], []
    for l in range(DEPTH):
        xp, s_f, s_b = _layer(xp, ctx_mod[l][None, None, :], zero_state, zero_state,
                              w_in[l], w_out[l], w_a2[l], b_a2[l], g_head[l],
                              g_pre_mix[l], g_post_mix[l], g_pre_mlp[l], g_post_mlp[l],
                              w_fc1[l], w_fc2[l])
        fwd_states.append(s_f)
        bwd_states.append(s_b)
    new_state_fwd = jnp.stack(fwd_states, axis=1).astype(x_prompt.dtype)
    new_state_bwd = jnp.stack(bwd_states, axis=1).astype(x_prompt.dtype)

    xs = x_sample + _pos_emb_2d(x_sample.shape[1], x_sample.dtype)[None]
    for l in range(DEPTH):
        xs, _, _ = _layer(xs, lat_mod[l][:, None, :], state_gla_fwd[:, l], state_gla_bwd[:, l],
                          w_in[l], w_out[l], w_a2[l], b_a2[l], g_head[l],
                          g_pre_mix[l], g_post_mix[l], g_pre_mlp[l], g_post_mlp[l],
                          w_fc1[l], w_fc2[l])

    return (xp, xs, new_state_fwd, new_state_bwd)
```

```python
import functools

import numpy as np
import jax
import jax.numpy as jnp
from jax import lax
from jax.experimental import pallas as pl
from jax.experimental.pallas import tpu as pltpu

D_MODEL = 1024
DEPTH = 4
GRID_W = 64
GLA_HEADS = 4
GLA_DK = 64
GLA_DV = 128
CHUNK = 64
GATE_TAU = 16.0
LR_RANK = 16
FNO_GROUPS = 4
FNO_GROUP_W = 128
D_FF = 4 * D_MODEL
EPS = 1e-6
QK_W = GLA_HEADS * GLA_DK
V_W = GLA_HEADS * GLA_DV
FNO_W = FNO_GROUPS * FNO_GROUP_W
D_IN = 2 * QK_W + 2 * V_W + 2 * LR_RANK + FNO_W
U_OFF = 2 * QK_W + 2 * V_W
LR_OFF = U_OFF + FNO_W
HEAD_PAIRS = GLA_HEADS // 2
PAIR_K = 2 * GLA_DK
PAIR_V = 2 * GLA_DV
N_LEVELS = 6
N_GAPS = N_LEVELS + 2

VMEM_LIMIT = 48 * 1024 * 1024
ROW_TILE = 512
MOD_ROWS = 8

BF16 = jnp.bfloat16
F32 = jnp.float32


def _gap_matrices():
    c = CHUNK
    m_all = np.zeros((N_GAPS, c, c), np.float32)
    for p in range(c):
        m_all[0, p, :p + 1] = 1.0
        m_all[1, p, p + 1:] = 1.0
    for lv in range(N_LEVELS):
        m = c >> (lv + 1)
        for p in range(c):
            base = (p // (2 * m)) * 2 * m
            ref = base + m - 1
            if p > ref:
                m_all[2 + lv, p, ref + 1:p + 1] = 1.0
            else:
                m_all[2 + lv, p, p + 1:ref + 1] = 1.0
    fwd = m_all.reshape(N_GAPS * c, c)
    bwd = m_all[:, ::-1, ::-1].reshape(N_GAPS * c, c)
    lvl = np.full((c, c), N_LEVELS + 1, np.int32)
    for t in range(c):
        for s in range(t + 1):
            if s == t:
                lvl[t, s] = 0
            else:
                top = (t ^ s).bit_length() - 1
                lvl[t, s] = N_LEVELS - top
    lvl_f = np.concatenate([lvl, lvl], axis=1)
    lvl_b = np.concatenate([lvl[::-1, ::-1], lvl[::-1, ::-1]], axis=1)
    return fwd, bwd, lvl_f, lvl_b


def _dft_tables(n):
    idx = np.arange(n)
    ang = 2.0 * np.pi * ((idx[:, None] * idx[None, :]) % n) / n
    s = 1.0 / np.sqrt(n)
    return (np.cos(ang) * s).astype(np.float32), (np.sin(ang) * s).astype(np.float32)


def _dft_big(n, n1):
    n2 = n // n1
    k = np.arange(n)
    a = 2.0 * np.pi * ((np.arange(n1)[:, None] * n2 * k[None, :]) % n) / n
    b = 2.0 * np.pi * ((np.arange(n2)[:, None] * k[None, :]) % n) / n
    s = 1.0 / np.sqrt(n)
    ca, sa = jnp.asarray(np.cos(a) * s, F32), jnp.asarray(np.sin(a) * s, F32)
    cb, sb = jnp.asarray(np.cos(b), F32), jnp.asarray(np.sin(b), F32)
    cos = ca[:, None, :] * cb[None] - sa[:, None, :] * sb[None]
    nsin = -(sa[:, None, :] * cb[None] + ca[:, None, :] * sb[None])
    return cos.reshape(n, n).astype(BF16), nsin.reshape(n, n).astype(BF16)


def _pos_emb_2d(n_tokens):
    t = jnp.arange(n_tokens)
    r = (t // GRID_W).astype(F32)[:, None]
    col = (t % GRID_W).astype(F32)[:, None]
    nf = D_MODEL // 4
    omega = 1.0 / (10000.0 ** (jnp.arange(nf, dtype=F32) / nf))
    return jnp.concatenate([jnp.sin(r * omega), jnp.cos(r * omega),
                            jnp.sin(col * omega), jnp.cos(col * omega)], axis=-1)


def _silu(x):
    return x * (1.0 / (1.0 + jnp.exp(-x)))


def _rms(x, g):
    return x * lax.rsqrt(jnp.mean(x * x, axis=-1, keepdims=True) + EPS) * g


def _dot(a, b):
    return jnp.dot(a, b, preferred_element_type=F32)


def _dot_nt(a, b):
    return lax.dot_general(a, b, (((1,), (1,)), ((), ())), preferred_element_type=F32)


def _dot_tn(a, b):
    return lax.dot_general(a, b, (((0,), (0,)), ((), ())), preferred_element_type=F32)


def _params(n_grid):
    return pltpu.CompilerParams(dimension_semantics=("arbitrary",) * n_grid,
                                vmem_limit_bytes=VMEM_LIMIT)


def _mod_kernel(c_ref, w_ref, b_ref, o_ref):
    s = _silu(c_ref[...])
    o_ref[...] = jnp.dot(s, w_ref[...], preferred_element_type=F32,
                         precision=lax.Precision.HIGHEST) + b_ref[...]


def _modulation(cvec, w_ada, b_ada):
    tn = 1536
    return pl.pallas_call(
        _mod_kernel,
        grid=(DEPTH, 6 * D_MODEL // tn),
        in_specs=[pl.BlockSpec((MOD_ROWS, D_MODEL), lambda l, j: (0, 0)),
                  pl.BlockSpec((None, D_MODEL, tn), lambda l, j: (l, 0, j)),
                  pl.BlockSpec((None, 1, tn), lambda l, j: (l, 0, j))],
        out_specs=pl.BlockSpec((None, MOD_ROWS, tn), lambda l, j: (l, 0, j)),
        out_shape=jax.ShapeDtypeStruct((DEPTH, MOD_ROWS, 6 * D_MODEL), F32),
        compiler_params=_params(2),
        name="modulation",
    )(cvec, w_ada, b_ada.reshape(DEPTH, 1, 6 * D_MODEL))


def _embed_kernel(x_ref, p_ref, o_ref):
    o_ref[...] = x_ref[...] + p_ref[...]


def _embed(x_sample, pos):
    b, n, d = x_sample.shape
    return pl.pallas_call(
        _embed_kernel,
        grid=(b, n // ROW_TILE),
        in_specs=[pl.BlockSpec((None, ROW_TILE, d), lambda i, j: (i, j, 0)),
                  pl.BlockSpec((ROW_TILE, d), lambda i, j: (j, 0))],
        out_specs=pl.BlockSpec((None, ROW_TILE, d), lambda i, j: (i, j, 0)),
        out_shape=jax.ShapeDtypeStruct(x_sample.shape, x_sample.dtype),
        compiler_params=_params(2),
        name="embed",
    )(x_sample, pos)


def _premix_kernel(x_ref, mod_ref, gpre_ref, win_ref, wa2_ref, ba2_ref, cs_ref,
                   q_ref, k_ref, v_ref, g_ref, la_ref, uc_ref, us_ref):
    x = x_ref[...]
    sh = mod_ref[:, 0:D_MODEL]
    sc = mod_ref[:, D_MODEL:2 * D_MODEL]
    h = (_rms(x, gpre_ref[...]) * (1.0 + sc) + sh).astype(BF16)
    q_ref[...] = _dot(h, win_ref[:, 0:QK_W]) * (GLA_DK ** -0.5)
    k_ref[...] = _dot(h, win_ref[:, QK_W:2 * QK_W])
    v_ref[...] = _dot(h, win_ref[:, 2 * QK_W:2 * QK_W + V_W]).astype(BF16)
    g_ref[...] = _dot(h, win_ref[:, 2 * QK_W + V_W:U_OFF]).astype(BF16)
    lr = _dot(h, win_ref[:, LR_OFF:LR_OFF + 2 * LR_RANK]).astype(BF16)
    logit = _dot(lr, wa2_ref[...]) + ba2_ref[...]
    softplus_neg = jnp.maximum(-logit, 0.0) + jnp.log(1.0 + jnp.exp(-jnp.abs(logit)))
    la_ref[...] = softplus_neg * (-1.0 / GATE_TAU)
    for grp in range(FNO_GROUPS):
        lo = U_OFF + grp * FNO_GROUP_W
        u = _dot(h, win_ref[:, lo:lo + FNO_GROUP_W]).astype(BF16)
        ucs = _dot(u, cs_ref[...])
        uc_ref[:, grp * FNO_GROUP_W:(grp + 1) * FNO_GROUP_W] = ucs[:, :FNO_GROUP_W].astype(BF16)
        us_ref[:, grp * FNO_GROUP_W:(grp + 1) * FNO_GROUP_W] = ucs[:, FNO_GROUP_W:].astype(BF16)


def _premix(layer, x, mod, mod_sel, g_pre, w_in, wa2, ba2, cs_w):
    t = x.shape[0]
    row = lambda w: pl.BlockSpec((ROW_TILE, w), lambda i: (i, 0))
    lay = lambda *shape: pl.BlockSpec((None,) + shape, lambda i: (layer,) + (0,) * len(shape))
    out = lambda w, dt: jax.ShapeDtypeStruct((t, w), dt)
    return pl.pallas_call(
        _premix_kernel,
        grid=(t // ROW_TILE,),
        in_specs=[row(D_MODEL),
                  pl.BlockSpec((None, None, 1, 6 * D_MODEL), lambda i: (layer, mod_sel(i), 0, 0)),
                  lay(1, D_MODEL), lay(D_MODEL, D_IN), lay(2 * LR_RANK, 2 * QK_W), lay(1, 2 * QK_W),
                  pl.BlockSpec((FNO_GROUP_W, 2 * FNO_GROUP_W), lambda i: (0, 0))],
        out_specs=[row(QK_W), row(QK_W), row(V_W), row(V_W), row(2 * QK_W), row(FNO_W), row(FNO_W)],
        out_shape=[out(QK_W, F32), out(QK_W, F32), out(V_W, BF16), out(V_W, BF16),
                   out(2 * QK_W, F32), out(FNO_W, BF16), out(FNO_W, BF16)],
        compiler_params=_params(1),
        name="premix",
    )(x, mod, g_pre, w_in, wa2, ba2, cs_w)


def _gla_chunk(r0, forward, q_ref, k_ref, v_ref, la_ref, gap_ref, lvl_ref, st_ref):
    rows = pl.ds(r0, CHUNK)
    lane = lax.broadcasted_iota(jnp.int32, (CHUNK, PAIR_K), 1)
    head0 = lane < GLA_DK
    la = la_ref[rows, :]
    la_hi = la.astype(BF16)
    la_lo = (la - la_hi.astype(F32)).astype(BF16)
    gaps = _dot(gap_ref[...], jnp.concatenate([la_hi, la_lo], axis=1))
    decay = jnp.exp(gaps[:, :PAIR_K] + gaps[:, PAIR_K:])
    blk = lambda j: decay[j * CHUNK:(j + 1) * CHUNK]
    q = q_ref[rows, :]
    k = k_ref[rows, :]
    k_h = (jnp.where(head0, k, 0.0), jnp.where(head0, 0.0, k))
    v = v_ref[rows, :]
    lvl = lvl_ref[...]

    a = jnp.zeros((CHUNK, 2 * CHUNK), F32)
    for lv in range(N_LEVELS + 1):
        if lv == 0:
            ql = q.astype(BF16)
            kl = jnp.concatenate([k_h[0].astype(BF16), k_h[1].astype(BF16)], axis=0)
        else:
            d = blk(1 + lv)
            ql = (q * d).astype(BF16)
            kl = jnp.concatenate([(k_h[0] * d).astype(BF16), (k_h[1] * d).astype(BF16)], axis=0)
        a = jnp.where(lvl == lv, _dot_nt(ql, kl), a)

    q_in = (q * blk(0)).astype(BF16)
    st = jnp.concatenate([st_ref[0].astype(BF16), st_ref[1].astype(BF16)], axis=0)
    inter = _dot_nt(q_in, st)
    vlane = lax.broadcasted_iota(jnp.int32, (CHUNK, PAIR_V), 1)
    zero = jnp.zeros_like(v)
    v_blk = jnp.concatenate([jnp.where(vlane < GLA_DV, v, zero),
                             jnp.where(vlane < GLA_DV, zero, v)], axis=0)
    o = inter + _dot(a.astype(BF16), v_blk)

    d_out = blk(1)
    total = blk(0)[CHUNK - 1:CHUNK] if forward else blk(0)[0:1]
    for hd in range(2):
        k_out = (k_h[hd] * d_out).astype(BF16)
        upd = _dot_tn(v[:, hd * GLA_DV:(hd + 1) * GLA_DV], k_out)
        st_ref[hd] = st_ref[hd] * total + upd
    return o


def _gla_kernel(*refs, seq_len, n_seq, has_state, emit_state):
    (q_ref, k_ref, v_ref, laf_ref, lab_ref, g_ref, gh_ref,
     gapf_ref, gapb_ref, lvlf_ref, lvlb_ref) = refs[:11]
    pos = 11
    if has_state:
        s0f_ref, s0b_ref = refs[pos:pos + 2]
        pos += 2
    o_ref = refs[pos]
    pos += 1
    if emit_state:
        sf_ref, sb_ref = refs[pos:pos + 2]
        pos += 2
    acc_ref, st_ref = refs[pos:pos + 2]
    n_chunks = seq_len // CHUNK

    for sq in range(n_seq):
        base = sq * seq_len
        for forward in (True, False):
            if has_state:
                s0 = s0f_ref if forward else s0b_ref
                st_ref[...] = s0[...]
            else:
                st_ref[...] = jnp.zeros_like(st_ref)
            la_ref, gap_ref, lvl_ref = ((laf_ref, gapf_ref, lvlf_ref) if forward
                                        else (lab_ref, gapb_ref, lvlb_ref))

            def body(c, carry, forward=forward, la_ref=la_ref, gap_ref=gap_ref, lvl_ref=lvl_ref):
                ci = c if forward else n_chunks - 1 - c
                r0 = pl.multiple_of(base + ci * CHUNK, CHUNK)
                o = _gla_chunk(r0, forward, q_ref, k_ref, v_ref, la_ref, gap_ref, lvl_ref, st_ref)
                rows = pl.ds(r0, CHUNK)
                if forward:
                    acc_ref[rows, :] = o
                else:
                    o = o + acc_ref[rows, :]
                    g = g_ref[rows, :].astype(F32)
                    for hd in range(2):
                        sl = slice(hd * GLA_DV, (hd + 1) * GLA_DV)
                        oh = _rms(o[:, sl], gh_ref[:, sl]) * _silu(g[:, sl])
                        o_ref[rows, sl] = oh.astype(o_ref.dtype)
                return carry

            lax.fori_loop(0, n_chunks, body, 0)
            if emit_state:
                s_out = sf_ref if forward else sb_ref
                for hd in range(2):
                    s_out[sq, hd] = st_ref[hd]


def _gla(layer, q, k, v, la, g, g_head, consts, row0, seq_len, n_seq_total, n_seq, s0=None,
         emit_state=False):
    gapf, gapb, lvlf, lvlb = consts
    rows = seq_len * n_seq
    blk0 = row0 // rows
    n_steps = n_seq_total // n_seq
    rspec = lambda w, off=0: pl.BlockSpec((rows, w), lambda i, j: (blk0 + i, j + off))
    const = lambda a: pl.BlockSpec(a.shape, lambda i, j: (0,) * a.ndim)
    in_specs = [rspec(PAIR_K), rspec(PAIR_K), rspec(PAIR_V), rspec(PAIR_K), rspec(PAIR_K, HEAD_PAIRS),
                rspec(PAIR_V),
                pl.BlockSpec((None, 1, PAIR_V), lambda i, j: (layer, 0, j)),
                const(gapf), const(gapb), const(lvlf), const(lvlb)]
    args = [q, k, v, la, la, g, g_head, gapf, gapb, lvlf, lvlb]
    if s0 is not None:
        sspec = pl.BlockSpec((None, None, 2, GLA_DV, PAIR_K), lambda i, j: (i, layer, j, 0, 0))
        in_specs += [sspec, sspec]
        args += list(s0)
    out_specs = [pl.BlockSpec((rows, PAIR_V), lambda i, j: (i, j))]
    out_shape = [jax.ShapeDtypeStruct((n_seq_total * seq_len, V_W), BF16)]
    if emit_state:
        st_spec = pl.BlockSpec((n_seq, 2, GLA_DV, PAIR_K), lambda i, j: (i, j, 0, 0))
        st_shape = jax.ShapeDtypeStruct((n_seq_total, GLA_HEADS, GLA_DV, PAIR_K), F32)
        out_specs += [st_spec, st_spec]
        out_shape += [st_shape, st_shape]
    kern = functools.partial(_gla_kernel, seq_len=seq_len, n_seq=n_seq,
                             has_state=s0 is not None, emit_state=emit_state)
    return pl.pallas_call(
        kern,
        grid=(n_steps, HEAD_PAIRS),
        in_specs=in_specs,
        out_specs=out_specs,
        out_shape=out_shape,
        scratch_shapes=[pltpu.VMEM((rows, PAIR_V), F32), pltpu.VMEM((2, GLA_DV, PAIR_K), F32)],
        compiler_params=_params(2),
        name="gla_state" if emit_state else "gla",
    )(*args)


def _fnet_kernel(c_ref, ns_ref, uc_ref, us_ref, o_ref, *, seq_len, n_seq):
    for sq in range(n_seq):
        rows = slice(sq * seq_len, (sq + 1) * seq_len)
        y = _dot(c_ref[...], uc_ref[rows, :]) + _dot(ns_ref[...], us_ref[rows, :])
        o_ref[rows, :] = y.astype(o_ref.dtype)


def _fnet_ctx(uc, us, cos_n, nsin_n, seq_len, n_seq_total, n_seq):
    rows = seq_len * n_seq
    rspec = pl.BlockSpec((rows, FNO_W), lambda i: (i, 0))
    cspec = pl.BlockSpec((seq_len, seq_len), lambda i: (0, 0))
    return pl.pallas_call(
        functools.partial(_fnet_kernel, seq_len=seq_len, n_seq=n_seq),
        grid=(n_seq_total // n_seq,),
        in_specs=[cspec, cspec, rspec, rspec],
        out_specs=rspec,
        out_shape=jax.ShapeDtypeStruct((n_seq_total * seq_len, FNO_W), BF16),
        compiler_params=_params(1),
        name="fnet_ctx",
    )(cos_n, nsin_n, uc, us)


def _fnet_lat_kernel(c_ref, ns_ref, uc_ref, us_ref, o_ref):
    y = _dot(c_ref[...], uc_ref[...]) + _dot(ns_ref[...], us_ref[...])
    o_ref[...] = y.astype(o_ref.dtype)


def _fnet_lat(uc, us, cos_n, nsin_n, row0, seq_len, n_seq_total):
    tr = 256
    blk0 = row0 // seq_len
    mspec = pl.BlockSpec((tr, seq_len), lambda b, i: (i, 0))
    uspec = pl.BlockSpec((seq_len, FNO_W), lambda b, i: (blk0 + b, 0))
    n_tiles = seq_len // tr
    return pl.pallas_call(
        _fnet_lat_kernel,
        grid=(n_seq_total, n_tiles),
        in_specs=[mspec, mspec, uspec, uspec],
        out_specs=pl.BlockSpec((tr, FNO_W), lambda b, i: (b * n_tiles + i, 0)),
        out_shape=jax.ShapeDtypeStruct((n_seq_total * seq_len, FNO_W), BF16),
        compiler_params=_params(2),
        name="fnet_lat",
    )(cos_n, nsin_n, uc, us)


def _post_kernel(x_ref, o_ref, yf_ref, mod_ref, gpm_ref, gpre_ref, gpost_ref,
                 wout_ref, w1_ref, w2_ref, out_ref):
    gt_m = mod_ref[:, 2 * D_MODEL:3 * D_MODEL]
    sh_f = mod_ref[:, 3 * D_MODEL:4 * D_MODEL]
    sc_f = mod_ref[:, 4 * D_MODEL:5 * D_MODEL]
    gt_f = mod_ref[:, 5 * D_MODEL:6 * D_MODEL]
    y = _dot(o_ref[...], wout_ref[0:V_W, :]) + _dot(yf_ref[...], wout_ref[V_W:, :])
    x = x_ref[...] + gt_m * _rms(y, gpm_ref[...])
    h = (_rms(x, gpre_ref[...]) * (1.0 + sc_f) + sh_f).astype(BF16)
    f = jnp.zeros((ROW_TILE, D_MODEL), F32)
    n_split = 4
    wf = D_FF // n_split
    for j in range(n_split):
        a = jnp.maximum(_dot(h, w1_ref[:, j * wf:(j + 1) * wf]), 0.0)
        f = f + _dot((a * a).astype(BF16), w2_ref[j * wf:(j + 1) * wf, :])
    out_ref[...] = x + gt_f * _rms(f, gpost_ref[...])


def _post(layer, x, o, yf, mod, mod_sel, g_post_mix, g_pre_mlp, g_post_mlp, w_out, w_fc1, w_fc2):
    t = x.shape[0]
    row = lambda w: pl.BlockSpec((ROW_TILE, w), lambda i: (i, 0))
    lay = lambda *shape: pl.BlockSpec((None,) + shape, lambda i: (layer,) + (0,) * len(shape),
                                      pipeline_mode=pl.Buffered(1))
    return pl.pallas_call(
        _post_kernel,
        grid=(t // ROW_TILE,),
        in_specs=[row(D_MODEL), row(V_W), row(FNO_W),
                  pl.BlockSpec((None, None, 1, 6 * D_MODEL), lambda i: (layer, mod_sel(i), 0, 0)),
                  lay(1, D_MODEL), lay(1, D_MODEL), lay(1, D_MODEL),
                  lay(V_W + FNO_W, D_MODEL), lay(D_MODEL, D_FF), lay(D_FF, D_MODEL)],
        out_specs=row(D_MODEL),
        out_shape=jax.ShapeDtypeStruct((t, D_MODEL), F32),
        compiler_params=_params(1),
        name="post",
    )(x, o, yf, mod, g_post_mix, g_pre_mlp, g_post_mlp, w_out, w_fc1, w_fc2)


def _state_to_kernel(s):
    st = jnp.swapaxes(s, -1, -2)
    z = jnp.zeros_like(st)
    even = jnp.concatenate([st, z], axis=-1)
    odd = jnp.concatenate([z, st], axis=-1)
    is_even = (jnp.arange(GLA_HEADS) % 2 == 0)[None, None, :, None, None]
    return jnp.where(is_even, even, odd)


def _state_from_kernel(s):
    even = s[..., :GLA_DK]
    odd = s[..., GLA_DK:]
    is_even = (jnp.arange(GLA_HEADS) % 2 == 0)[None, :, None, None]
    return jnp.swapaxes(jnp.where(is_even, even, odd), -1, -2)


def kernel(x_prompt, x_sample, state_gla_fwd, state_gla_bwd, c, c_ctx, w_in, w_out, w_a2, b_a2,
           g_head, w_ada, b_ada, g_pre_mix, g_post_mix, g_pre_mlp, g_post_mlp, w_fc1, w_fc2):
    batch, seq, _ = x_prompt.shape
    dec_batch, dec_seq, _ = x_sample.shape
    t_ctx = batch * seq
    t_lat = dec_batch * dec_seq
    ctx_tiles = t_ctx // ROW_TILE
    lat_tiles = dec_seq // ROW_TILE

    def mod_sel(i):
        return jnp.where(i < ctx_tiles, 0, 1 + (i - ctx_tiles) // lat_tiles)

    w_in_p = jnp.concatenate([w_in[:, :, :U_OFF], w_in[:, :, U_OFF + 2 * LR_RANK:],
                              w_in[:, :, U_OFF:U_OFF + 2 * LR_RANK]], axis=-1).astype(BF16)
    za = jnp.zeros((DEPTH, LR_RANK, QK_W), F32)
    wa2 = jnp.concatenate([jnp.concatenate([w_a2[:, 0], za], axis=-1),
                           jnp.concatenate([za, w_a2[:, 1]], axis=-1)], axis=1).astype(BF16)
    ba2 = b_a2.reshape(DEPTH, 1, 2 * QK_W)
    w_out_b = w_out.astype(BF16)
    w_fc1_b = w_fc1.astype(BF16)
    w_fc2_b = w_fc2.astype(BF16)
    vec = lambda a: a.reshape(DEPTH, 1, -1)
    g_head_r = g_head.reshape(DEPTH, 1, V_W)

    gapf, gapb, lvlf, lvlb = _gap_matrices()
    as_bf16 = lambda a: jnp.asarray(a, F32).astype(BF16)
    consts = (as_bf16(gapf), as_bf16(gapb), jnp.asarray(lvlf), jnp.asarray(lvlb))
    cw, sw = _dft_tables(FNO_GROUP_W)
    cs_w = as_bf16(np.concatenate([cw, sw], axis=1))
    cn, sn = _dft_tables(seq)
    cos_ctx, nsin_ctx = as_bf16(cn), as_bf16(-sn)
    cos_lat, nsin_lat = _dft_big(dec_seq, GRID_W)

    cvec = jnp.zeros((MOD_ROWS, D_MODEL), F32).at[0].set(c_ctx).at[1:1 + dec_batch].set(c)
    mod = _modulation(cvec, w_ada, b_ada)[:, :1 + dec_batch, None, :]

    xs = _embed(x_sample, _pos_emb_2d(dec_seq))
    x = jnp.concatenate([x_prompt.reshape(t_ctx, D_MODEL), xs.reshape(t_lat, D_MODEL)], axis=0)
    s0 = (_state_to_kernel(state_gla_fwd), _state_to_kernel(state_gla_bwd))

    st_f, st_b = [], []
    for l in range(DEPTH):
        q, k, v, g, la, uc, us = _premix(l, x, mod, mod_sel, vec(g_pre_mix), w_in_p, wa2, ba2, cs_w)
        o_ctx, sf, sb = _gla(l, q, k, v, la, g, g_head_r, consts, 0, seq, batch, 4, emit_state=True)
        (o_lat,) = _gla(l, q, k, v, la, g, g_head_r, consts, t_ctx, dec_seq, dec_batch, 1, s0=s0)
        yf_ctx = _fnet_ctx(uc, us, cos_ctx, nsin_ctx, seq, batch, 4)
        yf_lat = _fnet_lat(uc, us, cos_lat, nsin_lat, t_ctx, dec_seq, dec_batch)
        o = jnp.concatenate([o_ctx, o_lat], axis=0)
        yf = jnp.concatenate([yf_ctx, yf_lat], axis=0)
        x = _post(l, x, o, yf, mod, mod_sel, vec(g_post_mix), vec(g_pre_mlp), vec(g_post_mlp),
                  w_out_b, w_fc1_b, w_fc2_b)
        st_f.append(_state_from_kernel(sf))
        st_b.append(_state_from_kernel(sb))

    y_prompt = x[:t_ctx].reshape(batch, seq, D_MODEL)
    y_sample = x[t_ctx:].reshape(dec_batch, dec_seq, D_MODEL)
    return (y_prompt, y_sample, jnp.stack(st_f, axis=1), jnp.stack(st_b, axis=1))
```

```python
import functools

import numpy as np
import jax
import jax.numpy as jnp
from jax import lax
from jax.experimental import pallas as pl
from jax.experimental.pallas import tpu as pltpu

D_MODEL = 1024
DEPTH = 4
GRID_W = 64
GLA_HEADS = 4
GLA_DK = 64
GLA_DV = 128
CHUNK = 64
GATE_TAU = 16.0
LR_RANK = 16
FNO_GROUPS = 4
FNO_GROUP_W = 128
D_FF = 4 * D_MODEL
EPS = 1e-6
QK_W = GLA_HEADS * GLA_DK
V_W = GLA_HEADS * GLA_DV
FNO_W = FNO_GROUPS * FNO_GROUP_W
D_IN = 2 * QK_W + 2 * V_W + 2 * LR_RANK + FNO_W
U_OFF = 2 * QK_W + 2 * V_W
LR_OFF = U_OFF + FNO_W
HEAD_PAIRS = GLA_HEADS // 2
PAIR_K = 2 * GLA_DK
PAIR_V = 2 * GLA_DV
N_LEVELS = 6
N_GAPS = N_LEVELS + 2
SCAN_UNROLL = 2

VMEM_LIMIT = 48 * 1024 * 1024
ROW_TILE = 512
MOD_ROWS = 8

BF16 = jnp.bfloat16
F32 = jnp.float32


def _gap_matrices():
    c = CHUNK
    m_all = np.zeros((N_GAPS, c, c), np.float32)
    for p in range(c):
        m_all[0, p, :p + 1] = 1.0
        m_all[1, p, p + 1:] = 1.0
    for lv in range(N_LEVELS):
        m = c >> (lv + 1)
        for p in range(c):
            base = (p // (2 * m)) * 2 * m
            ref = base + m - 1
            if p > ref:
                m_all[2 + lv, p, ref + 1:p + 1] = 1.0
            else:
                m_all[2 + lv, p, p + 1:ref + 1] = 1.0
    fwd = m_all.reshape(N_GAPS * c, c)
    bwd = m_all[:, ::-1, ::-1].reshape(N_GAPS * c, c)
    lvl = np.full((c, c), N_LEVELS + 1, np.int32)
    for t in range(c):
        for s in range(t + 1):
            if s == t:
                lvl[t, s] = 0
            else:
                top = (t ^ s).bit_length() - 1
                lvl[t, s] = N_LEVELS - top
    lvl_f = np.concatenate([lvl, lvl], axis=1)
    lvl_b = np.concatenate([lvl[::-1, ::-1], lvl[::-1, ::-1]], axis=1)
    return fwd, bwd, lvl_f, lvl_b


def _dft_tables(n):
    idx = np.arange(n)
    ang = 2.0 * np.pi * ((idx[:, None] * idx[None, :]) % n) / n
    s = 1.0 / np.sqrt(n)
    return (np.cos(ang) * s).astype(np.float32), (np.sin(ang) * s).astype(np.float32)


def _dft_big(n, n1):
    n2 = n // n1
    k = np.arange(n)
    a = 2.0 * np.pi * ((np.arange(n1)[:, None] * n2 * k[None, :]) % n) / n
    b = 2.0 * np.pi * ((np.arange(n2)[:, None] * k[None, :]) % n) / n
    s = 1.0 / np.sqrt(n)
    ca, sa = jnp.asarray(np.cos(a) * s, F32), jnp.asarray(np.sin(a) * s, F32)
    cb, sb = jnp.asarray(np.cos(b), F32), jnp.asarray(np.sin(b), F32)
    cos = ca[:, None, :] * cb[None] - sa[:, None, :] * sb[None]
    nsin = -(sa[:, None, :] * cb[None] + ca[:, None, :] * sb[None])
    return cos.reshape(n, n).astype(BF16), nsin.reshape(n, n).astype(BF16)


def _pos_emb_2d(n_tokens):
    t = jnp.arange(n_tokens)
    r = (t // GRID_W).astype(F32)[:, None]
    col = (t % GRID_W).astype(F32)[:, None]
    nf = D_MODEL // 4
    omega = 1.0 / (10000.0 ** (jnp.arange(nf, dtype=F32) / nf))
    return jnp.concatenate([jnp.sin(r * omega), jnp.cos(r * omega),
                            jnp.sin(col * omega), jnp.cos(col * omega)], axis=-1)


def _silu(x):
    return x * (1.0 / (1.0 + jnp.exp(-x)))


def _rms(x, g):
    return x * lax.rsqrt(jnp.mean(x * x, axis=-1, keepdims=True) + EPS) * g


def _dot(a, b):
    return jnp.dot(a, b, preferred_element_type=F32)


def _dot_nt(a, b):
    return lax.dot_general(a, b, (((1,), (1,)), ((), ())), preferred_element_type=F32)


def _dot_tn(a, b):
    return lax.dot_general(a, b, (((0,), (0,)), ((), ())), preferred_element_type=F32)


def _params(n_grid):
    return pltpu.CompilerParams(dimension_semantics=("arbitrary",) * n_grid,
                                vmem_limit_bytes=VMEM_LIMIT)


def _mod_kernel(c_ref, w_ref, b_ref, o_ref):
    s = _silu(c_ref[...])
    o_ref[...] = jnp.dot(s, w_ref[...], preferred_element_type=F32,
                         precision=lax.Precision.HIGHEST) + b_ref[...]


def _modulation(cvec, w_ada, b_ada):
    tn = 1536
    return pl.pallas_call(
        _mod_kernel,
        grid=(DEPTH, 6 * D_MODEL // tn),
        in_specs=[pl.BlockSpec((MOD_ROWS, D_MODEL), lambda l, j: (0, 0)),
                  pl.BlockSpec((None, D_MODEL, tn), lambda l, j: (l, 0, j)),
                  pl.BlockSpec((None, 1, tn), lambda l, j: (l, 0, j))],
        out_specs=pl.BlockSpec((None, MOD_ROWS, tn), lambda l, j: (l, 0, j)),
        out_shape=jax.ShapeDtypeStruct((DEPTH, MOD_ROWS, 6 * D_MODEL), F32),
        compiler_params=_params(2),
        name="modulation",
    )(cvec, w_ada, b_ada.reshape(DEPTH, 1, 6 * D_MODEL))


def _embed_kernel(x_ref, p_ref, o_ref):
    o_ref[...] = x_ref[...] + p_ref[...]


def _embed(x_sample, pos):
    b, n, d = x_sample.shape
    return pl.pallas_call(
        _embed_kernel,
        grid=(b, n // ROW_TILE),
        in_specs=[pl.BlockSpec((None, ROW_TILE, d), lambda i, j: (i, j, 0)),
                  pl.BlockSpec((ROW_TILE, d), lambda i, j: (j, 0))],
        out_specs=pl.BlockSpec((None, ROW_TILE, d), lambda i, j: (i, j, 0)),
        out_shape=jax.ShapeDtypeStruct(x_sample.shape, x_sample.dtype),
        compiler_params=_params(2),
        name="embed",
    )(x_sample, pos)


def _premix_kernel(x_ref, mod_ref, gpre_ref, win_ref, wa2_ref, ba2_ref, cs_ref,
                   q_ref, k_ref, v_ref, g_ref, la_ref, uc_ref, us_ref):
    x = x_ref[...]
    sh = mod_ref[:, 0:D_MODEL]
    sc = mod_ref[:, D_MODEL:2 * D_MODEL]
    h = (_rms(x, gpre_ref[...]) * (1.0 + sc) + sh).astype(BF16)
    q_ref[...] = _dot(h, win_ref[:, 0:QK_W]) * (GLA_DK ** -0.5)
    k_ref[...] = _dot(h, win_ref[:, QK_W:2 * QK_W])
    v_ref[...] = _dot(h, win_ref[:, 2 * QK_W:2 * QK_W + V_W]).astype(BF16)
    g_ref[...] = _dot(h, win_ref[:, 2 * QK_W + V_W:U_OFF]).astype(BF16)
    lr = _dot(h, win_ref[:, LR_OFF:LR_OFF + 2 * LR_RANK]).astype(BF16)
    logit = _dot(lr, wa2_ref[...]) + ba2_ref[...]
    softplus_neg = jnp.maximum(-logit, 0.0) + jnp.log(1.0 + jnp.exp(-jnp.abs(logit)))
    la_ref[...] = softplus_neg * (-1.0 / GATE_TAU)
    for grp in range(FNO_GROUPS):
        lo = U_OFF + grp * FNO_GROUP_W
        u = _dot(h, win_ref[:, lo:lo + FNO_GROUP_W]).astype(BF16)
        ucs = _dot(u, cs_ref[...])
        uc_ref[:, grp * FNO_GROUP_W:(grp + 1) * FNO_GROUP_W] = ucs[:, :FNO_GROUP_W].astype(BF16)
        us_ref[:, grp * FNO_GROUP_W:(grp + 1) * FNO_GROUP_W] = ucs[:, FNO_GROUP_W:].astype(BF16)


def _premix(layer, x, mod, mod_sel, g_pre, w_in, wa2, ba2, cs_w):
    t = x.shape[0]
    row = lambda w: pl.BlockSpec((ROW_TILE, w), lambda i: (i, 0))
    lay = lambda *shape: pl.BlockSpec((None,) + shape, lambda i: (layer,) + (0,) * len(shape))
    out = lambda w, dt: jax.ShapeDtypeStruct((t, w), dt)
    return pl.pallas_call(
        _premix_kernel,
        grid=(t // ROW_TILE,),
        in_specs=[row(D_MODEL),
                  pl.BlockSpec((None, None, 1, 6 * D_MODEL), lambda i: (layer, mod_sel(i), 0, 0)),
                  lay(1, D_MODEL), lay(D_MODEL, D_IN), lay(2 * LR_RANK, 2 * QK_W), lay(1, 2 * QK_W),
                  pl.BlockSpec((FNO_GROUP_W, 2 * FNO_GROUP_W), lambda i: (0, 0))],
        out_specs=[row(QK_W), row(QK_W), row(V_W), row(V_W), row(2 * QK_W), row(FNO_W), row(FNO_W)],
        out_shape=[out(QK_W, F32), out(QK_W, F32), out(V_W, BF16), out(V_W, BF16),
                   out(2 * QK_W, F32), out(FNO_W, BF16), out(FNO_W, BF16)],
        compiler_params=_params(1),
        name="premix",
    )(x, mod, g_pre, w_in, wa2, ba2, cs_w)


def _gla_chunk(r0, forward, q_ref, k_ref, v_ref, la_ref, gap_ref, lvl_ref, st_ref):
    rows = pl.ds(r0, CHUNK)
    lane = lax.broadcasted_iota(jnp.int32, (CHUNK, PAIR_K), 1)
    head0 = lane < GLA_DK
    la = la_ref[rows, :]
    la_hi = la.astype(BF16)
    la_lo = (la - la_hi.astype(F32)).astype(BF16)
    gaps = _dot(gap_ref[...], jnp.concatenate([la_hi, la_lo], axis=1))
    decay = jnp.exp(gaps[:, :PAIR_K] + gaps[:, PAIR_K:])
    blk = lambda j: decay[j * CHUNK:(j + 1) * CHUNK]
    q = q_ref[rows, :]
    k = k_ref[rows, :]
    k_h = (jnp.where(head0, k, 0.0), jnp.where(head0, 0.0, k))
    v = v_ref[rows, :]
    lvl = lvl_ref[...]

    a = jnp.zeros((CHUNK, 2 * CHUNK), F32)
    for lv in range(N_LEVELS + 1):
        if lv == 0:
            ql = q.astype(BF16)
            kl = jnp.concatenate([k_h[0].astype(BF16), k_h[1].astype(BF16)], axis=0)
        else:
            d = blk(1 + lv)
            ql = (q * d).astype(BF16)
            kl = jnp.concatenate([(k_h[0] * d).astype(BF16), (k_h[1] * d).astype(BF16)], axis=0)
        a = jnp.where(lvl == lv, _dot_nt(ql, kl), a)

    q_in = (q * blk(0)).astype(BF16)
    st = jnp.concatenate([st_ref[0].astype(BF16), st_ref[1].astype(BF16)], axis=0)
    inter = _dot_nt(q_in, st)
    vlane = lax.broadcasted_iota(jnp.int32, (CHUNK, PAIR_V), 1)
    zero = jnp.zeros_like(v)
    v_blk = jnp.concatenate([jnp.where(vlane < GLA_DV, v, zero),
                             jnp.where(vlane < GLA_DV, zero, v)], axis=0)
    o = inter + _dot(a.astype(BF16), v_blk)

    d_out = blk(1)
    total = blk(0)[CHUNK - 1:CHUNK] if forward else blk(0)[0:1]
    for hd in range(2):
        k_out = (k_h[hd] * d_out).astype(BF16)
        upd = _dot_tn(v[:, hd * GLA_DV:(hd + 1) * GLA_DV], k_out)
        st_ref[hd] = st_ref[hd] * total + upd
    return o


def _gla_kernel(*refs, seq_len, n_seq, has_state, emit_state, n_alias):
    (q_ref, k_ref, v_ref, laf_ref, lab_ref, g_ref, gh_ref,
     gapf_ref, gapb_ref, lvlf_ref, lvlb_ref) = refs[:11]
    pos = 11
    if has_state:
        s0f_ref, s0b_ref = refs[pos:pos + 2]
        pos += 2
    pos += n_alias
    o_ref = refs[pos]
    pos += 1
    if emit_state:
        sf_ref, sb_ref = refs[pos:pos + 2]
        pos += 2
    acc_ref, stf_ref, stb_ref = refs[pos:pos + 3]
    n_chunks = seq_len // CHUNK
    half_iters = n_chunks // (2 * SCAN_UNROLL)
    fwd_refs = (q_ref, k_ref, v_ref, laf_ref, gapf_ref, lvlf_ref, stf_ref)
    bwd_refs = (q_ref, k_ref, v_ref, lab_ref, gapb_ref, lvlb_ref, stb_ref)

    def finish(o, rows):
        g = g_ref[rows, :].astype(F32)
        for hd in range(2):
            sl = slice(hd * GLA_DV, (hd + 1) * GLA_DV)
            oh = _rms(o[:, sl], gh_ref[:, sl]) * _silu(g[:, sl])
            o_ref[rows, sl] = oh.astype(o_ref.dtype)

    def seq_body(sq, carry):
        base = sq * seq_len
        if has_state:
            stf_ref[...] = s0f_ref[...]
            stb_ref[...] = s0b_ref[...]
        else:
            stf_ref[...] = jnp.zeros_like(stf_ref)
            stb_ref[...] = jnp.zeros_like(stb_ref)

        def step(it, second_visit):
            for u in range(SCAN_UNROLL):
                cf = it * SCAN_UNROLL + u
                for forward, ci, srefs in ((True, cf, fwd_refs), (False, n_chunks - 1 - cf, bwd_refs)):
                    r0 = pl.multiple_of(base + ci * CHUNK, CHUNK)
                    o = _gla_chunk(r0, forward, *srefs)
                    rows = pl.ds(r0, CHUNK)
                    if second_visit:
                        finish(o + acc_ref[rows, :], rows)
                    else:
                        acc_ref[rows, :] = o

        def first(i, c):
            step(i, False)
            return c

        def second(i, c):
            step(half_iters + i, True)
            return c

        lax.fori_loop(0, half_iters, first, 0)
        lax.fori_loop(0, half_iters, second, 0)
        if emit_state:
            for s_out, st_ref in ((sf_ref, stf_ref), (sb_ref, stb_ref)):
                for hd in range(2):
                    s_out[sq, hd] = st_ref[hd].T[hd * GLA_DK:(hd + 1) * GLA_DK, :]
        return carry

    lax.fori_loop(0, n_seq, seq_body, 0)


def _gla(layer, q, k, v, la, g, g_head, consts, row0, seq_len, n_seq_total, n_seq, t_all,
         s0=None, o_prev=None, st_prev=None, emit_state=False):
    gapf, gapb, lvlf, lvlb = consts
    rows = seq_len * n_seq
    blk0 = row0 // rows
    n_steps = n_seq_total // n_seq
    rspec = lambda w, off=0: pl.BlockSpec((rows, w), lambda i, j: (blk0 + i, j + off))
    const = lambda a: pl.BlockSpec(a.shape, lambda i, j: (0,) * a.ndim)
    in_specs = [rspec(PAIR_K), rspec(PAIR_K), rspec(PAIR_V), rspec(PAIR_K), rspec(PAIR_K, HEAD_PAIRS),
                rspec(PAIR_V),
                pl.BlockSpec((None, 1, PAIR_V), lambda i, j: (layer, 0, j)),
                const(gapf), const(gapb), const(lvlf), const(lvlb)]
    args = [q, k, v, la, la, g, g_head, gapf, gapb, lvlf, lvlb]
    if s0 is not None:
        sspec = pl.BlockSpec((None, None, 2, GLA_DV, PAIR_K), lambda i, j: (i, layer, j, 0, 0))
        in_specs += [sspec, sspec]
        args += list(s0)
    aliases = {}
    prev = ([(o_prev, 0)] if o_prev is not None else []) + (
        [(a, 1 + n) for n, a in enumerate(st_prev)] if st_prev else [])
    for a, out_idx in prev:
        aliases[len(args)] = out_idx
        in_specs.append(pl.BlockSpec(memory_space=pl.ANY))
        args.append(a)
    out_specs = [rspec(PAIR_V)]
    out_shape = [jax.ShapeDtypeStruct((t_all, V_W), BF16)]
    if emit_state:
        st_spec = pl.BlockSpec((n_seq, None, 2, GLA_DK, GLA_DV), lambda i, j: (i, layer, j, 0, 0))
        st_shape = jax.ShapeDtypeStruct((n_seq_total, DEPTH, GLA_HEADS, GLA_DK, GLA_DV), F32)
        out_specs += [st_spec, st_spec]
        out_shape += [st_shape, st_shape]
    kern = functools.partial(_gla_kernel, seq_len=seq_len, n_seq=n_seq, has_state=s0 is not None,
                             emit_state=emit_state, n_alias=len(prev))
    return pl.pallas_call(
        kern,
        grid=(n_steps, HEAD_PAIRS),
        in_specs=in_specs,
        out_specs=out_specs,
        out_shape=out_shape,
        input_output_aliases=aliases,
        scratch_shapes=[pltpu.VMEM((rows, PAIR_V), F32), pltpu.VMEM((2, GLA_DV, PAIR_K), F32),
                        pltpu.VMEM((2, GLA_DV, PAIR_K), F32)],
        compiler_params=_params(2),
        name="gla_state" if emit_state else "gla",
    )(*args)


def _fnet_kernel(c_ref, ns_ref, uc_ref, us_ref, o_ref, *, seq_len, n_seq):
    for sq in range(n_seq):
        rows = slice(sq * seq_len, (sq + 1) * seq_len)
        y = _dot(c_ref[...], uc_ref[rows, :]) + _dot(ns_ref[...], us_ref[rows, :])
        o_ref[rows, :] = y.astype(o_ref.dtype)


def _fnet_ctx(uc, us, cos_n, nsin_n, seq_len, n_seq_total, n_seq, t_all):
    rows = seq_len * n_seq
    rspec = pl.BlockSpec((rows, FNO_W), lambda i: (i, 0))
    cspec = pl.BlockSpec((seq_len, seq_len), lambda i: (0, 0))
    return pl.pallas_call(
        functools.partial(_fnet_kernel, seq_len=seq_len, n_seq=n_seq),
        grid=(n_seq_total // n_seq,),
        in_specs=[cspec, cspec, rspec, rspec],
        out_specs=rspec,
        out_shape=jax.ShapeDtypeStruct((t_all, FNO_W), BF16),
        compiler_params=_params(1),
        name="fnet_ctx",
    )(cos_n, nsin_n, uc, us)


def _fnet_lat_kernel(c_ref, ns_ref, uc_ref, us_ref, prev_ref, o_ref):
    del prev_ref
    y = _dot(c_ref[...], uc_ref[...]) + _dot(ns_ref[...], us_ref[...])
    o_ref[...] = y.astype(o_ref.dtype)


def _fnet_lat(uc, us, cos_n, nsin_n, yf_prev, row0, seq_len, n_seq_total):
    tr = 256
    blk0 = row0 // seq_len
    mspec = pl.BlockSpec((tr, seq_len), lambda b, i: (i, 0))
    uspec = pl.BlockSpec((seq_len, FNO_W), lambda b, i: (blk0 + b, 0))
    n_tiles = seq_len // tr
    return pl.pallas_call(
        _fnet_lat_kernel,
        grid=(n_seq_total, n_tiles),
        in_specs=[mspec, mspec, uspec, uspec, pl.BlockSpec(memory_space=pl.ANY)],
        out_specs=pl.BlockSpec((tr, FNO_W), lambda b, i: (row0 // tr + b * n_tiles + i, 0)),
        out_shape=jax.ShapeDtypeStruct(yf_prev.shape, BF16),
        input_output_aliases={4: 0},
        compiler_params=_params(2),
        name="fnet_lat",
    )(cos_n, nsin_n, uc, us, yf_prev)


def _post_kernel(x_ref, o_ref, yf_ref, mod_ref, gpm_ref, gpre_ref, gpost_ref,
                 wout_ref, w1_ref, w2_ref, out_ref):
    gt_m = mod_ref[:, 2 * D_MODEL:3 * D_MODEL]
    sh_f = mod_ref[:, 3 * D_MODEL:4 * D_MODEL]
    sc_f = mod_ref[:, 4 * D_MODEL:5 * D_MODEL]
    gt_f = mod_ref[:, 5 * D_MODEL:6 * D_MODEL]
    y = _dot(o_ref[...], wout_ref[0:V_W, :]) + _dot(yf_ref[...], wout_ref[V_W:, :])
    x = x_ref[...] + gt_m * _rms(y, gpm_ref[...])
    h = (_rms(x, gpre_ref[...]) * (1.0 + sc_f) + sh_f).astype(BF16)
    f = jnp.zeros((ROW_TILE, D_MODEL), F32)
    n_split = 4
    wf = D_FF // n_split
    for j in range(n_split):
        a = jnp.maximum(_dot(h, w1_ref[:, j * wf:(j + 1) * wf]), 0.0)
        f = f + _dot((a * a).astype(BF16), w2_ref[j * wf:(j + 1) * wf, :])
    out_ref[...] = x + gt_f * _rms(f, gpost_ref[...])


def _post(layer, x, o, yf, mod, mod_sel, g_post_mix, g_pre_mlp, g_post_mlp, w_out, w_fc1, w_fc2):
    t = x.shape[0]
    row = lambda w: pl.BlockSpec((ROW_TILE, w), lambda i: (i, 0))
    lay = lambda *shape: pl.BlockSpec((None,) + shape, lambda i: (layer,) + (0,) * len(shape),
                                      pipeline_mode=pl.Buffered(1))
    return pl.pallas_call(
        _post_kernel,
        grid=(t // ROW_TILE,),
        in_specs=[row(D_MODEL), row(V_W), row(FNO_W),
                  pl.BlockSpec((None, None, 1, 6 * D_MODEL), lambda i: (layer, mod_sel(i), 0, 0)),
                  lay(1, D_MODEL), lay(1, D_MODEL), lay(1, D_MODEL),
                  lay(V_W + FNO_W, D_MODEL), lay(D_MODEL, D_FF), lay(D_FF, D_MODEL)],
        out_specs=row(D_MODEL),
        out_shape=jax.ShapeDtypeStruct((t, D_MODEL), F32),
        compiler_params=_params(1),
        name="post",
    )(x, o, yf, mod, g_post_mix, g_pre_mlp, g_post_mlp, w_out, w_fc1, w_fc2)


def _state_to_kernel(s):
    st = jnp.swapaxes(s, -1, -2)
    z = jnp.zeros_like(st)
    even = jnp.concatenate([st, z], axis=-1)
    odd = jnp.concatenate([z, st], axis=-1)
    is_even = (jnp.arange(GLA_HEADS) % 2 == 0)[None, None, :, None, None]
    return jnp.where(is_even, even, odd)


def kernel(x_prompt, x_sample, state_gla_fwd, state_gla_bwd, c, c_ctx, w_in, w_out, w_a2, b_a2,
           g_head, w_ada, b_ada, g_pre_mix, g_post_mix, g_pre_mlp, g_post_mlp, w_fc1, w_fc2):
    batch, seq, _ = x_prompt.shape
    dec_batch, dec_seq, _ = x_sample.shape
    t_ctx = batch * seq
    t_lat = dec_batch * dec_seq
    t_all = t_ctx + t_lat
    ctx_tiles = t_ctx // ROW_TILE
    lat_tiles = dec_seq // ROW_TILE

    def mod_sel(i):
        return jnp.where(i < ctx_tiles, 0, 1 + (i - ctx_tiles) // lat_tiles)

    w_in_p = jnp.concatenate([w_in[:, :, :U_OFF], w_in[:, :, U_OFF + 2 * LR_RANK:],
                              w_in[:, :, U_OFF:U_OFF + 2 * LR_RANK]], axis=-1).astype(BF16)
    za = jnp.zeros((DEPTH, LR_RANK, QK_W), F32)
    wa2 = jnp.concatenate([jnp.concatenate([w_a2[:, 0], za], axis=-1),
                           jnp.concatenate([za, w_a2[:, 1]], axis=-1)], axis=1).astype(BF16)
    ba2 = b_a2.reshape(DEPTH, 1, 2 * QK_W)
    w_out_b = w_out.astype(BF16)
    w_fc1_b = w_fc1.astype(BF16)
    w_fc2_b = w_fc2.astype(BF16)
    vec = lambda a: a.reshape(DEPTH, 1, -1)
    g_head_r = g_head.reshape(DEPTH, 1, V_W)

    gapf, gapb, lvlf, lvlb = _gap_matrices()
    as_bf16 = lambda a: jnp.asarray(a, F32).astype(BF16)
    consts = (as_bf16(gapf), as_bf16(gapb), jnp.asarray(lvlf), jnp.asarray(lvlb))
    cw, sw = _dft_tables(FNO_GROUP_W)
    cs_w = as_bf16(np.concatenate([cw, sw], axis=1))
    cn, sn = _dft_tables(seq)
    cos_ctx, nsin_ctx = as_bf16(cn), as_bf16(-sn)
    cos_lat, nsin_lat = _dft_big(dec_seq, GRID_W)

    cvec = jnp.zeros((MOD_ROWS, D_MODEL), F32).at[0].set(c_ctx).at[1:1 + dec_batch].set(c)
    mod = _modulation(cvec, w_ada, b_ada)[:, :1 + dec_batch, None, :]

    xs = _embed(x_sample, _pos_emb_2d(dec_seq))
    x = jnp.concatenate([x_prompt.reshape(t_ctx, D_MODEL), xs.reshape(t_lat, D_MODEL)], axis=0)
    s0 = (_state_to_kernel(state_gla_fwd), _state_to_kernel(state_gla_bwd))

    states = None
    for l in range(DEPTH):
        q, k, v, g, la, uc, us = _premix(l, x, mod, mod_sel, vec(g_pre_mix), w_in_p, wa2, ba2, cs_w)
        o, *states = _gla(l, q, k, v, la, g, g_head_r, consts, 0, seq, batch, 4, t_all,
                          st_prev=states, emit_state=True)
        (o,) = _gla(l, q, k, v, la, g, g_head_r, consts, t_ctx, dec_seq, dec_batch, 1, t_all,
                    s0=s0, o_prev=o)
        yf = _fnet_ctx(uc, us, cos_ctx, nsin_ctx, seq, batch, 4, t_all)
        yf = _fnet_lat(uc, us, cos_lat, nsin_lat, yf, t_ctx, dec_seq, dec_batch)
        x = _post(l, x, o, yf, mod, mod_sel, vec(g_post_mix), vec(g_pre_mlp), vec(g_post_mlp),
                  w_out_b, w_fc1_b, w_fc2_b)

    y_prompt = x[:t_ctx].reshape(batch, seq, D_MODEL)
    y_sample = x[t_ctx:].reshape(dec_batch, dec_seq, D_MODEL)
    return (y_prompt, y_sample, states[0], states[1])
```

```python
import functools

import numpy as np
import jax
import jax.numpy as jnp
from jax import lax
from jax.experimental import pallas as pl
from jax.experimental.pallas import tpu as pltpu

D_MODEL = 1024
DEPTH = 4
GRID_W = 64
GLA_HEADS = 4
GLA_DK = 64
GLA_DV = 128
CHUNK = 64
GATE_TAU = 16.0
LR_RANK = 16
FNO_GROUPS = 4
FNO_GROUP_W = 128
D_FF = 4 * D_MODEL
EPS = 1e-6
QK_W = GLA_HEADS * GLA_DK
V_W = GLA_HEADS * GLA_DV
FNO_W = FNO_GROUPS * FNO_GROUP_W
D_IN = 2 * QK_W + 2 * V_W + 2 * LR_RANK + FNO_W
U_OFF = 2 * QK_W + 2 * V_W
LR_OFF = U_OFF + FNO_W
HEAD_PAIRS = GLA_HEADS // 2
PAIR_K = 2 * GLA_DK
PAIR_V = 2 * GLA_DV
N_LEVELS = 6
N_GAPS = N_LEVELS + 2
SCAN_UNROLL = 8
FAST_STEP_LIMIT = 1.0

VMEM_LIMIT = 48 * 1024 * 1024
ROW_TILE = 512
MOD_ROWS = 8

BF16 = jnp.bfloat16
F32 = jnp.float32


def _gap_matrices():
    c = CHUNK
    m_all = np.zeros((N_GAPS, c, c), np.float32)
    for p in range(c):
        m_all[0, p, :p + 1] = 1.0
        m_all[1, p, p + 1:] = 1.0
    for lv in range(N_LEVELS):
        m = c >> (lv + 1)
        for p in range(c):
            base = (p // (2 * m)) * 2 * m
            ref = base + m - 1
            if p > ref:
                m_all[2 + lv, p, ref + 1:p + 1] = 1.0
            else:
                m_all[2 + lv, p, p + 1:ref + 1] = 1.0
    fwd = m_all.reshape(N_GAPS * c, c)
    bwd = m_all[:, ::-1, ::-1].reshape(N_GAPS * c, c)
    lvl = np.full((c, c), N_LEVELS + 1, np.int32)
    for t in range(c):
        for s in range(t + 1):
            if s == t:
                lvl[t, s] = 0
            else:
                top = (t ^ s).bit_length() - 1
                lvl[t, s] = N_LEVELS - top
    lvl_f = np.concatenate([lvl, lvl], axis=1)
    lvl_b = np.concatenate([lvl[::-1, ::-1], lvl[::-1, ::-1]], axis=1)
    return fwd, bwd, lvl_f, lvl_b


def _dft_tables(n):
    idx = np.arange(n)
    ang = 2.0 * np.pi * ((idx[:, None] * idx[None, :]) % n) / n
    s = 1.0 / np.sqrt(n)
    return (np.cos(ang) * s).astype(np.float32), (np.sin(ang) * s).astype(np.float32)


def _dft_big(n, n1):
    n2 = n // n1
    k = np.arange(n)
    a = 2.0 * np.pi * ((np.arange(n1)[:, None] * n2 * k[None, :]) % n) / n
    b = 2.0 * np.pi * ((np.arange(n2)[:, None] * k[None, :]) % n) / n
    s = 1.0 / np.sqrt(n)
    ca, sa = jnp.asarray(np.cos(a) * s, F32), jnp.asarray(np.sin(a) * s, F32)
    cb, sb = jnp.asarray(np.cos(b), F32), jnp.asarray(np.sin(b), F32)
    cos = ca[:, None, :] * cb[None] - sa[:, None, :] * sb[None]
    nsin = -(sa[:, None, :] * cb[None] + ca[:, None, :] * sb[None])
    return cos.reshape(n, n).astype(BF16), nsin.reshape(n, n).astype(BF16)


def _pos_emb_2d(n_tokens):
    t = jnp.arange(n_tokens)
    r = (t // GRID_W).astype(F32)[:, None]
    col = (t % GRID_W).astype(F32)[:, None]
    nf = D_MODEL // 4
    omega = 1.0 / (10000.0 ** (jnp.arange(nf, dtype=F32) / nf))
    return jnp.concatenate([jnp.sin(r * omega), jnp.cos(r * omega),
                            jnp.sin(col * omega), jnp.cos(col * omega)], axis=-1)


def _silu(x):
    return x * (1.0 / (1.0 + jnp.exp(-x)))


def _rms(x, g):
    return x * lax.rsqrt(jnp.mean(x * x, axis=-1, keepdims=True) + EPS) * g


def _dot(a, b):
    return jnp.dot(a, b, preferred_element_type=F32)


def _dot_nt(a, b):
    return lax.dot_general(a, b, (((1,), (1,)), ((), ())), preferred_element_type=F32)


def _dot_tn(a, b):
    return lax.dot_general(a, b, (((0,), (0,)), ((), ())), preferred_element_type=F32)


def _params(n_grid):
    return pltpu.CompilerParams(dimension_semantics=("arbitrary",) * n_grid,
                                vmem_limit_bytes=VMEM_LIMIT)


def _mod_kernel(c_ref, w_ref, b_ref, o_ref):
    s = _silu(c_ref[...])
    o_ref[...] = jnp.dot(s, w_ref[...], preferred_element_type=F32,
                         precision=lax.Precision.HIGHEST) + b_ref[...]


def _modulation(cvec, w_ada, b_ada):
    tn = 1536
    return pl.pallas_call(
        _mod_kernel,
        grid=(DEPTH, 6 * D_MODEL // tn),
        in_specs=[pl.BlockSpec((MOD_ROWS, D_MODEL), lambda l, j: (0, 0)),
                  pl.BlockSpec((None, D_MODEL, tn), lambda l, j: (l, 0, j)),
                  pl.BlockSpec((None, 1, tn), lambda l, j: (l, 0, j))],
        out_specs=pl.BlockSpec((None, MOD_ROWS, tn), lambda l, j: (l, 0, j)),
        out_shape=jax.ShapeDtypeStruct((DEPTH, MOD_ROWS, 6 * D_MODEL), F32),
        compiler_params=_params(2),
        name="modulation",
    )(cvec, w_ada, b_ada.reshape(DEPTH, 1, 6 * D_MODEL))


def _embed_kernel(x_ref, p_ref, o_ref):
    o_ref[...] = x_ref[...] + p_ref[...]


def _embed(x_sample, pos):
    b, n, d = x_sample.shape
    return pl.pallas_call(
        _embed_kernel,
        grid=(b, n // ROW_TILE),
        in_specs=[pl.BlockSpec((None, ROW_TILE, d), lambda i, j: (i, j, 0)),
                  pl.BlockSpec((ROW_TILE, d), lambda i, j: (j, 0))],
        out_specs=pl.BlockSpec((None, ROW_TILE, d), lambda i, j: (i, j, 0)),
        out_shape=jax.ShapeDtypeStruct(x_sample.shape, x_sample.dtype),
        compiler_params=_params(2),
        name="embed",
    )(x_sample, pos)


def _premix_kernel(x_ref, mod_ref, gpre_ref, win_ref, wa2_ref, ba2_ref, cs_ref,
                   q_ref, k_ref, v_ref, g_ref, la_ref, uc_ref, us_ref):
    x = x_ref[...]
    sh = mod_ref[:, 0:D_MODEL]
    sc = mod_ref[:, D_MODEL:2 * D_MODEL]
    h = (_rms(x, gpre_ref[...]) * (1.0 + sc) + sh).astype(BF16)
    q_ref[...] = _dot(h, win_ref[:, 0:QK_W]) * (GLA_DK ** -0.5)
    k_ref[...] = _dot(h, win_ref[:, QK_W:2 * QK_W])
    v_ref[...] = _dot(h, win_ref[:, 2 * QK_W:2 * QK_W + V_W]).astype(BF16)
    g_ref[...] = _dot(h, win_ref[:, 2 * QK_W + V_W:U_OFF]).astype(BF16)
    lr = _dot(h, win_ref[:, LR_OFF:LR_OFF + 2 * LR_RANK]).astype(BF16)
    logit = _dot(lr, wa2_ref[...]) + ba2_ref[...]
    softplus_neg = jnp.maximum(-logit, 0.0) + jnp.log(1.0 + jnp.exp(-jnp.abs(logit)))
    la_ref[...] = softplus_neg * (-1.0 / GATE_TAU)
    for grp in range(FNO_GROUPS):
        lo = U_OFF + grp * FNO_GROUP_W
        u = _dot(h, win_ref[:, lo:lo + FNO_GROUP_W]).astype(BF16)
        ucs = _dot(u, cs_ref[...])
        uc_ref[:, grp * FNO_GROUP_W:(grp + 1) * FNO_GROUP_W] = ucs[:, :FNO_GROUP_W].astype(BF16)
        us_ref[:, grp * FNO_GROUP_W:(grp + 1) * FNO_GROUP_W] = ucs[:, FNO_GROUP_W:].astype(BF16)


def _premix(layer, x, mod, mod_sel, g_pre, w_in, wa2, ba2, cs_w):
    t = x.shape[0]
    row = lambda w: pl.BlockSpec((ROW_TILE, w), lambda i: (i, 0))
    lay = lambda *shape: pl.BlockSpec((None,) + shape, lambda i: (layer,) + (0,) * len(shape))
    out = lambda w, dt: jax.ShapeDtypeStruct((t, w), dt)
    return pl.pallas_call(
        _premix_kernel,
        grid=(t // ROW_TILE,),
        in_specs=[row(D_MODEL),
                  pl.BlockSpec((None, None, 1, 6 * D_MODEL), lambda i: (layer, mod_sel(i), 0, 0)),
                  lay(1, D_MODEL), lay(D_MODEL, D_IN), lay(2 * LR_RANK, 2 * QK_W), lay(1, 2 * QK_W),
                  pl.BlockSpec((FNO_GROUP_W, 2 * FNO_GROUP_W), lambda i: (0, 0))],
        out_specs=[row(QK_W), row(QK_W), row(V_W), row(V_W), row(2 * QK_W), row(FNO_W), row(FNO_W)],
        out_shape=[out(QK_W, F32), out(QK_W, F32), out(V_W, BF16), out(V_W, BF16),
                   out(2 * QK_W, F32), out(FNO_W, BF16), out(FNO_W, BF16)],
        compiler_params=_params(1),
        name="premix",
    )(x, mod, g_pre, w_in, wa2, ba2, cs_w)


def _gla_chunk(r0, forward, q_ref, k_ref, v_ref, la_ref, gap_ref, lvl_ref, st_ref):
    rows = pl.ds(r0, CHUNK)
    lane = lax.broadcasted_iota(jnp.int32, (CHUNK, PAIR_K), 1)
    head0 = lane < GLA_DK
    la = la_ref[rows, :]
    la_hi = la.astype(BF16)
    la_lo = (la - la_hi.astype(F32)).astype(BF16)
    gaps = _dot(gap_ref[...], jnp.concatenate([la_hi, la_lo], axis=1))
    decay = jnp.exp(gaps[:, :PAIR_K] + gaps[:, PAIR_K:])
    blk = lambda j: decay[j * CHUNK:(j + 1) * CHUNK]
    q = q_ref[rows, :]
    k = k_ref[rows, :]
    k_h = (jnp.where(head0, k, 0.0), jnp.where(head0, 0.0, k))
    v = v_ref[rows, :]
    lvl = lvl_ref[...]

    a = jnp.zeros((CHUNK, 2 * CHUNK), F32)
    for lv in range(N_LEVELS + 1):
        if lv == 0:
            ql = q.astype(BF16)
            kl = jnp.concatenate([k_h[0].astype(BF16), k_h[1].astype(BF16)], axis=0)
        else:
            d = blk(1 + lv)
            ql = (q * d).astype(BF16)
            kl = jnp.concatenate([(k_h[0] * d).astype(BF16), (k_h[1] * d).astype(BF16)], axis=0)
        a = jnp.where(lvl == lv, _dot_nt(ql, kl), a)

    q_in = (q * blk(0)).astype(BF16)
    st = jnp.concatenate([st_ref[0].astype(BF16), st_ref[1].astype(BF16)], axis=0)
    inter = _dot_nt(q_in, st)
    vlane = lax.broadcasted_iota(jnp.int32, (CHUNK, PAIR_V), 1)
    zero = jnp.zeros_like(v)
    v_blk = jnp.concatenate([jnp.where(vlane < GLA_DV, v, zero),
                             jnp.where(vlane < GLA_DV, zero, v)], axis=0)
    o = inter + _dot(a.astype(BF16), v_blk)

    d_out = blk(1)
    total = blk(0)[CHUNK - 1:CHUNK] if forward else blk(0)[0:1]
    for hd in range(2):
        k_out = (k_h[hd] * d_out).astype(BF16)
        upd = _dot_tn(v[:, hd * GLA_DV:(hd + 1) * GLA_DV], k_out)
        st_ref[hd] = st_ref[hd] * total + upd
    return o


def _gla_fast_step(jobs, fwd_refs, bwd_refs):
    refs = [fwd_refs if fw else bwd_refs for _, fw, _ in jobs]
    rows = [pl.ds(r0, CHUNK) for r0, _, _ in jobs]
    lane = lax.broadcasted_iota(jnp.int32, (CHUNK, PAIR_K), 1)
    head0 = lane < GLA_DK
    vlane = lax.broadcasted_iota(jnp.int32, (CHUNK, PAIR_V), 1) < GLA_DV

    la = [r[3][rw, :] for r, rw in zip(refs, rows)]
    la_hi = [x.astype(BF16) for x in la]
    la_lo = [(x - h.astype(F32)).astype(BF16) for x, h in zip(la, la_hi)]
    cum = [_dot(r[4][0:CHUNK, :], jnp.concatenate([h, l], axis=1))
           for r, h, l in zip(refs, la_hi, la_lo)]
    b = [x[:, :PAIR_K] + x[:, PAIR_K:] for x in cum]
    b_end = [x[CHUNK - 1:CHUNK] if fw else x[0:1] for x, (_, fw, _) in zip(b, jobs)]
    q = [r[0][rw, :] for r, rw in zip(refs, rows)]
    k = [r[1][rw, :] for r, rw in zip(refs, rows)]
    v = [r[2][rw, :] for r, rw in zip(refs, rows)]
    qd = [(x * jnp.exp(y)).astype(BF16) for x, y in zip(q, b)]
    kd = [x * jnp.exp(-y) for x, y in zip(k, b)]
    ko = [x * jnp.exp(e - y) for x, y, e in zip(k, b, b_end)]
    total = [jnp.exp(e) for e in b_end]
    kd_cat = [jnp.concatenate([jnp.where(head0, x, 0.0).astype(BF16),
                               jnp.where(head0, 0.0, x).astype(BF16)], axis=0) for x in kd]
    score = [_dot_nt(x, y) for x, y in zip(qd, kd_cat)]
    a = [jnp.where(r[5][...] <= N_LEVELS, s, 0.0).astype(BF16) for r, s in zip(refs, score)]
    v_blk = [jnp.concatenate([jnp.where(vlane, x, jnp.zeros_like(x)),
                              jnp.where(vlane, jnp.zeros_like(x), x)], axis=0) for x in v]
    intra = [_dot(x, y) for x, y in zip(a, v_blk)]
    ko_h = [(jnp.where(head0, x, 0.0).astype(BF16), jnp.where(head0, 0.0, x).astype(BF16)) for x in ko]
    upd = [[_dot_tn(x[:, hd * GLA_DV:(hd + 1) * GLA_DV], y[hd]) for hd in range(2)]
           for x, y in zip(v, ko_h)]

    outs = [None] * len(jobs)
    for scan in sorted({(fw, slot) for _, fw, slot in jobs}):
        st_ref = (fwd_refs if scan[0] else bwd_refs)[6]
        st = [st_ref[scan[1], 0], st_ref[scan[1], 1]]
        for j, (_, fw, slot) in enumerate(jobs):
            if (fw, slot) != scan:
                continue
            st_cat = jnp.concatenate([st[0].astype(BF16), st[1].astype(BF16)], axis=0)
            outs[j] = _dot_nt(qd[j], st_cat) + intra[j]
            st = [st[hd] * total[j] + upd[j][hd] for hd in range(2)]
        st_ref[scan[1], 0] = st[0]
        st_ref[scan[1], 1] = st[1]
    return outs


def _gla_kernel(*refs, seq_len, n_seq, has_state, emit_state, n_alias, unroll, par):
    (q_ref, k_ref, v_ref, laf_ref, lab_ref, g_ref, gh_ref,
     gapf_ref, gapb_ref, lvlf_ref, lvlb_ref) = refs[:11]
    pos = 11
    if has_state:
        s0f_ref, s0b_ref = refs[pos:pos + 2]
        pos += 2
    pos += n_alias
    o_ref = refs[pos]
    pos += 1
    if emit_state:
        sf_ref, sb_ref = refs[pos:pos + 2]
        pos += 2
    acc_ref, stf_ref, stb_ref = refs[pos:pos + 3]
    n_chunks = seq_len // CHUNK
    half_iters = n_chunks // (2 * unroll)
    fwd_refs = (q_ref, k_ref, v_ref, laf_ref, gapf_ref, lvlf_ref, stf_ref)
    bwd_refs = (q_ref, k_ref, v_ref, lab_ref, gapb_ref, lvlb_ref, stb_ref)

    def finish(o, rows):
        g = g_ref[rows, :].astype(F32)
        for hd in range(2):
            sl = slice(hd * GLA_DV, (hd + 1) * GLA_DV)
            oh = _rms(o[:, sl], gh_ref[:, sl]) * _silu(g[:, sl])
            o_ref[rows, sl] = oh.astype(o_ref.dtype)

    def seq_body(sg, carry, fast):
        base = sg * (par * seq_len)
        if has_state:
            stf_ref[0] = s0f_ref[...]
            stb_ref[0] = s0b_ref[...]
        else:
            stf_ref[...] = jnp.zeros_like(stf_ref)
            stb_ref[...] = jnp.zeros_like(stb_ref)

        def step(it, second_visit):
            jobs = []
            for slot in range(par):
                for u in range(unroll):
                    cf = it * unroll + u
                    for forward, ci in ((True, cf), (False, n_chunks - 1 - cf)):
                        r0 = base + slot * seq_len + ci * CHUNK
                        jobs.append((pl.multiple_of(r0, CHUNK), forward, slot))
            if fast:
                outs = _gla_fast_step(jobs, fwd_refs, bwd_refs)
            else:
                outs = [_gla_chunk(r0, fw, *(fwd_refs if fw else bwd_refs)[:6],
                                   (stf_ref if fw else stb_ref).at[slot]) for r0, fw, slot in jobs]
            for (r0, _, _), o in zip(jobs, outs):
                rows = pl.ds(r0, CHUNK)
                if second_visit:
                    finish(o + acc_ref[rows, :], rows)
                else:
                    acc_ref[rows, :] = o

        def first(i, c):
            step(i, False)
            return c

        def second(i, c):
            step(half_iters + i, True)
            return c

        lax.fori_loop(0, half_iters, first, 0)
        lax.fori_loop(0, half_iters, second, 0)
        if emit_state:
            for s_out, st_ref in ((sf_ref, stf_ref), (sb_ref, stb_ref)):
                for slot in range(par):
                    for hd in range(2):
                        s_out[sg * par + slot, hd] = st_ref[slot, hd].T[hd * GLA_DK:(hd + 1) * GLA_DK, :]
        return carry

    weakest = jnp.minimum(jnp.min(laf_ref[...]), jnp.min(lab_ref[...]))
    is_fast = weakest >= -FAST_STEP_LIMIT

    @pl.when(is_fast)
    def _():
        lax.fori_loop(0, n_seq // par, functools.partial(seq_body, fast=True), 0)

    @pl.when(jnp.logical_not(is_fast))
    def _():
        lax.fori_loop(0, n_seq // par, functools.partial(seq_body, fast=False), 0)


def _gla(layer, q, k, v, la, g, g_head, consts, row0, seq_len, n_seq_total, n_seq, t_all,
         s0=None, o_prev=None, st_prev=None, emit_state=False):
    gapf, gapb, lvlf, lvlb = consts
    rows = seq_len * n_seq
    blk0 = row0 // rows
    n_steps = n_seq_total // n_seq
    rspec = lambda w, off=0: pl.BlockSpec((rows, w), lambda i, j: (blk0 + i, j + off))
    const = lambda a: pl.BlockSpec(a.shape, lambda i, j: (0,) * a.ndim)
    in_specs = [rspec(PAIR_K), rspec(PAIR_K), rspec(PAIR_V), rspec(PAIR_K), rspec(PAIR_K, HEAD_PAIRS),
                rspec(PAIR_V),
                pl.BlockSpec((None, 1, PAIR_V), lambda i, j: (layer, 0, j)),
                const(gapf), const(gapb), const(lvlf), const(lvlb)]
    args = [q, k, v, la, la, g, g_head, gapf, gapb, lvlf, lvlb]
    if s0 is not None:
        sspec = pl.BlockSpec((None, None, 2, GLA_DV, PAIR_K), lambda i, j: (i, layer, j, 0, 0))
        in_specs += [sspec, sspec]
        args += list(s0)
    aliases = {}
    prev = ([(o_prev, 0)] if o_prev is not None else []) + (
        [(a, 1 + n) for n, a in enumerate(st_prev)] if st_prev else [])
    for a, out_idx in prev:
        aliases[len(args)] = out_idx
        in_specs.append(pl.BlockSpec(memory_space=pl.ANY))
        args.append(a)
    out_specs = [rspec(PAIR_V)]
    out_shape = [jax.ShapeDtypeStruct((t_all, V_W), BF16)]
    if emit_state:
        st_spec = pl.BlockSpec((n_seq, None, 2, GLA_DK, GLA_DV), lambda i, j: (i, layer, j, 0, 0))
        st_shape = jax.ShapeDtypeStruct((n_seq_total, DEPTH, GLA_HEADS, GLA_DK, GLA_DV), F32)
        out_specs += [st_spec, st_spec]
        out_shape += [st_shape, st_shape]
    unroll = min(SCAN_UNROLL, seq_len // (2 * CHUNK))
    par = 1 if s0 is not None else min(n_seq, SCAN_UNROLL // unroll)
    kern = functools.partial(_gla_kernel, seq_len=seq_len, n_seq=n_seq, has_state=s0 is not None,
                             emit_state=emit_state, n_alias=len(prev),
                             unroll=unroll, par=par)
    return pl.pallas_call(
        kern,
        grid=(n_steps, HEAD_PAIRS),
        in_specs=in_specs,
        out_specs=out_specs,
        out_shape=out_shape,
        input_output_aliases=aliases,
        scratch_shapes=[pltpu.VMEM((rows, PAIR_V), F32), pltpu.VMEM((par, 2, GLA_DV, PAIR_K), F32),
                        pltpu.VMEM((par, 2, GLA_DV, PAIR_K), F32)],
        compiler_params=_params(2),
        name="gla_state" if emit_state else "gla",
    )(*args)


def _fnet_kernel(c_ref, ns_ref, uc_ref, us_ref, o_ref, *, seq_len, n_seq):
    for sq in range(n_seq):
        rows = slice(sq * seq_len, (sq + 1) * seq_len)
        y = _dot(c_ref[...], uc_ref[rows, :]) + _dot(ns_ref[...], us_ref[rows, :])
        o_ref[rows, :] = y.astype(o_ref.dtype)


def _fnet_ctx(uc, us, cos_n, nsin_n, seq_len, n_seq_total, n_seq, t_all):
    rows = seq_len * n_seq
    rspec = pl.BlockSpec((rows, FNO_W), lambda i: (i, 0))
    cspec = pl.BlockSpec((seq_len, seq_len), lambda i: (0, 0))
    return pl.pallas_call(
        functools.partial(_fnet_kernel, seq_len=seq_len, n_seq=n_seq),
        grid=(n_seq_total // n_seq,),
        in_specs=[cspec, cspec, rspec, rspec],
        out_specs=rspec,
        out_shape=jax.ShapeDtypeStruct((t_all, FNO_W), BF16),
        compiler_params=_params(1),
        name="fnet_ctx",
    )(cos_n, nsin_n, uc, us)


def _fnet_lat_kernel(c_ref, ns_ref, uc_ref, us_ref, prev_ref, o_ref):
    del prev_ref
    y = _dot(c_ref[...], uc_ref[...]) + _dot(ns_ref[...], us_ref[...])
    o_ref[...] = y.astype(o_ref.dtype)


def _fnet_lat(uc, us, cos_n, nsin_n, yf_prev, row0, seq_len, n_seq_total):
    tr = 256
    blk0 = row0 // seq_len
    mspec = pl.BlockSpec((tr, seq_len), lambda b, i: (i, 0))
    uspec = pl.BlockSpec((seq_len, FNO_W), lambda b, i: (blk0 + b, 0))
    n_tiles = seq_len // tr
    return pl.pallas_call(
        _fnet_lat_kernel,
        grid=(n_seq_total, n_tiles),
        in_specs=[mspec, mspec, uspec, uspec, pl.BlockSpec(memory_space=pl.ANY)],
        out_specs=pl.BlockSpec((tr, FNO_W), lambda b, i: (row0 // tr + b * n_tiles + i, 0)),
        out_shape=jax.ShapeDtypeStruct(yf_prev.shape, BF16),
        input_output_aliases={4: 0},
        compiler_params=_params(2),
        name="fnet_lat",
    )(cos_n, nsin_n, uc, us, yf_prev)


def _post_kernel(x_ref, o_ref, yf_ref, mod_ref, gpm_ref, gpre_ref, gpost_ref,
                 wout_ref, w1_ref, w2_ref, out_ref):
    gt_m = mod_ref[:, 2 * D_MODEL:3 * D_MODEL]
    sh_f = mod_ref[:, 3 * D_MODEL:4 * D_MODEL]
    sc_f = mod_ref[:, 4 * D_MODEL:5 * D_MODEL]
    gt_f = mod_ref[:, 5 * D_MODEL:6 * D_MODEL]
    y = _dot(o_ref[...], wout_ref[0:V_W, :]) + _dot(yf_ref[...], wout_ref[V_W:, :])
    x = x_ref[...] + gt_m * _rms(y, gpm_ref[...])
    h = (_rms(x, gpre_ref[...]) * (1.0 + sc_f) + sh_f).astype(BF16)
    f = jnp.zeros((ROW_TILE, D_MODEL), F32)
    n_split = 4
    wf = D_FF // n_split
    for j in range(n_split):
        a = jnp.maximum(_dot(h, w1_ref[:, j * wf:(j + 1) * wf]), 0.0)
        f = f + _dot((a * a).astype(BF16), w2_ref[j * wf:(j + 1) * wf, :])
    out_ref[...] = x + gt_f * _rms(f, gpost_ref[...])


def _post(layer, x, o, yf, mod, mod_sel, g_post_mix, g_pre_mlp, g_post_mlp, w_out, w_fc1, w_fc2):
    t = x.shape[0]
    row = lambda w: pl.BlockSpec((ROW_TILE, w), lambda i: (i, 0))
    lay = lambda *shape: pl.BlockSpec((None,) + shape, lambda i: (layer,) + (0,) * len(shape),
                                      pipeline_mode=pl.Buffered(1))
    return pl.pallas_call(
        _post_kernel,
        grid=(t // ROW_TILE,),
        in_specs=[row(D_MODEL), row(V_W), row(FNO_W),
                  pl.BlockSpec((None, None, 1, 6 * D_MODEL), lambda i: (layer, mod_sel(i), 0, 0)),
                  lay(1, D_MODEL), lay(1, D_MODEL), lay(1, D_MODEL),
                  lay(V_W + FNO_W, D_MODEL), lay(D_MODEL, D_FF), lay(D_FF, D_MODEL)],
        out_specs=row(D_MODEL),
        out_shape=jax.ShapeDtypeStruct((t, D_MODEL), F32),
        compiler_params=_params(1),
        name="post",
    )(x, o, yf, mod, g_post_mix, g_pre_mlp, g_post_mlp, w_out, w_fc1, w_fc2)


def _state_to_kernel(s):
    st = jnp.swapaxes(s, -1, -2)
    z = jnp.zeros_like(st)
    even = jnp.concatenate([st, z], axis=-1)
    odd = jnp.concatenate([z, st], axis=-1)
    is_even = (jnp.arange(GLA_HEADS) % 2 == 0)[None, None, :, None, None]
    return jnp.where(is_even, even, odd)


def kernel(x_prompt, x_sample, state_gla_fwd, state_gla_bwd, c, c_ctx, w_in, w_out, w_a2, b_a2,
           g_head, w_ada, b_ada, g_pre_mix, g_post_mix, g_pre_mlp, g_post_mlp, w_fc1, w_fc2):
    batch, seq, _ = x_prompt.shape
    dec_batch, dec_seq, _ = x_sample.shape
    t_ctx = batch * seq
    t_lat = dec_batch * dec_seq
    t_all = t_ctx + t_lat
    ctx_tiles = t_ctx // ROW_TILE
    lat_tiles = dec_seq // ROW_TILE

    def mod_sel(i):
        return jnp.where(i < ctx_tiles, 0, 1 + (i - ctx_tiles) // lat_tiles)

    w_in_p = jnp.concatenate([w_in[:, :, :U_OFF], w_in[:, :, U_OFF + 2 * LR_RANK:],
                              w_in[:, :, U_OFF:U_OFF + 2 * LR_RANK]], axis=-1).astype(BF16)
    za = jnp.zeros((DEPTH, LR_RANK, QK_W), F32)
    wa2 = jnp.concatenate([jnp.concatenate([w_a2[:, 0], za], axis=-1),
                           jnp.concatenate([za, w_a2[:, 1]], axis=-1)], axis=1).astype(BF16)
    ba2 = b_a2.reshape(DEPTH, 1, 2 * QK_W)
    w_out_b = w_out.astype(BF16)
    w_fc1_b = w_fc1.astype(BF16)
    w_fc2_b = w_fc2.astype(BF16)
    vec = lambda a: a.reshape(DEPTH, 1, -1)
    g_head_r = g_head.reshape(DEPTH, 1, V_W)

    gapf, gapb, lvlf, lvlb = _gap_matrices()
    as_bf16 = lambda a: jnp.asarray(a, F32).astype(BF16)
    consts = (as_bf16(gapf), as_bf16(gapb), jnp.asarray(lvlf), jnp.asarray(lvlb))
    cw, sw = _dft_tables(FNO_GROUP_W)
    cs_w = as_bf16(np.concatenate([cw, sw], axis=1))
    cn, sn = _dft_tables(seq)
    cos_ctx, nsin_ctx = as_bf16(cn), as_bf16(-sn)
    cos_lat, nsin_lat = _dft_big(dec_seq, GRID_W)

    cvec = jnp.zeros((MOD_ROWS, D_MODEL), F32).at[0].set(c_ctx).at[1:1 + dec_batch].set(c)
    mod = _modulation(cvec, w_ada, b_ada)[:, :1 + dec_batch, None, :]

    xs = _embed(x_sample, _pos_emb_2d(dec_seq))
    x = jnp.concatenate([x_prompt.reshape(t_ctx, D_MODEL), xs.reshape(t_lat, D_MODEL)], axis=0)
    s0 = (_state_to_kernel(state_gla_fwd), _state_to_kernel(state_gla_bwd))

    states = None
    for l in range(DEPTH):
        q, k, v, g, la, uc, us = _premix(l, x, mod, mod_sel, vec(g_pre_mix), w_in_p, wa2, ba2, cs_w)
        o, *states = _gla(l, q, k, v, la, g, g_head_r, consts, 0, seq, batch, 4, t_all,
                          st_prev=states, emit_state=True)
        (o,) = _gla(l, q, k, v, la, g, g_head_r, consts, t_ctx, dec_seq, dec_batch, 1, t_all,
                    s0=s0, o_prev=o)
        yf = _fnet_ctx(uc, us, cos_ctx, nsin_ctx, seq, batch, 4, t_all)
        yf = _fnet_lat(uc, us, cos_lat, nsin_lat, yf, t_ctx, dec_seq, dec_batch)
        x = _post(l, x, o, yf, mod, mod_sel, vec(g_post_mix), vec(g_pre_mlp), vec(g_post_mlp),
                  w_out_b, w_fc1_b, w_fc2_b)

    y_prompt = x[:t_ctx].reshape(batch, seq, D_MODEL)
    y_sample = x[t_ctx:].reshape(dec_batch, dec_seq, D_MODEL)
    return (y_prompt, y_sample, states[0], states[1])
```

```python
import functools

import numpy as np
import jax
import jax.numpy as jnp
from jax import lax
from jax.experimental import pallas as pl
from jax.experimental.pallas import tpu as pltpu

D_MODEL = 1024
DEPTH = 4
GRID_W = 64
GLA_HEADS = 4
GLA_DK = 64
GLA_DV = 128
CHUNK = 64
GATE_TAU = 16.0
LR_RANK = 16
FNO_GROUPS = 4
FNO_GROUP_W = 128
D_FF = 4 * D_MODEL
EPS = 1e-6
QK_W = GLA_HEADS * GLA_DK
V_W = GLA_HEADS * GLA_DV
FNO_W = FNO_GROUPS * FNO_GROUP_W
D_IN = 2 * QK_W + 2 * V_W + 2 * LR_RANK + FNO_W
U_OFF = 2 * QK_W + 2 * V_W
LR_OFF = U_OFF + FNO_W
HEAD_PAIRS = GLA_HEADS // 2
PAIR_K = 2 * GLA_DK
PAIR_V = 2 * GLA_DV
N_LEVELS = 6
N_GAPS = N_LEVELS + 2
SCAN_UNROLL = 8
FAST_STEP_LIMIT = 1.0

VMEM_LIMIT = 48 * 1024 * 1024
ROW_TILE = 512
MOD_ROWS = 8

BF16 = jnp.bfloat16
F32 = jnp.float32


def _gap_matrices():
    c = CHUNK
    m_all = np.zeros((N_GAPS, c, c), np.float32)
    for p in range(c):
        m_all[0, p, :p + 1] = 1.0
        m_all[1, p, p + 1:] = 1.0
    for lv in range(N_LEVELS):
        m = c >> (lv + 1)
        for p in range(c):
            base = (p // (2 * m)) * 2 * m
            ref = base + m - 1
            if p > ref:
                m_all[2 + lv, p, ref + 1:p + 1] = 1.0
            else:
                m_all[2 + lv, p, p + 1:ref + 1] = 1.0
    fwd = m_all.reshape(N_GAPS * c, c)
    bwd = m_all[:, ::-1, ::-1].reshape(N_GAPS * c, c)
    lvl = np.full((c, c), N_LEVELS + 1, np.int32)
    for t in range(c):
        for s in range(t + 1):
            if s == t:
                lvl[t, s] = 0
            else:
                top = (t ^ s).bit_length() - 1
                lvl[t, s] = N_LEVELS - top
    lvl_f = np.concatenate([lvl, lvl], axis=1)
    lvl_b = np.concatenate([lvl[::-1, ::-1], lvl[::-1, ::-1]], axis=1)
    return fwd, bwd, lvl_f, lvl_b


def _dft_tables(n):
    idx = np.arange(n)
    ang = 2.0 * np.pi * ((idx[:, None] * idx[None, :]) % n) / n
    s = 1.0 / np.sqrt(n)
    return (np.cos(ang) * s).astype(np.float32), (np.sin(ang) * s).astype(np.float32)


def _dft_big(n, n1):
    n2 = n // n1
    k = np.arange(n)
    a = 2.0 * np.pi * ((np.arange(n1)[:, None] * n2 * k[None, :]) % n) / n
    b = 2.0 * np.pi * ((np.arange(n2)[:, None] * k[None, :]) % n) / n
    s = 1.0 / np.sqrt(n)
    ca, sa = jnp.asarray(np.cos(a) * s, F32), jnp.asarray(np.sin(a) * s, F32)
    cb, sb = jnp.asarray(np.cos(b), F32), jnp.asarray(np.sin(b), F32)
    cos = ca[:, None, :] * cb[None] - sa[:, None, :] * sb[None]
    nsin = -(sa[:, None, :] * cb[None] + ca[:, None, :] * sb[None])
    return cos.reshape(n, n).astype(BF16), nsin.reshape(n, n).astype(BF16)


def _pos_emb_2d(n_tokens):
    t = jnp.arange(n_tokens)
    r = (t // GRID_W).astype(F32)[:, None]
    col = (t % GRID_W).astype(F32)[:, None]
    nf = D_MODEL // 4
    omega = 1.0 / (10000.0 ** (jnp.arange(nf, dtype=F32) / nf))
    return jnp.concatenate([jnp.sin(r * omega), jnp.cos(r * omega),
                            jnp.sin(col * omega), jnp.cos(col * omega)], axis=-1)


def _silu(x):
    return x * (1.0 / (1.0 + jnp.exp(-x)))


def _rms(x, g):
    return x * lax.rsqrt(jnp.mean(x * x, axis=-1, keepdims=True) + EPS) * g


def _dot(a, b):
    return jnp.dot(a, b, preferred_element_type=F32)


def _dot_nt(a, b):
    return lax.dot_general(a, b, (((1,), (1,)), ((), ())), preferred_element_type=F32)


def _dot_tn(a, b):
    return lax.dot_general(a, b, (((0,), (0,)), ((), ())), preferred_element_type=F32)


def _params(n_grid):
    return pltpu.CompilerParams(dimension_semantics=("arbitrary",) * n_grid,
                                vmem_limit_bytes=VMEM_LIMIT)


def _mod_kernel(c_ref, w_ref, b_ref, o_ref):
    s = _silu(c_ref[...])
    o_ref[...] = jnp.dot(s, w_ref[...], preferred_element_type=F32,
                         precision=lax.Precision.HIGHEST) + b_ref[...]


def _modulation(cvec, w_ada, b_ada):
    tn = 1536
    return pl.pallas_call(
        _mod_kernel,
        grid=(DEPTH, 6 * D_MODEL // tn),
        in_specs=[pl.BlockSpec((MOD_ROWS, D_MODEL), lambda l, j: (0, 0)),
                  pl.BlockSpec((None, D_MODEL, tn), lambda l, j: (l, 0, j)),
                  pl.BlockSpec((None, 1, tn), lambda l, j: (l, 0, j))],
        out_specs=pl.BlockSpec((None, MOD_ROWS, tn), lambda l, j: (l, 0, j)),
        out_shape=jax.ShapeDtypeStruct((DEPTH, MOD_ROWS, 6 * D_MODEL), F32),
        compiler_params=_params(2),
        name="modulation",
    )(cvec, w_ada, b_ada.reshape(DEPTH, 1, 6 * D_MODEL))


def _embed_kernel(x_ref, p_ref, o_ref):
    o_ref[...] = x_ref[...] + p_ref[...]


def _embed(x_sample, pos):
    b, n, d = x_sample.shape
    return pl.pallas_call(
        _embed_kernel,
        grid=(b, n // ROW_TILE),
        in_specs=[pl.BlockSpec((None, ROW_TILE, d), lambda i, j: (i, j, 0)),
                  pl.BlockSpec((ROW_TILE, d), lambda i, j: (j, 0))],
        out_specs=pl.BlockSpec((None, ROW_TILE, d), lambda i, j: (i, j, 0)),
        out_shape=jax.ShapeDtypeStruct(x_sample.shape, x_sample.dtype),
        compiler_params=_params(2),
        name="embed",
    )(x_sample, pos)


def _premix_kernel(x_ref, mod_ref, gpre_ref, win_ref, wa2_ref, ba2_ref, cs_ref,
                   q_ref, k_ref, v_ref, g_ref, la_ref, uc_ref, us_ref):
    x = x_ref[...]
    sh = mod_ref[:, 0:D_MODEL]
    sc = mod_ref[:, D_MODEL:2 * D_MODEL]
    h = (_rms(x, gpre_ref[...]) * (1.0 + sc) + sh).astype(BF16)
    u = _dot(h, win_ref[:, U_OFF:U_OFF + FNO_W]).astype(BF16)
    lr = _dot(h, win_ref[:, LR_OFF:LR_OFF + 2 * LR_RANK]).astype(BF16)
    q_ref[...] = _dot(h, win_ref[:, 0:QK_W]) * (GLA_DK ** -0.5)
    k_ref[...] = _dot(h, win_ref[:, QK_W:2 * QK_W])
    v_ref[...] = _dot(h, win_ref[:, 2 * QK_W:2 * QK_W + V_W]).astype(BF16)
    g_ref[...] = _dot(h, win_ref[:, 2 * QK_W + V_W:U_OFF]).astype(BF16)
    logit = _dot(lr, wa2_ref[...]) + ba2_ref[...]
    softplus_neg = jnp.maximum(-logit, 0.0) + jnp.log(1.0 + jnp.exp(-jnp.abs(logit)))
    la_ref[...] = softplus_neg * (-1.0 / GATE_TAU)
    for grp in range(FNO_GROUPS):
        sl = slice(grp * FNO_GROUP_W, (grp + 1) * FNO_GROUP_W)
        ucs = _dot(u[:, sl], cs_ref[...])
        uc_ref[:, sl] = ucs[:, :FNO_GROUP_W].astype(BF16)
        us_ref[:, sl] = ucs[:, FNO_GROUP_W:].astype(BF16)


def _premix(layer, x, mod, mod_sel, g_pre, w_in, wa2, ba2, cs_w):
    t = x.shape[0]
    row = lambda w: pl.BlockSpec((ROW_TILE, w), lambda i: (i, 0))
    lay = lambda *shape: pl.BlockSpec((None,) + shape, lambda i: (layer,) + (0,) * len(shape))
    out = lambda w, dt: jax.ShapeDtypeStruct((t, w), dt)
    return pl.pallas_call(
        _premix_kernel,
        grid=(t // ROW_TILE,),
        in_specs=[row(D_MODEL),
                  pl.BlockSpec((None, None, 1, 6 * D_MODEL), lambda i: (layer, mod_sel(i), 0, 0)),
                  lay(1, D_MODEL), lay(D_MODEL, D_IN), lay(2 * LR_RANK, 2 * QK_W), lay(1, 2 * QK_W),
                  pl.BlockSpec((FNO_GROUP_W, 2 * FNO_GROUP_W), lambda i: (0, 0))],
        out_specs=[row(QK_W), row(QK_W), row(V_W), row(V_W), row(2 * QK_W), row(FNO_W), row(FNO_W)],
        out_shape=[out(QK_W, F32), out(QK_W, F32), out(V_W, BF16), out(V_W, BF16),
                   out(2 * QK_W, F32), out(FNO_W, BF16), out(FNO_W, BF16)],
        compiler_params=_params(1),
        name="premix",
    )(x, mod, g_pre, w_in, wa2, ba2, cs_w)


def _gla_chunk(r0, forward, q_ref, k_ref, v_ref, la_ref, gap_ref, lvl_ref, st_ref):
    rows = pl.ds(r0, CHUNK)
    lane = lax.broadcasted_iota(jnp.int32, (CHUNK, PAIR_K), 1)
    head0 = lane < GLA_DK
    la = la_ref[rows, :]
    la_hi = la.astype(BF16)
    la_lo = (la - la_hi.astype(F32)).astype(BF16)
    gaps = _dot(gap_ref[...], jnp.concatenate([la_hi, la_lo], axis=1))
    decay = jnp.exp(gaps[:, :PAIR_K] + gaps[:, PAIR_K:])
    blk = lambda j: decay[j * CHUNK:(j + 1) * CHUNK]
    q = q_ref[rows, :]
    k = k_ref[rows, :]
    k_h = (jnp.where(head0, k, 0.0), jnp.where(head0, 0.0, k))
    v = v_ref[rows, :]
    lvl = lvl_ref[...]

    a = jnp.zeros((CHUNK, 2 * CHUNK), F32)
    for lv in range(N_LEVELS + 1):
        if lv == 0:
            ql = q.astype(BF16)
            kl = jnp.concatenate([k_h[0].astype(BF16), k_h[1].astype(BF16)], axis=0)
        else:
            d = blk(1 + lv)
            ql = (q * d).astype(BF16)
            kl = jnp.concatenate([(k_h[0] * d).astype(BF16), (k_h[1] * d).astype(BF16)], axis=0)
        a = jnp.where(lvl == lv, _dot_nt(ql, kl), a)

    q_in = (q * blk(0)).astype(BF16)
    st = jnp.concatenate([st_ref[0].astype(BF16), st_ref[1].astype(BF16)], axis=0)
    inter = _dot_nt(q_in, st)
    vlane = lax.broadcasted_iota(jnp.int32, (CHUNK, PAIR_V), 1)
    zero = jnp.zeros_like(v)
    v_blk = jnp.concatenate([jnp.where(vlane < GLA_DV, v, zero),
                             jnp.where(vlane < GLA_DV, zero, v)], axis=0)
    o = inter + _dot(a.astype(BF16), v_blk)

    d_out = blk(1)
    total = blk(0)[CHUNK - 1:CHUNK] if forward else blk(0)[0:1]
    for hd in range(2):
        k_out = (k_h[hd] * d_out).astype(BF16)
        upd = _dot_tn(v[:, hd * GLA_DV:(hd + 1) * GLA_DV], k_out)
        st_ref[hd] = st_ref[hd] * total + upd
    return o


def _gla_fast_step(jobs, fwd_refs, bwd_refs):
    refs = [fwd_refs if fw else bwd_refs for _, fw, _ in jobs]
    rows = [pl.ds(r0, CHUNK) for r0, _, _ in jobs]
    lane = lax.broadcasted_iota(jnp.int32, (CHUNK, PAIR_K), 1)
    head0 = lane < GLA_DK
    vlane = lax.broadcasted_iota(jnp.int32, (CHUNK, PAIR_V), 1) < GLA_DV

    la = [r[3][rw, :] for r, rw in zip(refs, rows)]
    la_hi = [x.astype(BF16) for x in la]
    la_lo = [(x - h.astype(F32)).astype(BF16) for x, h in zip(la, la_hi)]
    cum = [_dot(r[4][0:CHUNK, :], jnp.concatenate([h, l], axis=1))
           for r, h, l in zip(refs, la_hi, la_lo)]
    b = [x[:, :PAIR_K] + x[:, PAIR_K:] for x in cum]
    b_end = [x[CHUNK - 1:CHUNK] if fw else x[0:1] for x, (_, fw, _) in zip(b, jobs)]
    q = [r[0][rw, :] for r, rw in zip(refs, rows)]
    k = [r[1][rw, :] for r, rw in zip(refs, rows)]
    v = [r[2][rw, :] for r, rw in zip(refs, rows)]
    qd = [(x * jnp.exp(y)).astype(BF16) for x, y in zip(q, b)]
    kd = [x * jnp.exp(-y) for x, y in zip(k, b)]
    ko = [x * jnp.exp(e - y) for x, y, e in zip(k, b, b_end)]
    total = [jnp.exp(e) for e in b_end]
    kd_cat = [jnp.concatenate([jnp.where(head0, x, 0.0).astype(BF16),
                               jnp.where(head0, 0.0, x).astype(BF16)], axis=0) for x in kd]
    score = [_dot_nt(x, y) for x, y in zip(qd, kd_cat)]
    a = [jnp.where(r[5][...] <= N_LEVELS, s, 0.0).astype(BF16) for r, s in zip(refs, score)]
    v_blk = [jnp.concatenate([jnp.where(vlane, x, jnp.zeros_like(x)),
                              jnp.where(vlane, jnp.zeros_like(x), x)], axis=0) for x in v]
    intra = [_dot(x, y) for x, y in zip(a, v_blk)]
    ko_h = [(jnp.where(head0, x, 0.0).astype(BF16), jnp.where(head0, 0.0, x).astype(BF16)) for x in ko]
    upd = [[_dot_tn(x[:, hd * GLA_DV:(hd + 1) * GLA_DV], y[hd]) for hd in range(2)]
           for x, y in zip(v, ko_h)]

    outs = [None] * len(jobs)
    for scan in sorted({(fw, slot) for _, fw, slot in jobs}):
        st_ref = (fwd_refs if scan[0] else bwd_refs)[6]
        st = [st_ref[scan[1], 0], st_ref[scan[1], 1]]
        for j, (_, fw, slot) in enumerate(jobs):
            if (fw, slot) != scan:
                continue
            st_cat = jnp.concatenate([st[0].astype(BF16), st[1].astype(BF16)], axis=0)
            outs[j] = _dot_nt(qd[j], st_cat) + intra[j]
            st = [st[hd] * total[j] + upd[j][hd] for hd in range(2)]
        st_ref[scan[1], 0] = st[0]
        st_ref[scan[1], 1] = st[1]
    return outs


def _gla_kernel(*refs, seq_len, n_seq, has_state, emit_state, unroll, par):
    (q_ref, k_ref, v_ref, laf_ref, lab_ref, g_ref, gh_ref,
     gapf_ref, gapb_ref, lvlf_ref, lvlb_ref) = refs[:11]
    pos = 11
    if has_state:
        s0f_ref, s0b_ref = refs[pos:pos + 2]
        pos += 2
    o_ref = refs[pos]
    pos += 1
    if emit_state:
        sf_ref, sb_ref = refs[pos:pos + 2]
        pos += 2
    acc_ref, stf_ref, stb_ref = refs[pos:pos + 3]
    n_chunks = seq_len // CHUNK
    half_iters = n_chunks // (2 * unroll)
    fwd_refs = (q_ref, k_ref, v_ref, laf_ref, gapf_ref, lvlf_ref, stf_ref)
    bwd_refs = (q_ref, k_ref, v_ref, lab_ref, gapb_ref, lvlb_ref, stb_ref)

    def finish(o, rows):
        g = g_ref[rows, :].astype(F32)
        for hd in range(2):
            sl = slice(hd * GLA_DV, (hd + 1) * GLA_DV)
            oh = _rms(o[:, sl], gh_ref[:, sl]) * _silu(g[:, sl])
            o_ref[rows, sl] = oh.astype(o_ref.dtype)

    def seq_body(sg, carry, fast):
        base = sg * (par * seq_len)
        if has_state:
            stf_ref[0] = s0f_ref[...]
            stb_ref[0] = s0b_ref[...]
        else:
            stf_ref[...] = jnp.zeros_like(stf_ref)
            stb_ref[...] = jnp.zeros_like(stb_ref)

        def step(it, second_visit):
            jobs = []
            for slot in range(par):
                for u in range(unroll):
                    cf = it * unroll + u
                    for forward, ci in ((True, cf), (False, n_chunks - 1 - cf)):
                        r0 = base + slot * seq_len + ci * CHUNK
                        jobs.append((pl.multiple_of(r0, CHUNK), forward, slot))
            if fast:
                outs = _gla_fast_step(jobs, fwd_refs, bwd_refs)
            else:
                outs = [_gla_chunk(r0, fw, *(fwd_refs if fw else bwd_refs)[:6],
                                   (stf_ref if fw else stb_ref).at[slot]) for r0, fw, slot in jobs]
            for (r0, _, _), o in zip(jobs, outs):
                rows = pl.ds(r0, CHUNK)
                if second_visit:
                    finish(o + acc_ref[rows, :], rows)
                else:
                    acc_ref[rows, :] = o

        def first(i, c):
            step(i, False)
            return c

        def second(i, c):
            step(half_iters + i, True)
            return c

        lax.fori_loop(0, half_iters, first, 0)
        lax.fori_loop(0, half_iters, second, 0)
        if emit_state:
            for s_out, st_ref in ((sf_ref, stf_ref), (sb_ref, stb_ref)):
                for slot in range(par):
                    for hd in range(2):
                        s_out[sg * par + slot, hd] = st_ref[slot, hd].T[hd * GLA_DK:(hd + 1) * GLA_DK, :]
        return carry

    weakest = jnp.minimum(jnp.min(laf_ref[...]), jnp.min(lab_ref[...]))
    is_fast = weakest >= -FAST_STEP_LIMIT

    @pl.when(is_fast)
    def _():
        lax.fori_loop(0, n_seq // par, functools.partial(seq_body, fast=True), 0)

    @pl.when(jnp.logical_not(is_fast))
    def _():
        lax.fori_loop(0, n_seq // par, functools.partial(seq_body, fast=False), 0)


def _gla(layer, q, k, v, la, g, g_head, consts, seq_len, n_seq_total, n_seq, s0=None,
         emit_state=False):
    gapf, gapb, lvlf, lvlb = consts
    rows = seq_len * n_seq
    n_steps = n_seq_total // n_seq
    rspec = lambda w, off=0: pl.BlockSpec((rows, w), lambda i, j: (i, j + off))
    const = lambda a: pl.BlockSpec(a.shape, lambda i, j: (0,) * a.ndim)
    in_specs = [rspec(PAIR_K), rspec(PAIR_K), rspec(PAIR_V), rspec(PAIR_K), rspec(PAIR_K, HEAD_PAIRS),
                rspec(PAIR_V),
                pl.BlockSpec((None, 1, PAIR_V), lambda i, j: (layer, 0, j)),
                const(gapf), const(gapb), const(lvlf), const(lvlb)]
    args = [q, k, v, la, la, g, g_head, gapf, gapb, lvlf, lvlb]
    if s0 is not None:
        sspec = pl.BlockSpec((None, None, 2, GLA_DV, PAIR_K), lambda i, j: (i, layer, j, 0, 0))
        in_specs += [sspec, sspec]
        args += list(s0)
    out_specs = [rspec(PAIR_V)]
    out_shape = [jax.ShapeDtypeStruct((n_seq_total * seq_len, V_W), BF16)]
    if emit_state:
        st_spec = pl.BlockSpec((n_seq, 2, GLA_DK, GLA_DV), lambda i, j: (i, j, 0, 0))
        st_shape = jax.ShapeDtypeStruct((n_seq_total, GLA_HEADS, GLA_DK, GLA_DV), F32)
        out_specs += [st_spec, st_spec]
        out_shape += [st_shape, st_shape]
    unroll = min(SCAN_UNROLL, seq_len // (2 * CHUNK))
    par = 1 if s0 is not None else min(n_seq, SCAN_UNROLL // unroll)
    kern = functools.partial(_gla_kernel, seq_len=seq_len, n_seq=n_seq, has_state=s0 is not None,
                             emit_state=emit_state, unroll=unroll, par=par)
    return pl.pallas_call(
        kern,
        grid=(n_steps, HEAD_PAIRS),
        in_specs=in_specs,
        out_specs=out_specs,
        out_shape=out_shape,
        scratch_shapes=[pltpu.VMEM((rows, PAIR_V), F32), pltpu.VMEM((par, 2, GLA_DV, PAIR_K), F32),
                        pltpu.VMEM((par, 2, GLA_DV, PAIR_K), F32)],
        compiler_params=_params(2),
        name="gla_state" if emit_state else "gla",
    )(*args)


def _fnet_kernel(c_ref, ns_ref, uc_ref, us_ref, o_ref, *, seq_len, n_seq):
    for sq in range(n_seq):
        rows = slice(sq * seq_len, (sq + 1) * seq_len)
        y = _dot(c_ref[...], uc_ref[rows, :]) + _dot(ns_ref[...], us_ref[rows, :])
        o_ref[rows, :] = y.astype(o_ref.dtype)


def _fnet_ctx(uc, us, cos_n, nsin_n, seq_len, n_seq_total, n_seq):
    rows = seq_len * n_seq
    rspec = pl.BlockSpec((rows, FNO_W), lambda i: (i, 0))
    cspec = pl.BlockSpec((seq_len, seq_len), lambda i: (0, 0))
    return pl.pallas_call(
        functools.partial(_fnet_kernel, seq_len=seq_len, n_seq=n_seq),
        grid=(n_seq_total // n_seq,),
        in_specs=[cspec, cspec, rspec, rspec],
        out_specs=rspec,
        out_shape=jax.ShapeDtypeStruct((n_seq_total * seq_len, FNO_W), BF16),
        compiler_params=_params(1),
        name="fnet_ctx",
    )(cos_n, nsin_n, uc, us)


def _fnet_lat_kernel(c_ref, ns_ref, uc_ref, us_ref, o_ref):
    y = _dot(c_ref[...], uc_ref[...]) + _dot(ns_ref[...], us_ref[...])
    o_ref[...] = y.astype(o_ref.dtype)


def _fnet_lat(uc, us, cos_n, nsin_n, seq_len, n_seq_total):
    tr = 256
    mspec = pl.BlockSpec((tr, seq_len), lambda b, i: (i, 0))
    uspec = pl.BlockSpec((seq_len, FNO_W), lambda b, i: (b, 0))
    n_tiles = seq_len // tr
    return pl.pallas_call(
        _fnet_lat_kernel,
        grid=(n_seq_total, n_tiles),
        in_specs=[mspec, mspec, uspec, uspec],
        out_specs=pl.BlockSpec((tr, FNO_W), lambda b, i: (b * n_tiles + i, 0)),
        out_shape=jax.ShapeDtypeStruct((n_seq_total * seq_len, FNO_W), BF16),
        compiler_params=_params(2),
        name="fnet_lat",
    )(cos_n, nsin_n, uc, us)


def _post_kernel(x_ref, o_ref, yf_ref, mod_ref, gpm_ref, gpre_ref, gpost_ref,
                 wout_ref, w1_ref, w2_ref, out_ref):
    gt_m = mod_ref[:, 2 * D_MODEL:3 * D_MODEL]
    sh_f = mod_ref[:, 3 * D_MODEL:4 * D_MODEL]
    sc_f = mod_ref[:, 4 * D_MODEL:5 * D_MODEL]
    gt_f = mod_ref[:, 5 * D_MODEL:6 * D_MODEL]
    y = _dot(o_ref[...], wout_ref[0:V_W, :]) + _dot(yf_ref[...], wout_ref[V_W:, :])
    x = x_ref[...] + gt_m * _rms(y, gpm_ref[...])
    h = (_rms(x, gpre_ref[...]) * (1.0 + sc_f) + sh_f).astype(BF16)
    f = jnp.zeros((ROW_TILE, D_MODEL), F32)
    n_split = 4
    wf = D_FF // n_split
    for j in range(n_split):
        a = jnp.maximum(_dot(h, w1_ref[:, j * wf:(j + 1) * wf]), 0.0)
        f = f + _dot((a * a).astype(BF16), w2_ref[j * wf:(j + 1) * wf, :])
    out_ref[...] = x + gt_f * _rms(f, gpost_ref[...])


def _post(layer, x, o, yf, mod, mod_sel, g_post_mix, g_pre_mlp, g_post_mlp, w_out, w_fc1, w_fc2):
    t = x.shape[0]
    row = lambda w: pl.BlockSpec((ROW_TILE, w), lambda i: (i, 0))
    lay = lambda *shape: pl.BlockSpec((None,) + shape, lambda i: (layer,) + (0,) * len(shape),
                                      pipeline_mode=pl.Buffered(1))
    return pl.pallas_call(
        _post_kernel,
        grid=(t // ROW_TILE,),
        in_specs=[row(D_MODEL), row(V_W), row(FNO_W),
                  pl.BlockSpec((None, None, 1, 6 * D_MODEL), lambda i: (layer, mod_sel(i), 0, 0)),
                  lay(1, D_MODEL), lay(1, D_MODEL), lay(1, D_MODEL),
                  lay(V_W + FNO_W, D_MODEL), lay(D_MODEL, D_FF), lay(D_FF, D_MODEL)],
        out_specs=row(D_MODEL),
        out_shape=jax.ShapeDtypeStruct((t, D_MODEL), F32),
        compiler_params=_params(1),
        name="post",
    )(x, o, yf, mod, g_post_mix, g_pre_mlp, g_post_mlp, w_out, w_fc1, w_fc2)


def _state_to_kernel(s):
    st = jnp.swapaxes(s, -1, -2)
    z = jnp.zeros_like(st)
    even = jnp.concatenate([st, z], axis=-1)
    odd = jnp.concatenate([z, st], axis=-1)
    is_even = (jnp.arange(GLA_HEADS) % 2 == 0)[None, None, :, None, None]
    return jnp.where(is_even, even, odd)


def kernel(x_prompt, x_sample, state_gla_fwd, state_gla_bwd, c, c_ctx, w_in, w_out, w_a2, b_a2,
           g_head, w_ada, b_ada, g_pre_mix, g_post_mix, g_pre_mlp, g_post_mlp, w_fc1, w_fc2):
    batch, seq, _ = x_prompt.shape
    dec_batch, dec_seq, _ = x_sample.shape
    t_ctx = batch * seq
    t_lat = dec_batch * dec_seq
    lat_tiles = dec_seq // ROW_TILE
    mod_sel = (lambda i: 0, lambda i: 1 + i // lat_tiles)

    w_in_p = jnp.concatenate([w_in[:, :, :U_OFF], w_in[:, :, U_OFF + 2 * LR_RANK:],
                              w_in[:, :, U_OFF:U_OFF + 2 * LR_RANK]], axis=-1).astype(BF16)
    za = jnp.zeros((DEPTH, LR_RANK, QK_W), F32)
    wa2 = jnp.concatenate([jnp.concatenate([w_a2[:, 0], za], axis=-1),
                           jnp.concatenate([za, w_a2[:, 1]], axis=-1)], axis=1).astype(BF16)
    ba2 = b_a2.reshape(DEPTH, 1, 2 * QK_W)
    w_out_b = w_out.astype(BF16)
    w_fc1_b = w_fc1.astype(BF16)
    w_fc2_b = w_fc2.astype(BF16)
    vec = lambda a: a.reshape(DEPTH, 1, -1)
    g_head_r = g_head.reshape(DEPTH, 1, V_W)

    gapf, gapb, lvlf, lvlb = _gap_matrices()
    as_bf16 = lambda a: jnp.asarray(a, F32).astype(BF16)
    consts = (as_bf16(gapf), as_bf16(gapb), jnp.asarray(lvlf), jnp.asarray(lvlb))
    cw, sw = _dft_tables(FNO_GROUP_W)
    cs_w = as_bf16(np.concatenate([cw, sw], axis=1))
    cn, sn = _dft_tables(seq)
    cos_ctx, nsin_ctx = as_bf16(cn), as_bf16(-sn)
    cos_lat, nsin_lat = _dft_big(dec_seq, GRID_W)

    cvec = jnp.zeros((MOD_ROWS, D_MODEL), F32).at[0].set(c_ctx).at[1:1 + dec_batch].set(c)
    mod = _modulation(cvec, w_ada, b_ada)[:, :1 + dec_batch, None, :]

    xs = [x_prompt.reshape(t_ctx, D_MODEL),
          _embed(x_sample, _pos_emb_2d(dec_seq)).reshape(t_lat, D_MODEL)]
    s0 = (_state_to_kernel(state_gla_fwd), _state_to_kernel(state_gla_bwd))

    st_f, st_b = [], []
    for l in range(DEPTH):
        pre = [_premix(l, x, mod, sel, vec(g_pre_mix), w_in_p, wa2, ba2, cs_w)
               for x, sel in zip(xs, mod_sel)]
        q, k, v, g, la, uc, us = pre[0]
        o_ctx, sf, sb = _gla(l, q, k, v, la, g, g_head_r, consts, seq, batch, 4, emit_state=True)
        yf_ctx = _fnet_ctx(uc, us, cos_ctx, nsin_ctx, seq, batch, 4)
        q, k, v, g, la, uc, us = pre[1]
        (o_lat,) = _gla(l, q, k, v, la, g, g_head_r, consts, dec_seq, dec_batch, 1, s0=s0)
        yf_lat = _fnet_lat(uc, us, cos_lat, nsin_lat, dec_seq, dec_batch)
        xs = [_post(l, x, o, yf, mod, sel, vec(g_post_mix), vec(g_pre_mlp), vec(g_post_mlp),
                    w_out_b, w_fc1_b, w_fc2_b)
              for x, o, yf, sel in zip(xs, (o_ctx, o_lat), (yf_ctx, yf_lat), mod_sel)]
        st_f.append(sf)
        st_b.append(sb)

    y_prompt = xs[0].reshape(batch, seq, D_MODEL)
    y_sample = xs[1].reshape(dec_batch, dec_seq, D_MODEL)
    return (y_prompt, y_sample, jnp.stack(st_f, axis=1), jnp.stack(st_b, axis=1))
```

```python
import functools

import numpy as np
import jax
import jax.numpy as jnp
from jax import lax
from jax.experimental import pallas as pl
from jax.experimental.pallas import tpu as pltpu

D_MODEL = 1024
DEPTH = 4
GRID_W = 64
GLA_HEADS = 4
GLA_DK = 64
GLA_DV = 128
CHUNK = 64
GATE_TAU = 16.0
LR_RANK = 16
FNO_GROUPS = 4
FNO_GROUP_W = 128
D_FF = 4 * D_MODEL
EPS = 1e-6
QK_W = GLA_HEADS * GLA_DK
V_W = GLA_HEADS * GLA_DV
FNO_W = FNO_GROUPS * FNO_GROUP_W
MAIN_W = 2 * QK_W + 2 * V_W
LR_W = 2 * LR_RANK
HEAD_PAIRS = GLA_HEADS // 2
PAIR_K = 2 * GLA_DK
PAIR_V = 2 * GLA_DV
N_LEVELS = 6
N_GAPS = N_LEVELS + 2
SCAN_UNROLL = 8
FAST_STEP_LIMIT = 1.0

VMEM_LIMIT = 48 * 1024 * 1024
ROW_TILE = 512
MOD_ROWS = 8
FFT_UNROLL = 8

BF16 = jnp.bfloat16
F32 = jnp.float32


def _gap_matrices():
    c = CHUNK
    m_all = np.zeros((N_GAPS, c, c), np.float32)
    for p in range(c):
        m_all[0, p, :p + 1] = 1.0
        m_all[1, p, p + 1:] = 1.0
    for lv in range(N_LEVELS):
        m = c >> (lv + 1)
        for p in range(c):
            base = (p // (2 * m)) * 2 * m
            ref = base + m - 1
            if p > ref:
                m_all[2 + lv, p, ref + 1:p + 1] = 1.0
            else:
                m_all[2 + lv, p, p + 1:ref + 1] = 1.0
    fwd = m_all.reshape(N_GAPS * c, c)
    bwd = m_all[:, ::-1, ::-1].reshape(N_GAPS * c, c)
    lvl = np.full((c, c), N_LEVELS + 1, np.int32)
    for t in range(c):
        for s in range(t + 1):
            if s == t:
                lvl[t, s] = 0
            else:
                top = (t ^ s).bit_length() - 1
                lvl[t, s] = N_LEVELS - top
    lvl_f = np.concatenate([lvl, lvl], axis=1)
    lvl_b = np.concatenate([lvl[::-1, ::-1], lvl[::-1, ::-1]], axis=1)
    return fwd, bwd, lvl_f, lvl_b


def _dft_tables(n):
    idx = np.arange(n)
    ang = 2.0 * np.pi * ((idx[:, None] * idx[None, :]) % n) / n
    s = 1.0 / np.sqrt(n)
    return (np.cos(ang) * s).astype(np.float32), (np.sin(ang) * s).astype(np.float32)


def _pos_emb_2d(n_tokens):
    t = jnp.arange(n_tokens)
    r = (t // GRID_W).astype(F32)[:, None]
    col = (t % GRID_W).astype(F32)[:, None]
    nf = D_MODEL // 4
    omega = 1.0 / (10000.0 ** (jnp.arange(nf, dtype=F32) / nf))
    return jnp.concatenate([jnp.sin(r * omega), jnp.cos(r * omega),
                            jnp.sin(col * omega), jnp.cos(col * omega)], axis=-1)


def _silu(x):
    return x * (1.0 / (1.0 + jnp.exp(-x)))


def _rms(x, g):
    return x * lax.rsqrt(jnp.mean(x * x, axis=-1, keepdims=True) + EPS) * g


def _dot(a, b):
    return jnp.dot(a, b, preferred_element_type=F32)


def _dot_nt(a, b):
    return lax.dot_general(a, b, (((1,), (1,)), ((), ())), preferred_element_type=F32)


def _dot_tn(a, b):
    return lax.dot_general(a, b, (((0,), (0,)), ((), ())), preferred_element_type=F32)


def _params(n_grid):
    return pltpu.CompilerParams(dimension_semantics=("arbitrary",) * n_grid,
                                vmem_limit_bytes=VMEM_LIMIT)


def _mod_kernel(c_ref, w_ref, b_ref, o_ref):
    s = _silu(c_ref[...])
    o_ref[...] = jnp.dot(s, w_ref[...], preferred_element_type=F32,
                         precision=lax.Precision.HIGHEST) + b_ref[...]


def _modulation(cvec, w_ada, b_ada):
    tn = 1536
    return pl.pallas_call(
        _mod_kernel,
        grid=(DEPTH, 6 * D_MODEL // tn),
        in_specs=[pl.BlockSpec((MOD_ROWS, D_MODEL), lambda l, j: (0, 0)),
                  pl.BlockSpec((None, D_MODEL, tn), lambda l, j: (l, 0, j)),
                  pl.BlockSpec((None, 1, tn), lambda l, j: (l, 0, j))],
        out_specs=pl.BlockSpec((None, MOD_ROWS, tn), lambda l, j: (l, 0, j)),
        out_shape=jax.ShapeDtypeStruct((DEPTH, MOD_ROWS, 6 * D_MODEL), F32),
        compiler_params=_params(2),
        name="modulation",
    )(cvec, w_ada, b_ada.reshape(DEPTH, 1, 6 * D_MODEL))


def _embed_kernel(x_ref, p_ref, o_ref):
    o_ref[...] = x_ref[...] + p_ref[...]


def _embed(x_sample, pos):
    b, n, d = x_sample.shape
    return pl.pallas_call(
        _embed_kernel,
        grid=(b, n // ROW_TILE),
        in_specs=[pl.BlockSpec((None, ROW_TILE, d), lambda i, j: (i, j, 0)),
                  pl.BlockSpec((ROW_TILE, d), lambda i, j: (j, 0))],
        out_specs=pl.BlockSpec((None, ROW_TILE, d), lambda i, j: (i, j, 0)),
        out_shape=jax.ShapeDtypeStruct(x_sample.shape, x_sample.dtype),
        compiler_params=_params(2),
        name="embed",
    )(x_sample, pos)


def _premix_kernel(x_ref, mod_ref, gpre_ref, win_ref, wu_ref, wlr_ref, wa2_ref, ba2_ref, cs_ref,
                   q_ref, k_ref, v_ref, g_ref, la_ref, uc_ref, us_ref):
    x = x_ref[...]
    sh = mod_ref[:, 0:D_MODEL]
    sc = mod_ref[:, D_MODEL:2 * D_MODEL]
    h = (_rms(x, gpre_ref[...]) * (1.0 + sc) + sh).astype(BF16)
    u = _dot(h, wu_ref[...]).astype(BF16)
    lr = _dot(h, wlr_ref[...]).astype(BF16)
    q_ref[...] = _dot(h, win_ref[:, 0:QK_W]) * (GLA_DK ** -0.5)
    k_ref[...] = _dot(h, win_ref[:, QK_W:2 * QK_W])
    v_ref[...] = _dot(h, win_ref[:, 2 * QK_W:2 * QK_W + V_W]).astype(BF16)
    g_ref[...] = _dot(h, win_ref[:, 2 * QK_W + V_W:MAIN_W]).astype(BF16)
    logit = _dot(lr, wa2_ref[...]) + ba2_ref[...]
    softplus_neg = jnp.maximum(-logit, 0.0) + jnp.log(1.0 + jnp.exp(-jnp.abs(logit)))
    la_ref[...] = softplus_neg * (-1.0 / GATE_TAU)
    for grp in range(FNO_GROUPS):
        sl = slice(grp * FNO_GROUP_W, (grp + 1) * FNO_GROUP_W)
        ucs = _dot(u[:, sl], cs_ref[...])
        uc_ref[:, sl] = ucs[:, :FNO_GROUP_W].astype(uc_ref.dtype)
        us_ref[:, sl] = ucs[:, FNO_GROUP_W:].astype(us_ref.dtype)


def _premix(layer, x, mod, mod_sel, g_pre, w_main, w_u, w_lr, wa2, ba2, cs_w, fno_dtype):
    t = x.shape[0]
    row = lambda w: pl.BlockSpec((ROW_TILE, w), lambda i: (i, 0))
    lay = lambda *shape: pl.BlockSpec((None,) + shape, lambda i: (layer,) + (0,) * len(shape))
    out = lambda w, dt: jax.ShapeDtypeStruct((t, w), dt)
    return pl.pallas_call(
        _premix_kernel,
        grid=(t // ROW_TILE,),
        in_specs=[row(D_MODEL),
                  pl.BlockSpec((None, None, 1, 6 * D_MODEL), lambda i: (layer, mod_sel(i), 0, 0)),
                  lay(1, D_MODEL), lay(D_MODEL, MAIN_W), lay(D_MODEL, FNO_W), lay(D_MODEL, LR_W),
                  lay(LR_W, 2 * QK_W), lay(1, 2 * QK_W),
                  pl.BlockSpec((FNO_GROUP_W, 2 * FNO_GROUP_W), lambda i: (0, 0))],
        out_specs=[row(QK_W), row(QK_W), row(V_W), row(V_W), row(2 * QK_W), row(FNO_W), row(FNO_W)],
        out_shape=[out(QK_W, F32), out(QK_W, F32), out(V_W, BF16), out(V_W, BF16),
                   out(2 * QK_W, F32), out(FNO_W, fno_dtype), out(FNO_W, fno_dtype)],
        compiler_params=_params(1),
        name="premix",
    )(x, mod, g_pre, w_main, w_u, w_lr, wa2, ba2, cs_w)


def _gla_chunk(r0, forward, q_ref, k_ref, v_ref, la_ref, gap_ref, lvl_ref, st_ref):
    rows = pl.ds(r0, CHUNK)
    lane = lax.broadcasted_iota(jnp.int32, (CHUNK, PAIR_K), 1)
    head0 = lane < GLA_DK
    la = la_ref[rows, :]
    la_hi = la.astype(BF16)
    la_lo = (la - la_hi.astype(F32)).astype(BF16)
    gaps = _dot(gap_ref[...], jnp.concatenate([la_hi, la_lo], axis=1))
    decay = jnp.exp(gaps[:, :PAIR_K] + gaps[:, PAIR_K:])
    blk = lambda j: decay[j * CHUNK:(j + 1) * CHUNK]
    q = q_ref[rows, :]
    k = k_ref[rows, :]
    k_h = (jnp.where(head0, k, 0.0), jnp.where(head0, 0.0, k))
    v = v_ref[rows, :]
    lvl = lvl_ref[...]

    a = jnp.zeros((CHUNK, 2 * CHUNK), F32)
    for lv in range(N_LEVELS + 1):
        if lv == 0:
            ql = q.astype(BF16)
            kl = jnp.concatenate([k_h[0].astype(BF16), k_h[1].astype(BF16)], axis=0)
        else:
            d = blk(1 + lv)
            ql = (q * d).astype(BF16)
            kl = jnp.concatenate([(k_h[0] * d).astype(BF16), (k_h[1] * d).astype(BF16)], axis=0)
        a = jnp.where(lvl == lv, _dot_nt(ql, kl), a)

    q_in = (q * blk(0)).astype(BF16)
    st = jnp.concatenate([st_ref[0].astype(BF16), st_ref[1].astype(BF16)], axis=0)
    inter = _dot_nt(q_in, st)
    vlane = lax.broadcasted_iota(jnp.int32, (CHUNK, PAIR_V), 1)
    zero = jnp.zeros_like(v)
    v_blk = jnp.concatenate([jnp.where(vlane < GLA_DV, v, zero),
                             jnp.where(vlane < GLA_DV, zero, v)], axis=0)
    o = inter + _dot(a.astype(BF16), v_blk)

    d_out = blk(1)
    total = blk(0)[CHUNK - 1:CHUNK] if forward else blk(0)[0:1]
    for hd in range(2):
        k_out = (k_h[hd] * d_out).astype(BF16)
        upd = _dot_tn(v[:, hd * GLA_DV:(hd + 1) * GLA_DV], k_out)
        st_ref[hd] = st_ref[hd] * total + upd
    return o


def _gla_fast_step(jobs, fwd_refs, bwd_refs):
    refs = [fwd_refs if fw else bwd_refs for _, fw, _ in jobs]
    rows = [pl.ds(r0, CHUNK) for r0, _, _ in jobs]
    lane = lax.broadcasted_iota(jnp.int32, (CHUNK, PAIR_K), 1)
    head0 = lane < GLA_DK
    vlane = lax.broadcasted_iota(jnp.int32, (CHUNK, PAIR_V), 1) < GLA_DV

    la = [r[3][rw, :] for r, rw in zip(refs, rows)]
    la_hi = [x.astype(BF16) for x in la]
    la_lo = [(x - h.astype(F32)).astype(BF16) for x, h in zip(la, la_hi)]
    cum = [_dot(r[4][0:CHUNK, :], jnp.concatenate([h, l], axis=1))
           for r, h, l in zip(refs, la_hi, la_lo)]
    b = [x[:, :PAIR_K] + x[:, PAIR_K:] for x in cum]
    b_end = [x[CHUNK - 1:CHUNK] if fw else x[0:1] for x, (_, fw, _) in zip(b, jobs)]
    q = [r[0][rw, :] for r, rw in zip(refs, rows)]
    k = [r[1][rw, :] for r, rw in zip(refs, rows)]
    v = [r[2][rw, :] for r, rw in zip(refs, rows)]
    qd = [(x * jnp.exp(y)).astype(BF16) for x, y in zip(q, b)]
    kd = [x * jnp.exp(-y) for x, y in zip(k, b)]
    ko = [x * jnp.exp(e - y) for x, y, e in zip(k, b, b_end)]
    total = [jnp.exp(e) for e in b_end]
    kd_cat = [jnp.concatenate([jnp.where(head0, x, 0.0).astype(BF16),
                               jnp.where(head0, 0.0, x).astype(BF16)], axis=0) for x in kd]
    score = [_dot_nt(x, y) for x, y in zip(qd, kd_cat)]
    a = [jnp.where(r[5][...] <= N_LEVELS, s, 0.0).astype(BF16) for r, s in zip(refs, score)]
    v_blk = [jnp.concatenate([jnp.where(vlane, x, jnp.zeros_like(x)),
                              jnp.where(vlane, jnp.zeros_like(x), x)], axis=0) for x in v]
    intra = [_dot(x, y) for x, y in zip(a, v_blk)]
    ko_h = [(jnp.where(head0, x, 0.0).astype(BF16), jnp.where(head0, 0.0, x).astype(BF16)) for x in ko]
    upd = [[_dot_tn(x[:, hd * GLA_DV:(hd + 1) * GLA_DV], y[hd]) for hd in range(2)]
           for x, y in zip(v, ko_h)]

    outs = [None] * len(jobs)
    for scan in sorted({(fw, slot) for _, fw, slot in jobs}):
        st_ref = (fwd_refs if scan[0] else bwd_refs)[6]
        st = [st_ref[scan[1], 0], st_ref[scan[1], 1]]
        for j, (_, fw, slot) in enumerate(jobs):
            if (fw, slot) != scan:
                continue
            st_cat = jnp.concatenate([st[0].astype(BF16), st[1].astype(BF16)], axis=0)
            outs[j] = _dot_nt(qd[j], st_cat) + intra[j]
            st = [st[hd] * total[j] + upd[j][hd] for hd in range(2)]
        st_ref[scan[1], 0] = st[0]
        st_ref[scan[1], 1] = st[1]
    return outs


def _gla_kernel(*refs, seq_len, n_seq, has_state, emit_state, unroll, par):
    (q_ref, k_ref, v_ref, laf_ref, lab_ref, g_ref, gh_ref,
     gapf_ref, gapb_ref, lvlf_ref, lvlb_ref) = refs[:11]
    pos = 11
    if has_state:
        s0f_ref, s0b_ref = refs[pos:pos + 2]
        pos += 2
    o_ref = refs[pos]
    pos += 1
    if emit_state:
        sf_ref, sb_ref = refs[pos:pos + 2]
        pos += 2
    acc_ref, stf_ref, stb_ref = refs[pos:pos + 3]
    n_chunks = seq_len // CHUNK
    half_iters = n_chunks // (2 * unroll)
    fwd_refs = (q_ref, k_ref, v_ref, laf_ref, gapf_ref, lvlf_ref, stf_ref)
    bwd_refs = (q_ref, k_ref, v_ref, lab_ref, gapb_ref, lvlb_ref, stb_ref)

    def finish(o, rows):
        g = g_ref[rows, :].astype(F32)
        for hd in range(2):
            sl = slice(hd * GLA_DV, (hd + 1) * GLA_DV)
            oh = _rms(o[:, sl], gh_ref[:, sl]) * _silu(g[:, sl])
            o_ref[rows, sl] = oh.astype(o_ref.dtype)

    def seq_body(sg, carry, fast):
        base = sg * (par * seq_len)
        if has_state:
            stf_ref[0] = s0f_ref[...]
            stb_ref[0] = s0b_ref[...]
        else:
            stf_ref[...] = jnp.zeros_like(stf_ref)
            stb_ref[...] = jnp.zeros_like(stb_ref)

        def step(it, second_visit):
            jobs = []
            for slot in range(par):
                for u in range(unroll):
                    cf = it * unroll + u
                    for forward, ci in ((True, cf), (False, n_chunks - 1 - cf)):
                        r0 = base + slot * seq_len + ci * CHUNK
                        jobs.append((pl.multiple_of(r0, CHUNK), forward, slot))
            if fast:
                outs = _gla_fast_step(jobs, fwd_refs, bwd_refs)
            else:
                outs = [_gla_chunk(r0, fw, *(fwd_refs if fw else bwd_refs)[:6],
                                   (stf_ref if fw else stb_ref).at[slot]) for r0, fw, slot in jobs]
            for (r0, _, _), o in zip(jobs, outs):
                rows = pl.ds(r0, CHUNK)
                if second_visit:
                    finish(o + acc_ref[rows, :], rows)
                else:
                    acc_ref[rows, :] = o

        def first(i, c):
            step(i, False)
            return c

        def second(i, c):
            step(half_iters + i, True)
            return c

        lax.fori_loop(0, half_iters, first, 0)
        lax.fori_loop(0, half_iters, second, 0)
        if emit_state:
            for s_out, st_ref in ((sf_ref, stf_ref), (sb_ref, stb_ref)):
                for slot in range(par):
                    for hd in range(2):
                        s_out[sg * par + slot, hd] = st_ref[slot, hd].T[hd * GLA_DK:(hd + 1) * GLA_DK, :]
        return carry

    weakest = jnp.minimum(jnp.min(laf_ref[...]), jnp.min(lab_ref[...]))
    is_fast = weakest >= -FAST_STEP_LIMIT

    @pl.when(is_fast)
    def _():
        lax.fori_loop(0, n_seq // par, functools.partial(seq_body, fast=True), 0)

    @pl.when(jnp.logical_not(is_fast))
    def _():
        lax.fori_loop(0, n_seq // par, functools.partial(seq_body, fast=False), 0)


def _gla(layer, q, k, v, la, g, g_head, consts, seq_len, n_seq_total, n_seq, s0=None,
         emit_state=False):
    gapf, gapb, lvlf, lvlb = consts
    rows = seq_len * n_seq
    n_steps = n_seq_total // n_seq
    rspec = lambda w, off=0: pl.BlockSpec((rows, w), lambda i, j: (i, j + off))
    const = lambda a: pl.BlockSpec(a.shape, lambda i, j: (0,) * a.ndim)
    in_specs = [rspec(PAIR_K), rspec(PAIR_K), rspec(PAIR_V), rspec(PAIR_K), rspec(PAIR_K, HEAD_PAIRS),
                rspec(PAIR_V),
                pl.BlockSpec((None, 1, PAIR_V), lambda i, j: (layer, 0, j)),
                const(gapf), const(gapb), const(lvlf), const(lvlb)]
    args = [q, k, v, la, la, g, g_head, gapf, gapb, lvlf, lvlb]
    if s0 is not None:
        sspec = pl.BlockSpec((None, None, 2, GLA_DV, PAIR_K), lambda i, j: (i, layer, j, 0, 0))
        in_specs += [sspec, sspec]
        args += list(s0)
    out_specs = [rspec(PAIR_V)]
    out_shape = [jax.ShapeDtypeStruct((n_seq_total * seq_len, V_W), BF16)]
    if emit_state:
        st_spec = pl.BlockSpec((n_seq, 2, GLA_DK, GLA_DV), lambda i, j: (i, j, 0, 0))
        st_shape = jax.ShapeDtypeStruct((n_seq_total, GLA_HEADS, GLA_DK, GLA_DV), F32)
        out_specs += [st_spec, st_spec]
        out_shape += [st_shape, st_shape]
    unroll = min(SCAN_UNROLL, seq_len // (2 * CHUNK))
    par = 1 if s0 is not None else min(n_seq, SCAN_UNROLL // unroll)
    kern = functools.partial(_gla_kernel, seq_len=seq_len, n_seq=n_seq, has_state=s0 is not None,
                             emit_state=emit_state, unroll=unroll, par=par)
    return pl.pallas_call(
        kern,
        grid=(n_steps, HEAD_PAIRS),
        in_specs=in_specs,
        out_specs=out_specs,
        out_shape=out_shape,
        scratch_shapes=[pltpu.VMEM((rows, PAIR_V), F32), pltpu.VMEM((par, 2, GLA_DV, PAIR_K), F32),
                        pltpu.VMEM((par, 2, GLA_DV, PAIR_K), F32)],
        compiler_params=_params(2),
        name="gla_state" if emit_state else "gla",
    )(*args)


def _fnet_kernel(c_ref, ns_ref, uc_ref, us_ref, o_ref, *, seq_len, n_seq):
    for sq in range(n_seq):
        rows = slice(sq * seq_len, (sq + 1) * seq_len)
        y = _dot(c_ref[...], uc_ref[rows, :]) + _dot(ns_ref[...], us_ref[rows, :])
        o_ref[rows, :] = y.astype(o_ref.dtype)


def _fnet_ctx(uc, us, cos_n, nsin_n, seq_len, n_seq_total, n_seq):
    rows = seq_len * n_seq
    rspec = pl.BlockSpec((rows, FNO_W), lambda i: (i, 0))
    cspec = pl.BlockSpec((seq_len, seq_len), lambda i: (0, 0))
    return pl.pallas_call(
        functools.partial(_fnet_kernel, seq_len=seq_len, n_seq=n_seq),
        grid=(n_seq_total // n_seq,),
        in_specs=[cspec, cspec, rspec, rspec],
        out_specs=rspec,
        out_shape=jax.ShapeDtypeStruct((n_seq_total * seq_len, FNO_W), BF16),
        compiler_params=_params(1),
        name="fnet_ctx",
    )(cos_n, nsin_n, uc, us)


def _fft_tables(n_side):
    c, s = _dft_tables(n_side)
    l1 = np.block([[c, -s], [-s, -c]])
    l2 = np.concatenate([c, s], axis=1)
    idx = np.arange(n_side)
    ang = 2.0 * np.pi * (idx[:, None] * idx[None, :]) / (n_side * n_side)
    lanes = np.ones((1, 1, FNO_GROUP_W), np.float32)
    tc = np.cos(ang).astype(np.float32)[:, :, None] * lanes
    ts = np.sin(ang).astype(np.float32)[:, :, None] * lanes
    return l1, l2, tc, ts


def _fft_kernel(uc_ref, us_ref, l1_ref, l2_ref, tc_ref, ts_ref, o_ref, zr_ref, zi_ref, *, n_side):
    def stage1(n2, carry):
        col = pl.ds(n2, n_side, stride=n_side)
        x = jnp.concatenate([uc_ref[col, :].astype(BF16), us_ref[col, :].astype(BF16)], axis=0)
        z = _dot(l1_ref[...], x)
        zr, zi = z[:n_side], z[n_side:]
        tc, ts = tc_ref[n2], ts_ref[n2]
        out = pl.ds(pl.multiple_of(n2 * n_side, n_side), n_side)
        zr_ref[out, :] = zr * tc + zi * ts
        zi_ref[out, :] = zi * tc - zr * ts
        return carry

    def stage2(k1, carry):
        col = pl.ds(k1, n_side, stride=n_side)
        z = jnp.concatenate([zr_ref[col, :].astype(BF16), zi_ref[col, :].astype(BF16)], axis=0)
        o_ref[col, :] = _dot(l2_ref[...], z)
        return carry

    lax.fori_loop(0, n_side, stage1, 0, unroll=FFT_UNROLL)
    lax.fori_loop(0, n_side, stage2, 0, unroll=FFT_UNROLL)


def _fnet_fft(uc, us, seq_len, n_seq_total):
    n_side = GRID_W
    l1, l2, tc, ts = _fft_tables(n_side)
    as_bf16 = lambda a: jnp.asarray(a, F32).astype(BF16)
    blk = pl.BlockSpec((seq_len, FNO_GROUP_W), lambda b, g: (b, g))
    const = lambda a: pl.BlockSpec(a.shape, lambda b, g: (0,) * a.ndim)
    return pl.pallas_call(
        functools.partial(_fft_kernel, n_side=n_side),
        grid=(n_seq_total, FNO_GROUPS),
        in_specs=[blk, blk, const(l1), const(l2), const(tc), const(ts)],
        out_specs=blk,
        out_shape=jax.ShapeDtypeStruct((n_seq_total * seq_len, FNO_W), F32),
        scratch_shapes=[pltpu.VMEM((seq_len, FNO_GROUP_W), F32), pltpu.VMEM((seq_len, FNO_GROUP_W), F32)],
        compiler_params=_params(2),
        name="fnet_fft",
    )(uc, us, as_bf16(l1), as_bf16(l2), jnp.asarray(tc), jnp.asarray(ts))


def _post_kernel(x_ref, o_ref, yf_ref, mod_ref, gpm_ref, gpre_ref, gpost_ref,
                 wout_ref, w1_ref, w2_ref, out_ref):
    gt_m = mod_ref[:, 2 * D_MODEL:3 * D_MODEL]
    sh_f = mod_ref[:, 3 * D_MODEL:4 * D_MODEL]
    sc_f = mod_ref[:, 4 * D_MODEL:5 * D_MODEL]
    gt_f = mod_ref[:, 5 * D_MODEL:6 * D_MODEL]
    y = _dot(o_ref[...], wout_ref[0:V_W, :]) + _dot(yf_ref[...].astype(BF16), wout_ref[V_W:, :])
    x = x_ref[...] + gt_m * _rms(y, gpm_ref[...])
    h = (_rms(x, gpre_ref[...]) * (1.0 + sc_f) + sh_f).astype(BF16)
    f = jnp.zeros((ROW_TILE, D_MODEL), F32)
    n_split = 4
    wf = D_FF // n_split
    for j in range(n_split):
        a = jnp.maximum(_dot(h, w1_ref[:, j * wf:(j + 1) * wf]), 0.0)
        f = f + _dot((a * a).astype(BF16), w2_ref[j * wf:(j + 1) * wf, :])
    out_ref[...] = x + gt_f * _rms(f, gpost_ref[...])


def _post(layer, x, o, yf, mod, mod_sel, g_post_mix, g_pre_mlp, g_post_mlp, w_out, w_fc1, w_fc2):
    t = x.shape[0]
    row = lambda w: pl.BlockSpec((ROW_TILE, w), lambda i: (i, 0))
    lay = lambda *shape: pl.BlockSpec((None,) + shape, lambda i: (layer,) + (0,) * len(shape),
                                      pipeline_mode=pl.Buffered(1))
    return pl.pallas_call(
        _post_kernel,
        grid=(t // ROW_TILE,),
        in_specs=[row(D_MODEL), row(V_W), row(FNO_W),
                  pl.BlockSpec((None, None, 1, 6 * D_MODEL), lambda i: (layer, mod_sel(i), 0, 0)),
                  lay(1, D_MODEL), lay(1, D_MODEL), lay(1, D_MODEL),
                  lay(V_W + FNO_W, D_MODEL), lay(D_MODEL, D_FF), lay(D_FF, D_MODEL)],
        out_specs=row(D_MODEL),
        out_shape=jax.ShapeDtypeStruct((t, D_MODEL), F32),
        compiler_params=_params(1),
        name="post",
    )(x, o, yf, mod, g_post_mix, g_pre_mlp, g_post_mlp, w_out, w_fc1, w_fc2)


def _state_to_kernel(s):
    st = jnp.swapaxes(s, -1, -2)
    z = jnp.zeros_like(st)
    even = jnp.concatenate([st, z], axis=-1)
    odd = jnp.concatenate([z, st], axis=-1)
    is_even = (jnp.arange(GLA_HEADS) % 2 == 0)[None, None, :, None, None]
    return jnp.where(is_even, even, odd)


def kernel(x_prompt, x_sample, state_gla_fwd, state_gla_bwd, c, c_ctx, w_in, w_out, w_a2, b_a2,
           g_head, w_ada, b_ada, g_pre_mix, g_post_mix, g_pre_mlp, g_post_mlp, w_fc1, w_fc2):
    batch, seq, _ = x_prompt.shape
    dec_batch, dec_seq, _ = x_sample.shape
    t_ctx = batch * seq
    t_lat = dec_batch * dec_seq
    lat_tiles = dec_seq // ROW_TILE
    mod_sel = (lambda i: 0, lambda i: 1 + i // lat_tiles)

    w_main = w_in[:, :, :MAIN_W].astype(BF16)
    w_lr = w_in[:, :, MAIN_W:MAIN_W + LR_W].astype(BF16)
    w_u = w_in[:, :, MAIN_W + LR_W:].astype(BF16)
    za = jnp.zeros((DEPTH, LR_RANK, QK_W), F32)
    wa2 = jnp.concatenate([jnp.concatenate([w_a2[:, 0], za], axis=-1),
                           jnp.concatenate([za, w_a2[:, 1]], axis=-1)], axis=1).astype(BF16)
    ba2 = b_a2.reshape(DEPTH, 1, 2 * QK_W)
    w_out_b = w_out.astype(BF16)
    w_fc1_b = w_fc1.astype(BF16)
    w_fc2_b = w_fc2.astype(BF16)
    vec = lambda a: a.reshape(DEPTH, 1, -1)
    g_head_r = g_head.reshape(DEPTH, 1, V_W)

    gapf, gapb, lvlf, lvlb = _gap_matrices()
    as_bf16 = lambda a: jnp.asarray(a, F32).astype(BF16)
    consts = (as_bf16(gapf), as_bf16(gapb), jnp.asarray(lvlf), jnp.asarray(lvlb))
    cw, sw = _dft_tables(FNO_GROUP_W)
    cs_w = as_bf16(np.concatenate([cw, sw], axis=1))
    cn, sn = _dft_tables(seq)
    cos_ctx, nsin_ctx = as_bf16(cn), as_bf16(-sn)

    cvec = jnp.zeros((MOD_ROWS, D_MODEL), F32).at[0].set(c_ctx).at[1:1 + dec_batch].set(c)
    mod = _modulation(cvec, w_ada, b_ada)[:, :1 + dec_batch, None, :]

    xs = [x_prompt.reshape(t_ctx, D_MODEL),
          _embed(x_sample, _pos_emb_2d(dec_seq)).reshape(t_lat, D_MODEL)]
    s0 = (_state_to_kernel(state_gla_fwd), _state_to_kernel(state_gla_bwd))

    st_f, st_b = [], []
    for l in range(DEPTH):
        pre = [_premix(l, x, mod, sel, vec(g_pre_mix), w_main, w_u, w_lr, wa2, ba2, cs_w, dt)
               for x, sel, dt in zip(xs, mod_sel, (BF16, F32))]
        q, k, v, g, la, uc, us = pre[0]
        o_ctx, sf, sb = _gla(l, q, k, v, la, g, g_head_r, consts, seq, batch, 4, emit_state=True)
        yf_ctx = _fnet_ctx(uc, us, cos_ctx, nsin_ctx, seq, batch, 4)
        q, k, v, g, la, uc, us = pre[1]
        (o_lat,) = _gla(l, q, k, v, la, g, g_head_r, consts, dec_seq, dec_batch, 1, s0=s0)
        yf_lat = _fnet_fft(uc, us, dec_seq, dec_batch)
        xs = [_post(l, x, o, yf, mod, sel, vec(g_post_mix), vec(g_pre_mlp), vec(g_post_mlp),
                    w_out_b, w_fc1_b, w_fc2_b)
              for x, o, yf, sel in zip(xs, (o_ctx, o_lat), (yf_ctx, yf_lat), mod_sel)]
        st_f.append(sf)
        st_b.append(sb)

    y_prompt = xs[0].reshape(batch, seq, D_MODEL)
    y_sample = xs[1].reshape(dec_batch, dec_seq, D_MODEL)
    return (y_prompt, y_sample, jnp.stack(st_f, axis=1), jnp.stack(st_b, axis=1))
```

```python
import functools

import numpy as np
import jax
import jax.numpy as jnp
from jax import lax
from jax.experimental import pallas as pl
from jax.experimental.pallas import tpu as pltpu

D_MODEL = 1024
DEPTH = 4
GRID_W = 64
GLA_HEADS = 4
GLA_DK = 64
GLA_DV = 128
CHUNK = 64
GATE_TAU = 16.0
LR_RANK = 16
FNO_GROUPS = 4
FNO_GROUP_W = 128
D_FF = 4 * D_MODEL
EPS = 1e-6
QK_W = GLA_HEADS * GLA_DK
V_W = GLA_HEADS * GLA_DV
FNO_W = FNO_GROUPS * FNO_GROUP_W
MAIN_W = 2 * QK_W + 2 * V_W
LR_W = 2 * LR_RANK
D_IN = MAIN_W + LR_W + FNO_W
HEAD_PAIRS = GLA_HEADS // 2
PAIR_K = 2 * GLA_DK
PAIR_V = 2 * GLA_DV
N_LEVELS = 6
N_GAPS = N_LEVELS + 2
SCAN_UNROLL = 8
FAST_STEP_LIMIT = 1.0

VMEM_LIMIT = 48 * 1024 * 1024
ROW_TILE = 512
MOD_ROWS = 8
FFT_UNROLL = 8

BF16 = jnp.bfloat16
F32 = jnp.float32


def _gap_matrices():
    c = CHUNK
    m_all = np.zeros((N_GAPS, c, c), np.float32)
    for p in range(c):
        m_all[0, p, :p + 1] = 1.0
        m_all[1, p, p + 1:] = 1.0
    for lv in range(N_LEVELS):
        m = c >> (lv + 1)
        for p in range(c):
            base = (p // (2 * m)) * 2 * m
            ref = base + m - 1
            if p > ref:
                m_all[2 + lv, p, ref + 1:p + 1] = 1.0
            else:
                m_all[2 + lv, p, p + 1:ref + 1] = 1.0
    fwd = m_all.reshape(N_GAPS * c, c)
    bwd = m_all[:, ::-1, ::-1].reshape(N_GAPS * c, c)
    lvl = np.full((c, c), N_LEVELS + 1, np.int32)
    for t in range(c):
        for s in range(t + 1):
            if s == t:
                lvl[t, s] = 0
            else:
                top = (t ^ s).bit_length() - 1
                lvl[t, s] = N_LEVELS - top
    lvl_f = np.concatenate([lvl, lvl], axis=1)
    lvl_b = np.concatenate([lvl[::-1, ::-1], lvl[::-1, ::-1]], axis=1)
    return fwd, bwd, lvl_f, lvl_b


def _dft_tables(n):
    idx = np.arange(n)
    ang = 2.0 * np.pi * ((idx[:, None] * idx[None, :]) % n) / n
    s = 1.0 / np.sqrt(n)
    return (np.cos(ang) * s).astype(np.float32), (np.sin(ang) * s).astype(np.float32)


def _silu(x):
    return x * (1.0 / (1.0 + jnp.exp(-x)))


def _rms(x, g):
    return x * lax.rsqrt(jnp.mean(x * x, axis=-1, keepdims=True) + EPS) * g


def _dot(a, b):
    return jnp.dot(a, b, preferred_element_type=F32)


def _dot_nt(a, b):
    return lax.dot_general(a, b, (((1,), (1,)), ((), ())), preferred_element_type=F32)


def _dot_tn(a, b):
    return lax.dot_general(a, b, (((0,), (0,)), ((), ())), preferred_element_type=F32)


def _params(n_grid):
    return pltpu.CompilerParams(dimension_semantics=("arbitrary",) * n_grid,
                                vmem_limit_bytes=VMEM_LIMIT)


def _mod_kernel(c_ref, w_ref, b_ref, o_ref):
    s = _silu(c_ref[...])
    o_ref[...] = jnp.dot(s, w_ref[...], preferred_element_type=F32,
                         precision=lax.Precision.HIGHEST) + b_ref[...]


def _modulation(cvec, w_ada, b_ada):
    tn = 3072
    return pl.pallas_call(
        _mod_kernel,
        grid=(DEPTH, 6 * D_MODEL // tn),
        in_specs=[pl.BlockSpec((MOD_ROWS, D_MODEL), lambda l, j: (0, 0)),
                  pl.BlockSpec((None, D_MODEL, tn), lambda l, j: (l, 0, j)),
                  pl.BlockSpec((None, 1, tn), lambda l, j: (l, 0, j))],
        out_specs=pl.BlockSpec((None, MOD_ROWS, tn), lambda l, j: (l, 0, j)),
        out_shape=jax.ShapeDtypeStruct((DEPTH, MOD_ROWS, 6 * D_MODEL), F32),
        compiler_params=_params(2),
        name="modulation",
    )(cvec, w_ada, b_ada.reshape(DEPTH, 1, 6 * D_MODEL))


def _pos_table():
    nf = D_MODEL // 4
    omega = 1.0 / (10000.0 ** (np.arange(nf, dtype=np.float64) / nf))
    idx = np.arange(GRID_W, dtype=np.float64)[:, None]
    return np.concatenate([np.sin(idx * omega), np.cos(idx * omega)], axis=1).astype(np.float32)


def _embed_kernel(x_ref, tab_ref, o_ref):
    grid_rows = ROW_TILE // GRID_W
    r0 = pl.multiple_of(pl.program_id(1) * grid_rows, grid_rows)
    t = tab_ref[pl.ds(r0, grid_rows), :]
    by_row = jnp.concatenate([jnp.broadcast_to(t[i:i + 1], (GRID_W, D_MODEL // 2))
                              for i in range(grid_rows)], axis=0)
    by_col = jnp.concatenate([tab_ref[...]] * grid_rows, axis=0)
    o_ref[...] = x_ref[...] + jnp.concatenate([by_row, by_col], axis=1)


def _embed(x_sample):
    b, n, d = x_sample.shape
    tab = jnp.asarray(_pos_table())
    return pl.pallas_call(
        _embed_kernel,
        grid=(b, n // ROW_TILE),
        in_specs=[pl.BlockSpec((None, ROW_TILE, d), lambda i, j: (i, j, 0)),
                  pl.BlockSpec(tab.shape, lambda i, j: (0, 0))],
        out_specs=pl.BlockSpec((None, ROW_TILE, d), lambda i, j: (i, j, 0)),
        out_shape=jax.ShapeDtypeStruct(x_sample.shape, x_sample.dtype),
        compiler_params=_params(2),
        name="embed",
    )(x_sample, tab)


def _premix_kernel(x_ref, mod_ref, gpre_ref, w_ref, wa2_ref, ba2_ref, cs_ref,
                   q_ref, k_ref, v_ref, g_ref, la_ref, uc_ref, us_ref, win_ref, wu_ref, wlr_ref):
    @pl.when(pl.program_id(0) == 0)
    def _():
        win_ref[...] = w_ref[:, 0:MAIN_W].astype(BF16)
        wlr_ref[...] = w_ref[:, MAIN_W:MAIN_W + LR_W].astype(BF16)
        wu_ref[...] = w_ref[:, MAIN_W + LR_W:].astype(BF16)

    x = x_ref[...]
    sh = mod_ref[:, 0:D_MODEL]
    sc = mod_ref[:, D_MODEL:2 * D_MODEL]
    h = (_rms(x, gpre_ref[...]) * (1.0 + sc) + sh).astype(BF16)
    u = _dot(h, wu_ref[...]).astype(BF16)
    lr = _dot(h, wlr_ref[...]).astype(BF16)
    q_ref[...] = _dot(h, win_ref[:, 0:QK_W]) * (GLA_DK ** -0.5)
    k_ref[...] = _dot(h, win_ref[:, QK_W:2 * QK_W])
    v_ref[...] = _dot(h, win_ref[:, 2 * QK_W:2 * QK_W + V_W]).astype(BF16)
    g_ref[...] = _dot(h, win_ref[:, 2 * QK_W + V_W:MAIN_W]).astype(BF16)
    logit = _dot(lr, wa2_ref[...]) + ba2_ref[...]
    softplus_neg = jnp.maximum(-logit, 0.0) + jnp.log(1.0 + jnp.exp(-jnp.abs(logit)))
    la_ref[...] = softplus_neg * (-1.0 / GATE_TAU)
    for grp in range(FNO_GROUPS):
        sl = slice(grp * FNO_GROUP_W, (grp + 1) * FNO_GROUP_W)
        ucs = _dot(u[:, sl], cs_ref[...])
        uc_ref[:, sl] = ucs[:, :FNO_GROUP_W].astype(uc_ref.dtype)
        us_ref[:, sl] = ucs[:, FNO_GROUP_W:].astype(us_ref.dtype)


def _premix(layer, x, mod, mod_sel, g_pre, w_in, wa2, ba2, cs_w, fno_dtype):
    t = x.shape[0]
    row = lambda w: pl.BlockSpec((ROW_TILE, w), lambda i: (i, 0))
    lay = lambda *shape: pl.BlockSpec((None,) + shape, lambda i: (layer,) + (0,) * len(shape))
    out = lambda w, dt: jax.ShapeDtypeStruct((t, w), dt)
    return pl.pallas_call(
        _premix_kernel,
        grid=(t // ROW_TILE,),
        in_specs=[row(D_MODEL),
                  pl.BlockSpec((None, None, 1, 6 * D_MODEL), lambda i: (layer, mod_sel(i), 0, 0)),
                  lay(1, D_MODEL),
                  pl.BlockSpec((None, D_MODEL, D_IN), lambda i: (layer, 0, 0), pipeline_mode=pl.Buffered(1)),
                  lay(LR_W, 2 * QK_W), lay(1, 2 * QK_W),
                  pl.BlockSpec((FNO_GROUP_W, 2 * FNO_GROUP_W), lambda i: (0, 0))],
        out_specs=[row(QK_W), row(QK_W), row(V_W), row(V_W), row(2 * QK_W), row(FNO_W), row(FNO_W)],
        out_shape=[out(QK_W, F32), out(QK_W, F32), out(V_W, BF16), out(V_W, BF16),
                   out(2 * QK_W, F32), out(FNO_W, fno_dtype), out(FNO_W, fno_dtype)],
        scratch_shapes=[pltpu.VMEM((D_MODEL, MAIN_W), BF16), pltpu.VMEM((D_MODEL, FNO_W), BF16),
                        pltpu.VMEM((D_MODEL, LR_W), BF16)],
        compiler_params=_params(1),
        name="premix",
    )(x, mod, g_pre, w_in, wa2, ba2, cs_w)


def _gla_chunk(r0, forward, q_ref, k_ref, v_ref, la_ref, gap_ref, lvl_ref, st_ref):
    rows = pl.ds(r0, CHUNK)
    lane = lax.broadcasted_iota(jnp.int32, (CHUNK, PAIR_K), 1)
    head0 = lane < GLA_DK
    la = la_ref[rows, :]
    la_hi = la.astype(BF16)
    la_lo = (la - la_hi.astype(F32)).astype(BF16)
    gaps = _dot(gap_ref[...], jnp.concatenate([la_hi, la_lo], axis=1))
    decay = jnp.exp(gaps[:, :PAIR_K] + gaps[:, PAIR_K:])
    blk = lambda j: decay[j * CHUNK:(j + 1) * CHUNK]
    q = q_ref[rows, :]
    k = k_ref[rows, :]
    k_h = (jnp.where(head0, k, 0.0), jnp.where(head0, 0.0, k))
    v = v_ref[rows, :]
    lvl = lvl_ref[...]

    a = jnp.zeros((CHUNK, 2 * CHUNK), F32)
    for lv in range(N_LEVELS + 1):
        if lv == 0:
            ql = q.astype(BF16)
            kl = jnp.concatenate([k_h[0].astype(BF16), k_h[1].astype(BF16)], axis=0)
        else:
            d = blk(1 + lv)
            ql = (q * d).astype(BF16)
            kl = jnp.concatenate([(k_h[0] * d).astype(BF16), (k_h[1] * d).astype(BF16)], axis=0)
        a = jnp.where(lvl == lv, _dot_nt(ql, kl), a)

    q_in = (q * blk(0)).astype(BF16)
    st = jnp.concatenate([st_ref[0].astype(BF16), st_ref[1].astype(BF16)], axis=0)
    inter = _dot_nt(q_in, st)
    vlane = lax.broadcasted_iota(jnp.int32, (CHUNK, PAIR_V), 1)
    zero = jnp.zeros_like(v)
    v_blk = jnp.concatenate([jnp.where(vlane < GLA_DV, v, zero),
                             jnp.where(vlane < GLA_DV, zero, v)], axis=0)
    o = inter + _dot(a.astype(BF16), v_blk)

    d_out = blk(1)
    total = blk(0)[CHUNK - 1:CHUNK] if forward else blk(0)[0:1]
    for hd in range(2):
        k_out = (k_h[hd] * d_out).astype(BF16)
        upd = _dot_tn(v[:, hd * GLA_DV:(hd + 1) * GLA_DV], k_out)
        st_ref[hd] = st_ref[hd] * total + upd
    return o


def _gla_fast_step(jobs, fwd_refs, bwd_refs):
    refs = [fwd_refs if fw else bwd_refs for _, fw, _ in jobs]
    rows = [pl.ds(r0, CHUNK) for r0, _, _ in jobs]
    lane = lax.broadcasted_iota(jnp.int32, (CHUNK, PAIR_K), 1)
    head0 = lane < GLA_DK
    vlane = lax.broadcasted_iota(jnp.int32, (CHUNK, PAIR_V), 1) < GLA_DV

    la = [r[3][rw, :] for r, rw in zip(refs, rows)]
    la_hi = [x.astype(BF16) for x in la]
    la_lo = [(x - h.astype(F32)).astype(BF16) for x, h in zip(la, la_hi)]
    cum = [_dot(r[4][0:CHUNK, :], jnp.concatenate([h, l], axis=1))
           for r, h, l in zip(refs, la_hi, la_lo)]
    b = [x[:, :PAIR_K] + x[:, PAIR_K:] for x in cum]
    b_end = [x[CHUNK - 1:CHUNK] if fw else x[0:1] for x, (_, fw, _) in zip(b, jobs)]
    q = [r[0][rw, :] for r, rw in zip(refs, rows)]
    k = [r[1][rw, :] for r, rw in zip(refs, rows)]
    v = [r[2][rw, :] for r, rw in zip(refs, rows)]
    qd = [(x * jnp.exp(y)).astype(BF16) for x, y in zip(q, b)]
    kd = [x * jnp.exp(-y) for x, y in zip(k, b)]
    ko = [x * jnp.exp(e - y) for x, y, e in zip(k, b, b_end)]
    total = [jnp.exp(e) for e in b_end]
    kd_cat = [jnp.concatenate([jnp.where(head0, x, 0.0).astype(BF16),
                               jnp.where(head0, 0.0, x).astype(BF16)], axis=0) for x in kd]
    score = [_dot_nt(x, y) for x, y in zip(qd, kd_cat)]
    a = [jnp.where(r[5][...] <= N_LEVELS, s, 0.0).astype(BF16) for r, s in zip(refs, score)]
    v_blk = [jnp.concatenate([jnp.where(vlane, x, jnp.zeros_like(x)),
                              jnp.where(vlane, jnp.zeros_like(x), x)], axis=0) for x in v]
    intra = [_dot(x, y) for x, y in zip(a, v_blk)]
    ko_h = [(jnp.where(head0, x, 0.0).astype(BF16), jnp.where(head0, 0.0, x).astype(BF16)) for x in ko]
    upd = [[_dot_tn(x[:, hd * GLA_DV:(hd + 1) * GLA_DV], y[hd]) for hd in range(2)]
           for x, y in zip(v, ko_h)]

    outs = [None] * len(jobs)
    for scan in sorted({(fw, slot) for _, fw, slot in jobs}):
        st_ref = (fwd_refs if scan[0] else bwd_refs)[6]
        st = [st_ref[scan[1], 0], st_ref[scan[1], 1]]
        for j, (_, fw, slot) in enumerate(jobs):
            if (fw, slot) != scan:
                continue
            st_cat = jnp.concatenate([st[0].astype(BF16), st[1].astype(BF16)], axis=0)
            outs[j] = _dot_nt(qd[j], st_cat) + intra[j]
            st = [st[hd] * total[j] + upd[j][hd] for hd in range(2)]
        st_ref[scan[1], 0] = st[0]
        st_ref[scan[1], 1] = st[1]
    return outs


def _gla_kernel(*refs, seq_len, n_seq, has_state, emit_state, unroll, par):
    (q_ref, k_ref, v_ref, laf_ref, lab_ref, g_ref, gh_ref,
     gapf_ref, gapb_ref, lvlf_ref, lvlb_ref) = refs[:11]
    pos = 11
    if has_state:
        s0f_ref, s0b_ref = refs[pos:pos + 2]
        pos += 2
    o_ref = refs[pos]
    pos += 1
    if emit_state:
        sf_ref, sb_ref = refs[pos:pos + 2]
        pos += 2
    acc_ref, stf_ref, stb_ref = refs[pos:pos + 3]
    n_chunks = seq_len // CHUNK
    half_iters = n_chunks // (2 * unroll)
    fwd_refs = (q_ref, k_ref, v_ref, laf_ref, gapf_ref, lvlf_ref, stf_ref)
    bwd_refs = (q_ref, k_ref, v_ref, lab_ref, gapb_ref, lvlb_ref, stb_ref)

    def finish(o, rows):
        g = g_ref[rows, :].astype(F32)
        for hd in range(2):
            sl = slice(hd * GLA_DV, (hd + 1) * GLA_DV)
            oh = _rms(o[:, sl], gh_ref[:, sl]) * _silu(g[:, sl])
            o_ref[rows, sl] = oh.astype(o_ref.dtype)

    def seq_body(sg, carry, fast):
        base = sg * (par * seq_len)
        if has_state:
            stf_ref[0] = s0f_ref[...]
            stb_ref[0] = s0b_ref[...]
        else:
            stf_ref[...] = jnp.zeros_like(stf_ref)
            stb_ref[...] = jnp.zeros_like(stb_ref)

        def step(it, second_visit):
            jobs = []
            for slot in range(par):
                for u in range(unroll):
                    cf = it * unroll + u
                    for forward, ci in ((True, cf), (False, n_chunks - 1 - cf)):
                        r0 = base + slot * seq_len + ci * CHUNK
                        jobs.append((pl.multiple_of(r0, CHUNK), forward, slot))
            if fast:
                outs = _gla_fast_step(jobs, fwd_refs, bwd_refs)
            else:
                outs = [_gla_chunk(r0, fw, *(fwd_refs if fw else bwd_refs)[:6],
                                   (stf_ref if fw else stb_ref).at[slot]) for r0, fw, slot in jobs]
            for (r0, _, _), o in zip(jobs, outs):
                rows = pl.ds(r0, CHUNK)
                if second_visit:
                    finish(o + acc_ref[rows, :], rows)
                else:
                    acc_ref[rows, :] = o

        def first(i, c):
            step(i, False)
            return c

        def second(i, c):
            step(half_iters + i, True)
            return c

        lax.fori_loop(0, half_iters, first, 0)
        lax.fori_loop(0, half_iters, second, 0)
        if emit_state:
            for s_out, st_ref in ((sf_ref, stf_ref), (sb_ref, stb_ref)):
                for slot in range(par):
                    for hd in range(2):
                        s_out[sg * par + slot, hd] = st_ref[slot, hd].T[hd * GLA_DK:(hd + 1) * GLA_DK, :]
        return carry

    weakest = jnp.minimum(jnp.min(laf_ref[...]), jnp.min(lab_ref[...]))
    is_fast = weakest >= -FAST_STEP_LIMIT

    @pl.when(is_fast)
    def _():
        lax.fori_loop(0, n_seq // par, functools.partial(seq_body, fast=True), 0)

    @pl.when(jnp.logical_not(is_fast))
    def _():
        lax.fori_loop(0, n_seq // par, functools.partial(seq_body, fast=False), 0)


def _gla(layer, q, k, v, la, g, g_head, consts, seq_len, n_seq_total, n_seq, s0=None,
         emit_state=False):
    gapf, gapb, lvlf, lvlb = consts
    rows = seq_len * n_seq
    n_steps = n_seq_total // n_seq
    rspec = lambda w, off=0: pl.BlockSpec((rows, w), lambda i, j: (i, j + off))
    const = lambda a: pl.BlockSpec(a.shape, lambda i, j: (0,) * a.ndim)
    in_specs = [rspec(PAIR_K), rspec(PAIR_K), rspec(PAIR_V), rspec(PAIR_K), rspec(PAIR_K, HEAD_PAIRS),
                rspec(PAIR_V),
                pl.BlockSpec((None, 1, PAIR_V), lambda i, j: (layer, 0, j)),
                const(gapf), const(gapb), const(lvlf), const(lvlb)]
    args = [q, k, v, la, la, g, g_head, gapf, gapb, lvlf, lvlb]
    if s0 is not None:
        sspec = pl.BlockSpec((None, None, 2, GLA_DV, PAIR_K), lambda i, j: (i, layer, j, 0, 0))
        in_specs += [sspec, sspec]
        args += list(s0)
    out_specs = [rspec(PAIR_V)]
    out_shape = [jax.ShapeDtypeStruct((n_seq_total * seq_len, V_W), BF16)]
    if emit_state:
        st_spec = pl.BlockSpec((n_seq, 2, GLA_DK, GLA_DV), lambda i, j: (i, j, 0, 0))
        st_shape = jax.ShapeDtypeStruct((n_seq_total, GLA_HEADS, GLA_DK, GLA_DV), F32)
        out_specs += [st_spec, st_spec]
        out_shape += [st_shape, st_shape]
    unroll = min(SCAN_UNROLL, seq_len // (2 * CHUNK))
    par = 1 if s0 is not None else min(n_seq, SCAN_UNROLL // unroll)
    kern = functools.partial(_gla_kernel, seq_len=seq_len, n_seq=n_seq, has_state=s0 is not None,
                             emit_state=emit_state, unroll=unroll, par=par)
    return pl.pallas_call(
        kern,
        grid=(n_steps, HEAD_PAIRS),
        in_specs=in_specs,
        out_specs=out_specs,
        out_shape=out_shape,
        scratch_shapes=[pltpu.VMEM((rows, PAIR_V), F32), pltpu.VMEM((par, 2, GLA_DV, PAIR_K), F32),
                        pltpu.VMEM((par, 2, GLA_DV, PAIR_K), F32)],
        compiler_params=_params(2),
        name="gla_state" if emit_state else "gla",
    )(*args)


def _fft_tables(n_side):
    c, s = _dft_tables(n_side)
    l1 = np.block([[c, -s], [-s, -c]])
    l2 = np.concatenate([c, s], axis=1)
    idx = np.arange(n_side)
    ang = 2.0 * np.pi * (idx[:, None] * idx[None, :]) / (n_side * n_side)
    lanes = np.ones((1, 1, FNO_GROUP_W), np.float32)
    tc = np.cos(ang).astype(np.float32)[:, :, None] * lanes
    ts = np.sin(ang).astype(np.float32)[:, :, None] * lanes
    return l1, l2, tc, ts


def _fft_kernel(uc_ref, us_ref, l1_ref, l2_ref, tc_ref, ts_ref, o_ref, zr_ref, zi_ref, *, n_side):
    def stage1(n2, carry):
        col = pl.ds(n2, n_side, stride=n_side)
        x = jnp.concatenate([uc_ref[col, :].astype(BF16), us_ref[col, :].astype(BF16)], axis=0)
        z = _dot(l1_ref[...], x)
        zr, zi = z[:n_side], z[n_side:]
        tc, ts = tc_ref[n2], ts_ref[n2]
        out = pl.ds(pl.multiple_of(n2 * n_side, n_side), n_side)
        zr_ref[out, :] = zr * tc + zi * ts
        zi_ref[out, :] = zi * tc - zr * ts
        return carry

    def stage2(k1, carry):
        col = pl.ds(k1, n_side, stride=n_side)
        z = jnp.concatenate([zr_ref[col, :].astype(BF16), zi_ref[col, :].astype(BF16)], axis=0)
        o_ref[col, :] = _dot(l2_ref[...], z)
        return carry

    lax.fori_loop(0, n_side, stage1, 0, unroll=FFT_UNROLL)
    lax.fori_loop(0, n_side, stage2, 0, unroll=FFT_UNROLL)


def _fnet_fft(uc, us, seq_len, n_seq_total):
    n_side = GRID_W
    l1, l2, tc, ts = _fft_tables(n_side)
    as_bf16 = lambda a: jnp.asarray(a, F32).astype(BF16)
    blk = pl.BlockSpec((seq_len, FNO_GROUP_W), lambda b, g: (b, g))
    const = lambda a: pl.BlockSpec(a.shape, lambda b, g: (0,) * a.ndim)
    return pl.pallas_call(
        functools.partial(_fft_kernel, n_side=n_side),
        grid=(n_seq_total, FNO_GROUPS),
        in_specs=[blk, blk, const(l1), const(l2), const(tc), const(ts)],
        out_specs=blk,
        out_shape=jax.ShapeDtypeStruct((n_seq_total * seq_len, FNO_W), F32),
        scratch_shapes=[pltpu.VMEM((seq_len, FNO_GROUP_W), F32), pltpu.VMEM((seq_len, FNO_GROUP_W), F32)],
        compiler_params=_params(2),
        name="fnet_fft",
    )(uc, us, as_bf16(l1), as_bf16(l2), jnp.asarray(tc), jnp.asarray(ts))


def _post_kernel(*refs, dft_len):
    if dft_len:
        x_ref, o_ref, uc_ref, us_ref, cos_ref, nsin_ref = refs[:6]
        refs = refs[6:]
        yf = jnp.concatenate(
            [(_dot(cos_ref[...], uc_ref[r0:r0 + dft_len, :]) +
              _dot(nsin_ref[...], us_ref[r0:r0 + dft_len, :])).astype(BF16)
             for r0 in range(0, ROW_TILE, dft_len)], axis=0)
    else:
        x_ref, o_ref, yf_ref = refs[:3]
        refs = refs[3:]
        yf = yf_ref[...].astype(BF16)
    mod_ref, gpm_ref, gpre_ref, gpost_ref, wout_ref, w1_ref, w2_ref, out_ref = refs
    gt_m = mod_ref[:, 2 * D_MODEL:3 * D_MODEL]
    sh_f = mod_ref[:, 3 * D_MODEL:4 * D_MODEL]
    sc_f = mod_ref[:, 4 * D_MODEL:5 * D_MODEL]
    gt_f = mod_ref[:, 5 * D_MODEL:6 * D_MODEL]
    y = _dot(o_ref[...], wout_ref[0:V_W, :]) + _dot(yf, wout_ref[V_W:, :])
    x = x_ref[...] + gt_m * _rms(y, gpm_ref[...])
    h = (_rms(x, gpre_ref[...]) * (1.0 + sc_f) + sh_f).astype(BF16)
    f = jnp.zeros((ROW_TILE, D_MODEL), F32)
    n_split = 4
    wf = D_FF // n_split
    for j in range(n_split):
        a = jnp.maximum(_dot(h, w1_ref[:, j * wf:(j + 1) * wf]), 0.0)
        f = f + _dot((a * a).astype(BF16), w2_ref[j * wf:(j + 1) * wf, :])
    out_ref[...] = x + gt_f * _rms(f, gpost_ref[...])


def _post(layer, x, o, fno, mod, mod_sel, g_post_mix, g_pre_mlp, g_post_mlp, w_out, w_fc1, w_fc2):
    t = x.shape[0]
    row = lambda w: pl.BlockSpec((ROW_TILE, w), lambda i: (i, 0))
    lay = lambda *shape: pl.BlockSpec((None,) + shape, lambda i: (layer,) + (0,) * len(shape),
                                      pipeline_mode=pl.Buffered(1))
    dft_len = fno[2].shape[0] if len(fno) == 4 else 0
    fno_specs = [row(FNO_W)] * min(len(fno), 2) + [
        pl.BlockSpec((dft_len, dft_len), lambda i: (0, 0))] * (len(fno) - min(len(fno), 2))
    return pl.pallas_call(
        functools.partial(_post_kernel, dft_len=dft_len),
        grid=(t // ROW_TILE,),
        in_specs=[row(D_MODEL), row(V_W), *fno_specs,
                  pl.BlockSpec((None, None, 1, 6 * D_MODEL), lambda i: (layer, mod_sel(i), 0, 0)),
                  lay(1, D_MODEL), lay(1, D_MODEL), lay(1, D_MODEL),
                  lay(V_W + FNO_W, D_MODEL), lay(D_MODEL, D_FF), lay(D_FF, D_MODEL)],
        out_specs=row(D_MODEL),
        out_shape=jax.ShapeDtypeStruct((t, D_MODEL), F32),
        compiler_params=_params(1),
        name="post",
    )(x, o, *fno, mod, g_post_mix, g_pre_mlp, g_post_mlp, w_out, w_fc1, w_fc2)


def _state_to_kernel(s):
    st = jnp.swapaxes(s, -1, -2)
    z = jnp.zeros_like(st)
    even = jnp.concatenate([st, z], axis=-1)
    odd = jnp.concatenate([z, st], axis=-1)
    is_even = (jnp.arange(GLA_HEADS) % 2 == 0)[None, None, :, None, None]
    return jnp.where(is_even, even, odd)


def kernel(x_prompt, x_sample, state_gla_fwd, state_gla_bwd, c, c_ctx, w_in, w_out, w_a2, b_a2,
           g_head, w_ada, b_ada, g_pre_mix, g_post_mix, g_pre_mlp, g_post_mlp, w_fc1, w_fc2):
    batch, seq, _ = x_prompt.shape
    dec_batch, dec_seq, _ = x_sample.shape
    t_ctx = batch * seq
    t_lat = dec_batch * dec_seq
    assert dec_seq == GRID_W * GRID_W and seq % CHUNK == 0 and ROW_TILE % seq == 0
    lat_tiles = dec_seq // ROW_TILE
    mod_sel = (lambda i: 0, lambda i: 1 + i // lat_tiles)

    za = jnp.zeros((DEPTH, LR_RANK, QK_W), F32)
    wa2 = jnp.concatenate([jnp.concatenate([w_a2[:, 0], za], axis=-1),
                           jnp.concatenate([za, w_a2[:, 1]], axis=-1)], axis=1).astype(BF16)
    ba2 = b_a2.reshape(DEPTH, 1, 2 * QK_W)
    w_out_b = w_out.astype(BF16)
    w_fc1_b = w_fc1.astype(BF16)
    w_fc2_b = w_fc2.astype(BF16)
    vec = lambda a: a.reshape(DEPTH, 1, -1)
    g_head_r = g_head.reshape(DEPTH, 1, V_W)

    gapf, gapb, lvlf, lvlb = _gap_matrices()
    as_bf16 = lambda a: jnp.asarray(a, F32).astype(BF16)
    consts = (as_bf16(gapf), as_bf16(gapb), jnp.asarray(lvlf), jnp.asarray(lvlb))
    cw, sw = _dft_tables(FNO_GROUP_W)
    cs_w = as_bf16(np.concatenate([cw, sw], axis=1))
    cn, sn = _dft_tables(seq)
    cos_ctx, nsin_ctx = as_bf16(cn), as_bf16(-sn)

    cvec = jnp.zeros((MOD_ROWS, D_MODEL), F32).at[0].set(c_ctx).at[1:1 + dec_batch].set(c)
    mod = _modulation(cvec, w_ada, b_ada)[:, :1 + dec_batch, None, :]

    xs = [x_prompt.reshape(t_ctx, D_MODEL),
          _embed(x_sample).reshape(t_lat, D_MODEL)]
    s0 = (_state_to_kernel(state_gla_fwd), _state_to_kernel(state_gla_bwd))

    st_f, st_b = [], []
    for l in range(DEPTH):
        pre = [_premix(l, x, mod, sel, vec(g_pre_mix), w_in, wa2, ba2, cs_w, dt)
               for x, sel, dt in zip(xs, mod_sel, (BF16, F32))]
        q, k, v, g, la, uc, us = pre[0]
        o_ctx, sf, sb = _gla(l, q, k, v, la, g, g_head_r, consts, seq, batch, 4, emit_state=True)
        fno_ctx = (uc, us, cos_ctx, nsin_ctx)
        q, k, v, g, la, uc, us = pre[1]
        (o_lat,) = _gla(l, q, k, v, la, g, g_head_r, consts, dec_seq, dec_batch, 1, s0=s0)
        fno_lat = (_fnet_fft(uc, us, dec_seq, dec_batch),)
        xs = [_post(l, x, o, fno, mod, sel, vec(g_post_mix), vec(g_pre_mlp), vec(g_post_mlp),
                    w_out_b, w_fc1_b, w_fc2_b)
              for x, o, fno, sel in zip(xs, (o_ctx, o_lat), (fno_ctx, fno_lat), mod_sel)]
        st_f.append(sf)
        st_b.append(sb)

    y_prompt = xs[0].reshape(batch, seq, D_MODEL)
    y_sample = xs[1].reshape(dec_batch, dec_seq, D_MODEL)
    return (y_prompt, y_sample, jnp.stack(st_f, axis=1), jnp.stack(st_b, axis=1))
```

```python
import functools

import numpy as np
import jax
import jax.numpy as jnp
from jax import lax
from jax.experimental import pallas as pl
from jax.experimental.pallas import tpu as pltpu

D_MODEL = 1024
DEPTH = 4
GRID_W = 64
GLA_HEADS = 4
GLA_DK = 64
GLA_DV = 128
CHUNK = 64
GATE_TAU = 16.0
LR_RANK = 16
FNO_GROUPS = 4
FNO_GROUP_W = 128
D_FF = 4 * D_MODEL
EPS = 1e-6
QK_W = GLA_HEADS * GLA_DK
V_W = GLA_HEADS * GLA_DV
FNO_W = FNO_GROUPS * FNO_GROUP_W
MAIN_W = 2 * QK_W + 2 * V_W
LR_W = 2 * LR_RANK
D_IN = MAIN_W + LR_W + FNO_W
HEAD_PAIRS = GLA_HEADS // 2
PAIR_K = 2 * GLA_DK
PAIR_V = 2 * GLA_DV
N_LEVELS = 6
N_GAPS = N_LEVELS + 2
SCAN_UNROLL = 8
FAST_STEP_LIMIT = 1.0

VMEM_LIMIT = 48 * 1024 * 1024
ROW_TILE = 512
MOD_ROWS = 8
FFT_UNROLL = 8

BF16 = jnp.bfloat16
F32 = jnp.float32


def _gap_matrices():
    c = CHUNK
    m_all = np.zeros((N_GAPS, c, c), np.float32)
    for p in range(c):
        m_all[0, p, :p + 1] = 1.0
        m_all[1, p, p + 1:] = 1.0
    for lv in range(N_LEVELS):
        m = c >> (lv + 1)
        for p in range(c):
            base = (p // (2 * m)) * 2 * m
            ref = base + m - 1
            if p > ref:
                m_all[2 + lv, p, ref + 1:p + 1] = 1.0
            else:
                m_all[2 + lv, p, p + 1:ref + 1] = 1.0
    fwd = m_all.reshape(N_GAPS * c, c)
    bwd = m_all[:, ::-1, ::-1].reshape(N_GAPS * c, c)
    lvl = np.full((c, c), N_LEVELS + 1, np.int32)
    for t in range(c):
        for s in range(t + 1):
            if s == t:
                lvl[t, s] = 0
            else:
                top = (t ^ s).bit_length() - 1
                lvl[t, s] = N_LEVELS - top
    lvl_f = np.concatenate([lvl, lvl], axis=1)
    lvl_b = np.concatenate([lvl[::-1, ::-1], lvl[::-1, ::-1]], axis=1)
    return fwd, bwd, lvl_f, lvl_b


def _dft_tables(n):
    idx = np.arange(n)
    ang = 2.0 * np.pi * ((idx[:, None] * idx[None, :]) % n) / n
    s = 1.0 / np.sqrt(n)
    return (np.cos(ang) * s).astype(np.float32), (np.sin(ang) * s).astype(np.float32)


def _silu(x):
    return x * (1.0 / (1.0 + jnp.exp(-x)))


def _rms(x, g):
    return x * lax.rsqrt(jnp.mean(x * x, axis=-1, keepdims=True) + EPS) * g


def _dot(a, b):
    return jnp.dot(a, b, preferred_element_type=F32)


def _dot_nt(a, b):
    return lax.dot_general(a, b, (((1,), (1,)), ((), ())), preferred_element_type=F32)


def _dot_tn(a, b):
    return lax.dot_general(a, b, (((0,), (0,)), ((), ())), preferred_element_type=F32)


def _params(n_grid):
    return pltpu.CompilerParams(dimension_semantics=("arbitrary",) * n_grid,
                                vmem_limit_bytes=VMEM_LIMIT)


def _mod_kernel(c_ref, w_ref, b_ref, o_ref):
    s = _silu(c_ref[...])
    o_ref[...] = jnp.dot(s, w_ref[...], preferred_element_type=F32,
                         precision=lax.Precision.HIGHEST) + b_ref[...]


def _modulation(cvec, w_ada, b_ada):
    tn = 3072
    return pl.pallas_call(
        _mod_kernel,
        grid=(DEPTH, 6 * D_MODEL // tn),
        in_specs=[pl.BlockSpec((MOD_ROWS, D_MODEL), lambda l, j: (0, 0)),
                  pl.BlockSpec((None, D_MODEL, tn), lambda l, j: (l, 0, j)),
                  pl.BlockSpec((None, 1, tn), lambda l, j: (l, 0, j))],
        out_specs=pl.BlockSpec((None, MOD_ROWS, tn), lambda l, j: (l, 0, j)),
        out_shape=jax.ShapeDtypeStruct((DEPTH, MOD_ROWS, 6 * D_MODEL), F32),
        compiler_params=_params(2),
        name="modulation",
    )(cvec, w_ada, b_ada.reshape(DEPTH, 1, 6 * D_MODEL))


def _pos_table():
    nf = D_MODEL // 4
    omega = 1.0 / (10000.0 ** (np.arange(nf, dtype=np.float64) / nf))
    idx = np.arange(GRID_W, dtype=np.float64)[:, None]
    return np.concatenate([np.sin(idx * omega), np.cos(idx * omega)], axis=1).astype(np.float32)


def _pos_tile(tab_ref, tile):
    grid_rows = ROW_TILE // GRID_W
    r0 = pl.multiple_of(tile * grid_rows, grid_rows)
    t = tab_ref[pl.ds(r0, grid_rows), :]
    by_row = jnp.concatenate([jnp.broadcast_to(t[i:i + 1], (GRID_W, D_MODEL // 2))
                              for i in range(grid_rows)], axis=0)
    by_col = jnp.concatenate([tab_ref[...]] * grid_rows, axis=0)
    return jnp.concatenate([by_row, by_col], axis=1)


def _premix_kernel(*refs, pos_tiles):
    x_ref, mod_ref, gpre_ref, w_ref, wa2_ref, ba2_ref, cs_ref = refs[:7]
    refs = refs[7:]
    if pos_tiles:
        tab_ref, refs = refs[0], refs[1:]
    q_ref, k_ref, v_ref, g_ref, la_ref, uc_ref, us_ref, wbf_ref = refs

    @pl.when(pl.program_id(0) == 0)
    def _():
        wbf_ref[...] = w_ref[...].astype(BF16)

    x = x_ref[...]
    if pos_tiles:
        x = x + _pos_tile(tab_ref, pl.program_id(0) % pos_tiles)
    sh = mod_ref[:, 0:D_MODEL]
    sc = mod_ref[:, D_MODEL:2 * D_MODEL]
    h = (_rms(x, gpre_ref[...]) * (1.0 + sc) + sh).astype(BF16)
    u = _dot_nt(h, wbf_ref[MAIN_W + LR_W:, :]).astype(BF16)
    lr = _dot_nt(h, wbf_ref[MAIN_W:MAIN_W + LR_W, :]).astype(BF16)
    q_ref[...] = _dot_nt(h, wbf_ref[0:QK_W, :]) * (GLA_DK ** -0.5)
    k_ref[...] = _dot_nt(h, wbf_ref[QK_W:2 * QK_W, :])
    v_ref[...] = _dot_nt(h, wbf_ref[2 * QK_W:2 * QK_W + V_W, :]).astype(BF16)
    g_ref[...] = _dot_nt(h, wbf_ref[2 * QK_W + V_W:MAIN_W, :]).astype(BF16)
    logit = _dot(lr, wa2_ref[...]) + ba2_ref[...]
    softplus_neg = jnp.maximum(-logit, 0.0) + jnp.log(1.0 + jnp.exp(-jnp.abs(logit)))
    la_ref[...] = softplus_neg * (-1.0 / GATE_TAU)
    for grp in range(FNO_GROUPS):
        sl = slice(grp * FNO_GROUP_W, (grp + 1) * FNO_GROUP_W)
        ucs = _dot(u[:, sl], cs_ref[...])
        uc_ref[:, sl] = ucs[:, :FNO_GROUP_W].astype(uc_ref.dtype)
        us_ref[:, sl] = ucs[:, FNO_GROUP_W:].astype(us_ref.dtype)


def _premix(layer, x, mod, mod_sel, g_pre, w_t, wa2, ba2, cs_w, fno_dtype, pos=None):
    t = x.shape[0]
    pos_args, pos_specs, pos_tiles = [], [], 0
    if pos is not None:
        pos_args, pos_tiles = [pos[0]], pos[1]
        pos_specs = [pl.BlockSpec(pos[0].shape, lambda i: (0, 0))]
    row = lambda w: pl.BlockSpec((ROW_TILE, w), lambda i: (i, 0))
    lay = lambda *shape: pl.BlockSpec((None,) + shape, lambda i: (layer,) + (0,) * len(shape))
    out = lambda w, dt: jax.ShapeDtypeStruct((t, w), dt)
    return pl.pallas_call(
        functools.partial(_premix_kernel, pos_tiles=pos_tiles),
        grid=(t // ROW_TILE,),
        in_specs=[row(D_MODEL),
                  pl.BlockSpec((None, None, 1, 6 * D_MODEL), lambda i: (layer, mod_sel(i), 0, 0)),
                  lay(1, D_MODEL),
                  pl.BlockSpec((None, D_IN, D_MODEL), lambda i: (layer, 0, 0), pipeline_mode=pl.Buffered(1)),
                  lay(LR_W, 2 * QK_W), lay(1, 2 * QK_W),
                  pl.BlockSpec((FNO_GROUP_W, 2 * FNO_GROUP_W), lambda i: (0, 0)), *pos_specs],
        out_specs=[row(QK_W), row(QK_W), row(V_W), row(V_W), row(2 * QK_W), row(FNO_W), row(FNO_W)],
        out_shape=[out(QK_W, F32), out(QK_W, F32), out(V_W, BF16), out(V_W, BF16),
                   out(2 * QK_W, F32), out(FNO_W, fno_dtype), out(FNO_W, fno_dtype)],
        scratch_shapes=[pltpu.VMEM((D_IN, D_MODEL), BF16)],
        compiler_params=_params(1),
        name="premix",
    )(x, mod, g_pre, w_t, wa2, ba2, cs_w, *pos_args)


def _gla_chunk(r0, forward, q_ref, k_ref, v_ref, la_ref, gap_ref, lvl_ref, st_ref):
    rows = pl.ds(r0, CHUNK)
    lane = lax.broadcasted_iota(jnp.int32, (CHUNK, PAIR_K), 1)
    head0 = lane < GLA_DK
    la = la_ref[rows, :]
    la_hi = la.astype(BF16)
    la_lo = (la - la_hi.astype(F32)).astype(BF16)
    gaps = _dot(gap_ref[...], jnp.concatenate([la_hi, la_lo], axis=1))
    decay = jnp.exp(gaps[:, :PAIR_K] + gaps[:, PAIR_K:])
    blk = lambda j: decay[j * CHUNK:(j + 1) * CHUNK]
    q = q_ref[rows, :]
    k = k_ref[rows, :]
    k_h = (jnp.where(head0, k, 0.0), jnp.where(head0, 0.0, k))
    v = v_ref[rows, :]
    lvl = lvl_ref[...]

    a = jnp.zeros((CHUNK, 2 * CHUNK), F32)
    for lv in range(N_LEVELS + 1):
        if lv == 0:
            ql = q.astype(BF16)
            kl = jnp.concatenate([k_h[0].astype(BF16), k_h[1].astype(BF16)], axis=0)
        else:
            d = blk(1 + lv)
            ql = (q * d).astype(BF16)
            kl = jnp.concatenate([(k_h[0] * d).astype(BF16), (k_h[1] * d).astype(BF16)], axis=0)
        a = jnp.where(lvl == lv, _dot_nt(ql, kl), a)

    q_in = (q * blk(0)).astype(BF16)
    st = jnp.concatenate([st_ref[0].astype(BF16), st_ref[1].astype(BF16)], axis=0)
    inter = _dot_nt(q_in, st)
    vlane = lax.broadcasted_iota(jnp.int32, (CHUNK, PAIR_V), 1)
    zero = jnp.zeros_like(v)
    v_blk = jnp.concatenate([jnp.where(vlane < GLA_DV, v, zero),
                             jnp.where(vlane < GLA_DV, zero, v)], axis=0)
    o = inter + _dot(a.astype(BF16), v_blk)

    d_out = blk(1)
    total = blk(0)[CHUNK - 1:CHUNK] if forward else blk(0)[0:1]
    for hd in range(2):
        k_out = (k_h[hd] * d_out).astype(BF16)
        upd = _dot_tn(v[:, hd * GLA_DV:(hd + 1) * GLA_DV], k_out)
        st_ref[hd] = st_ref[hd] * total + upd
    return o


def _gla_fast_step(jobs, fwd_refs, bwd_refs):
    refs = [fwd_refs if fw else bwd_refs for _, fw, _ in jobs]
    rows = [pl.ds(r0, CHUNK) for r0, _, _ in jobs]
    lane = lax.broadcasted_iota(jnp.int32, (CHUNK, PAIR_K), 1)
    head0 = lane < GLA_DK
    vlane = lax.broadcasted_iota(jnp.int32, (CHUNK, PAIR_V), 1) < GLA_DV

    la = [r[3][rw, :] for r, rw in zip(refs, rows)]
    la_hi = [x.astype(BF16) for x in la]
    la_lo = [(x - h.astype(F32)).astype(BF16) for x, h in zip(la, la_hi)]
    cum = [_dot(r[4][0:CHUNK, :], jnp.concatenate([h, l], axis=1))
           for r, h, l in zip(refs, la_hi, la_lo)]
    b = [x[:, :PAIR_K] + x[:, PAIR_K:] for x in cum]
    b_end = [x[CHUNK - 1:CHUNK] if fw else x[0:1] for x, (_, fw, _) in zip(b, jobs)]
    q = [r[0][rw, :] for r, rw in zip(refs, rows)]
    k = [r[1][rw, :] for r, rw in zip(refs, rows)]
    v = [r[2][rw, :] for r, rw in zip(refs, rows)]
    qd = [(x * jnp.exp(y)).astype(BF16) for x, y in zip(q, b)]
    kd = [x * jnp.exp(-y) for x, y in zip(k, b)]
    ko = [x * jnp.exp(e - y) for x, y, e in zip(k, b, b_end)]
    total = [jnp.exp(e) for e in b_end]
    kd_cat = [jnp.concatenate([jnp.where(head0, x, 0.0).astype(BF16),
                               jnp.where(head0, 0.0, x).astype(BF16)], axis=0) for x in kd]
    score = [_dot_nt(x, y) for x, y in zip(qd, kd_cat)]
    a = [jnp.where(r[5][...] <= N_LEVELS, s, 0.0).astype(BF16) for r, s in zip(refs, score)]
    v_blk = [jnp.concatenate([jnp.where(vlane, x, jnp.zeros_like(x)),
                              jnp.where(vlane, jnp.zeros_like(x), x)], axis=0) for x in v]
    intra = [_dot(x, y) for x, y in zip(a, v_blk)]
    ko_h = [(jnp.where(head0, x, 0.0).astype(BF16), jnp.where(head0, 0.0, x).astype(BF16)) for x in ko]
    upd = [[_dot_tn(x[:, hd * GLA_DV:(hd + 1) * GLA_DV], y[hd]) for hd in range(2)]
           for x, y in zip(v, ko_h)]

    outs = [None] * len(jobs)
    for scan in sorted({(fw, slot) for _, fw, slot in jobs}):
        st_ref = (fwd_refs if scan[0] else bwd_refs)[6]
        st = [st_ref[scan[1], 0], st_ref[scan[1], 1]]
        for j, (_, fw, slot) in enumerate(jobs):
            if (fw, slot) != scan:
                continue
            st_cat = jnp.concatenate([st[0].astype(BF16), st[1].astype(BF16)], axis=0)
            outs[j] = _dot_nt(qd[j], st_cat) + intra[j]
            st = [st[hd] * total[j] + upd[j][hd] for hd in range(2)]
        st_ref[scan[1], 0] = st[0]
        st_ref[scan[1], 1] = st[1]
    return outs


def _gla_kernel(*refs, seq_len, n_seq, has_state, emit_state, unroll, par, layer, all_layers, n_alias):
    (q_ref, k_ref, v_ref, laf_ref, lab_ref, g_ref, gh_ref,
     gapf_ref, gapb_ref, lvlf_ref, lvlb_ref) = refs[:11]
    pos = 11
    if has_state:
        s0f_ref, s0b_ref = refs[pos:pos + 2]
        pos += 2
    pos += n_alias
    o_ref = refs[pos]
    pos += 1
    if emit_state:
        sf_ref, sb_ref = refs[pos:pos + 2]
        pos += 2
        if all_layers:
            sf_ref[...] = jnp.zeros_like(sf_ref)
            sb_ref[...] = jnp.zeros_like(sb_ref)
    acc_ref, stf_ref, stb_ref = refs[pos:pos + 3]
    n_chunks = seq_len // CHUNK
    half_iters = n_chunks // (2 * unroll)
    fwd_refs = (q_ref, k_ref, v_ref, laf_ref, gapf_ref, lvlf_ref, stf_ref)
    bwd_refs = (q_ref, k_ref, v_ref, lab_ref, gapb_ref, lvlb_ref, stb_ref)

    def finish(o, rows):
        g = g_ref[rows, :].astype(F32)
        for hd in range(2):
            sl = slice(hd * GLA_DV, (hd + 1) * GLA_DV)
            oh = _rms(o[:, sl], gh_ref[:, sl]) * _silu(g[:, sl])
            o_ref[rows, sl] = oh.astype(o_ref.dtype)

    def seq_body(sg, carry, fast):
        base = sg * (par * seq_len)
        if has_state:
            stf_ref[0] = s0f_ref[...]
            stb_ref[0] = s0b_ref[...]
        else:
            stf_ref[...] = jnp.zeros_like(stf_ref)
            stb_ref[...] = jnp.zeros_like(stb_ref)

        def step(it, second_visit):
            jobs = []
            for slot in range(par):
                for u in range(unroll):
                    cf = it * unroll + u
                    for forward, ci in ((True, cf), (False, n_chunks - 1 - cf)):
                        r0 = base + slot * seq_len + ci * CHUNK
                        jobs.append((pl.multiple_of(r0, CHUNK), forward, slot))
            if fast:
                outs = _gla_fast_step(jobs, fwd_refs, bwd_refs)
            else:
                outs = [_gla_chunk(r0, fw, *(fwd_refs if fw else bwd_refs)[:6],
                                   (stf_ref if fw else stb_ref).at[slot]) for r0, fw, slot in jobs]
            for (r0, _, _), o in zip(jobs, outs):
                rows = pl.ds(r0, CHUNK)
                if second_visit:
                    finish(o + acc_ref[rows, :], rows)
                else:
                    acc_ref[rows, :] = o

        def first(i, c):
            step(i, False)
            return c

        def second(i, c):
            step(half_iters + i, True)
            return c

        lax.fori_loop(0, half_iters, first, 0)
        lax.fori_loop(0, half_iters, second, 0)
        if emit_state:
            for s_out, st_ref in ((sf_ref, stf_ref), (sb_ref, stb_ref)):
                for slot in range(par):
                    for hd in range(2):
                        s_fin = st_ref[slot, hd].T[hd * GLA_DK:(hd + 1) * GLA_DK, :]
                        if all_layers:
                            s_out[sg * par + slot, layer, hd] = s_fin
                        else:
                            s_out[sg * par + slot, hd] = s_fin
        return carry

    weakest = jnp.minimum(jnp.min(laf_ref[...]), jnp.min(lab_ref[...]))
    is_fast = weakest >= -FAST_STEP_LIMIT

    @pl.when(is_fast)
    def _():
        lax.fori_loop(0, n_seq // par, functools.partial(seq_body, fast=True), 0)

    @pl.when(jnp.logical_not(is_fast))
    def _():
        lax.fori_loop(0, n_seq // par, functools.partial(seq_body, fast=False), 0)


def _gla(layer, q, k, v, la, g, g_head, consts, seq_len, n_seq_total, n_seq, s0=None,
         emit_state=False, st_prev=None):
    gapf, gapb, lvlf, lvlb = consts
    rows = seq_len * n_seq
    n_steps = n_seq_total // n_seq
    rspec = lambda w, off=0: pl.BlockSpec((rows, w), lambda i, j: (i, j + off))
    const = lambda a: pl.BlockSpec(a.shape, lambda i, j: (0,) * a.ndim)
    in_specs = [rspec(PAIR_K), rspec(PAIR_K), rspec(PAIR_V), rspec(PAIR_K), rspec(PAIR_K, HEAD_PAIRS),
                rspec(PAIR_V),
                pl.BlockSpec((None, 1, PAIR_V), lambda i, j: (layer, 0, j)),
                const(gapf), const(gapb), const(lvlf), const(lvlb)]
    args = [q, k, v, la, la, g, g_head, gapf, gapb, lvlf, lvlb]
    if s0 is not None:
        sspec = pl.BlockSpec((None, None, 2, GLA_DV, PAIR_K), lambda i, j: (i, layer, j, 0, 0))
        in_specs += [sspec, sspec]
        args += list(s0)
    out_specs = [rspec(PAIR_V)]
    out_shape = [jax.ShapeDtypeStruct((n_seq_total * seq_len, V_W), BF16)]
    aliases = {}
    if emit_state:
        if st_prev is None:
            st_spec = pl.BlockSpec((n_seq, DEPTH, 2, GLA_DK, GLA_DV), lambda i, j: (i, 0, j, 0, 0))
        else:
            st_spec = pl.BlockSpec((n_seq, None, 2, GLA_DK, GLA_DV), lambda i, j: (i, layer, j, 0, 0))
            for n, a in enumerate(st_prev):
                aliases[len(args)] = 1 + n
                in_specs.append(pl.BlockSpec(memory_space=pl.ANY))
                args.append(a)
        st_shape = jax.ShapeDtypeStruct((n_seq_total, DEPTH, GLA_HEADS, GLA_DK, GLA_DV), F32)
        out_specs += [st_spec, st_spec]
        out_shape += [st_shape, st_shape]
    unroll = min(SCAN_UNROLL, seq_len // (2 * CHUNK))
    par = 1 if s0 is not None else min(n_seq, SCAN_UNROLL // unroll)
    kern = functools.partial(_gla_kernel, seq_len=seq_len, n_seq=n_seq, has_state=s0 is not None,
                             emit_state=emit_state, unroll=unroll, par=par, layer=layer,
                             all_layers=emit_state and st_prev is None, n_alias=len(aliases))
    return pl.pallas_call(
        kern,
        grid=(n_steps, HEAD_PAIRS),
        in_specs=in_specs,
        out_specs=out_specs,
        out_shape=out_shape,
        input_output_aliases=aliases,
        scratch_shapes=[pltpu.VMEM((rows, PAIR_V), F32), pltpu.VMEM((par, 2, GLA_DV, PAIR_K), F32),
                        pltpu.VMEM((par, 2, GLA_DV, PAIR_K), F32)],
        compiler_params=_params(2),
        name="gla_state" if emit_state else "gla",
    )(*args)


def _fft_tables(n_side):
    c, s = _dft_tables(n_side)
    l1 = np.block([[c, -s], [-s, -c]])
    l2 = np.concatenate([c, s], axis=1)
    idx = np.arange(n_side)
    ang = 2.0 * np.pi * (idx[:, None] * idx[None, :]) / (n_side * n_side)
    lanes = np.ones((1, 1, FNO_GROUP_W), np.float32)
    tc = np.cos(ang).astype(np.float32)[:, :, None] * lanes
    ts = np.sin(ang).astype(np.float32)[:, :, None] * lanes
    return l1, l2, tc, ts


def _fft_kernel(uc_ref, us_ref, l1_ref, l2_ref, tc_ref, ts_ref, o_ref, zr_ref, zi_ref, *, n_side):
    def stage1(n2, carry):
        col = pl.ds(n2, n_side, stride=n_side)
        x = jnp.concatenate([uc_ref[col, :].astype(BF16), us_ref[col, :].astype(BF16)], axis=0)
        z = _dot(l1_ref[...], x)
        zr, zi = z[:n_side], z[n_side:]
        tc, ts = tc_ref[n2], ts_ref[n2]
        out = pl.ds(pl.multiple_of(n2 * n_side, n_side), n_side)
        zr_ref[out, :] = zr * tc + zi * ts
        zi_ref[out, :] = zi * tc - zr * ts
        return carry

    def stage2(k1, carry):
        col = pl.ds(k1, n_side, stride=n_side)
        z = jnp.concatenate([zr_ref[col, :].astype(BF16), zi_ref[col, :].astype(BF16)], axis=0)
        o_ref[col, :] = _dot(l2_ref[...], z)
        return carry

    lax.fori_loop(0, n_side, stage1, 0, unroll=FFT_UNROLL)
    lax.fori_loop(0, n_side, stage2, 0, unroll=FFT_UNROLL)


def _fnet_fft(uc, us, seq_len, n_seq_total):
    n_side = GRID_W
    l1, l2, tc, ts = _fft_tables(n_side)
    as_bf16 = lambda a: jnp.asarray(a, F32).astype(BF16)
    blk = pl.BlockSpec((seq_len, FNO_GROUP_W), lambda b, g: (b, g))
    const = lambda a: pl.BlockSpec(a.shape, lambda b, g: (0,) * a.ndim)
    return pl.pallas_call(
        functools.partial(_fft_kernel, n_side=n_side),
        grid=(n_seq_total, FNO_GROUPS),
        in_specs=[blk, blk, const(l1), const(l2), const(tc), const(ts)],
        out_specs=blk,
        out_shape=jax.ShapeDtypeStruct((n_seq_total * seq_len, FNO_W), F32),
        scratch_shapes=[pltpu.VMEM((seq_len, FNO_GROUP_W), F32), pltpu.VMEM((seq_len, FNO_GROUP_W), F32)],
        compiler_params=_params(2),
        name="fnet_fft",
    )(uc, us, as_bf16(l1), as_bf16(l2), jnp.asarray(tc), jnp.asarray(ts))


def _post_kernel(*refs, dft_len, pos_tiles):
    if dft_len:
        x_ref, o_ref, uc_ref, us_ref, cos_ref, nsin_ref = refs[:6]
        refs = refs[6:]
        yf = jnp.concatenate(
            [(_dot(cos_ref[...], uc_ref[r0:r0 + dft_len, :]) +
              _dot(nsin_ref[...], us_ref[r0:r0 + dft_len, :])).astype(BF16)
             for r0 in range(0, ROW_TILE, dft_len)], axis=0)
    else:
        x_ref, o_ref, yf_ref = refs[:3]
        refs = refs[3:]
        yf = yf_ref[...].astype(BF16)
    x_in = x_ref[...]
    if pos_tiles:
        x_in = x_in + _pos_tile(refs[0], pl.program_id(0) % pos_tiles)
        refs = refs[1:]
    mod_ref, gpm_ref, gpre_ref, gpost_ref, wout_ref, w1_ref, w2_ref, out_ref = refs
    gt_m = mod_ref[:, 2 * D_MODEL:3 * D_MODEL]
    sh_f = mod_ref[:, 3 * D_MODEL:4 * D_MODEL]
    sc_f = mod_ref[:, 4 * D_MODEL:5 * D_MODEL]
    gt_f = mod_ref[:, 5 * D_MODEL:6 * D_MODEL]
    y = _dot(o_ref[...], wout_ref[0:V_W, :]) + _dot(yf, wout_ref[V_W:, :])
    x = x_in + gt_m * _rms(y, gpm_ref[...])
    h = (_rms(x, gpre_ref[...]) * (1.0 + sc_f) + sh_f).astype(BF16)
    f = jnp.zeros((ROW_TILE, D_MODEL), F32)
    n_split = 4
    wf = D_FF // n_split
    for j in range(n_split):
        a = jnp.maximum(_dot(h, w1_ref[:, j * wf:(j + 1) * wf]), 0.0)
        f = f + _dot((a * a).astype(BF16), w2_ref[j * wf:(j + 1) * wf, :])
    out_ref[...] = x + gt_f * _rms(f, gpost_ref[...])


def _post(layer, x, o, fno, mod, mod_sel, g_post_mix, g_pre_mlp, g_post_mlp, w_out, w_fc1, w_fc2,
          pos=None):
    t = x.shape[0]
    row = lambda w: pl.BlockSpec((ROW_TILE, w), lambda i: (i, 0))
    lay = lambda *shape: pl.BlockSpec((None,) + shape, lambda i: (layer,) + (0,) * len(shape),
                                      pipeline_mode=pl.Buffered(1))
    dft_len = fno[2].shape[0] if len(fno) == 4 else 0
    fno_specs = [row(FNO_W)] * min(len(fno), 2) + [
        pl.BlockSpec((dft_len, dft_len), lambda i: (0, 0))] * (len(fno) - min(len(fno), 2))
    return pl.pallas_call(
        functools.partial(_post_kernel, dft_len=dft_len, pos_tiles=pos[1] if pos else 0),
        grid=(t // ROW_TILE,),
        in_specs=[row(D_MODEL), row(V_W), *fno_specs,
                  *([pl.BlockSpec(pos[0].shape, lambda i: (0, 0))] if pos else []),
                  pl.BlockSpec((None, None, 1, 6 * D_MODEL), lambda i: (layer, mod_sel(i), 0, 0)),
                  lay(1, D_MODEL), lay(1, D_MODEL), lay(1, D_MODEL),
                  lay(V_W + FNO_W, D_MODEL), lay(D_MODEL, D_FF), lay(D_FF, D_MODEL)],
        out_specs=row(D_MODEL),
        out_shape=jax.ShapeDtypeStruct((t, D_MODEL), F32),
        compiler_params=_params(1),
        name="post",
    )(x, o, *fno, *([pos[0]] if pos else []), mod, g_post_mix, g_pre_mlp, g_post_mlp, w_out, w_fc1, w_fc2)


def _state_to_kernel(s):
    st = jnp.swapaxes(s, -1, -2)
    z = jnp.zeros_like(st)
    even = jnp.concatenate([st, z], axis=-1)
    odd = jnp.concatenate([z, st], axis=-1)
    is_even = (jnp.arange(GLA_HEADS) % 2 == 0)[None, None, :, None, None]
    return jnp.where(is_even, even, odd)


def kernel(x_prompt, x_sample, state_gla_fwd, state_gla_bwd, c, c_ctx, w_in, w_out, w_a2, b_a2,
           g_head, w_ada, b_ada, g_pre_mix, g_post_mix, g_pre_mlp, g_post_mlp, w_fc1, w_fc2):
    batch, seq, _ = x_prompt.shape
    dec_batch, dec_seq, _ = x_sample.shape
    t_ctx = batch * seq
    t_lat = dec_batch * dec_seq
    assert dec_seq == GRID_W * GRID_W and seq % CHUNK == 0 and ROW_TILE % seq == 0
    lat_tiles = dec_seq // ROW_TILE
    mod_sel = (lambda i: 0, lambda i: 1 + i // lat_tiles)

    za = jnp.zeros((DEPTH, LR_RANK, QK_W), F32)
    wa2 = jnp.concatenate([jnp.concatenate([w_a2[:, 0], za], axis=-1),
                           jnp.concatenate([za, w_a2[:, 1]], axis=-1)], axis=1).astype(BF16)
    ba2 = b_a2.reshape(DEPTH, 1, 2 * QK_W)
    w_t = jnp.swapaxes(w_in, 1, 2)
    w_out_b = w_out.astype(BF16)
    w_fc1_b = w_fc1.astype(BF16)
    w_fc2_b = w_fc2.astype(BF16)
    vec = lambda a: a.reshape(DEPTH, 1, -1)
    g_head_r = g_head.reshape(DEPTH, 1, V_W)

    gapf, gapb, lvlf, lvlb = _gap_matrices()
    as_bf16 = lambda a: jnp.asarray(a, F32).astype(BF16)
    consts = (as_bf16(gapf), as_bf16(gapb), jnp.asarray(lvlf), jnp.asarray(lvlb))
    cw, sw = _dft_tables(FNO_GROUP_W)
    cs_w = as_bf16(np.concatenate([cw, sw], axis=1))
    cn, sn = _dft_tables(seq)
    cos_ctx, nsin_ctx = as_bf16(cn), as_bf16(-sn)
    pos = (jnp.asarray(_pos_table()), lat_tiles)

    cvec = jnp.zeros((MOD_ROWS, D_MODEL), F32).at[0].set(c_ctx).at[1:1 + dec_batch].set(c)
    mod = _modulation(cvec, w_ada, b_ada)[:, :1 + dec_batch, None, :]

    xs = [x_prompt.reshape(t_ctx, D_MODEL), x_sample.reshape(t_lat, D_MODEL)]
    s0 = (_state_to_kernel(state_gla_fwd), _state_to_kernel(state_gla_bwd))

    states = None
    for l in range(DEPTH):
        lat_pos = pos if l == 0 else None
        pre = [_premix(l, x, mod, sel, vec(g_pre_mix), w_t, wa2, ba2, cs_w, dt, p)
               for x, sel, dt, p in zip(xs, mod_sel, (BF16, F32), (None, lat_pos))]
        q, k, v, g, la, uc, us = pre[0]
        o_ctx, *states = _gla(l, q, k, v, la, g, g_head_r, consts, seq, batch, 4, emit_state=True,
                              st_prev=states)
        fno_ctx = (uc, us, cos_ctx, nsin_ctx)
        q, k, v, g, la, uc, us = pre[1]
        (o_lat,) = _gla(l, q, k, v, la, g, g_head_r, consts, dec_seq, dec_batch, 1, s0=s0)
        fno_lat = (_fnet_fft(uc, us, dec_seq, dec_batch),)
        xs = [_post(l, x, o, fno, mod, sel, vec(g_post_mix), vec(g_pre_mlp), vec(g_post_mlp),
                    w_out_b, w_fc1_b, w_fc2_b, p)
              for x, o, fno, sel, p in zip(xs, (o_ctx, o_lat), (fno_ctx, fno_lat), mod_sel, (None, lat_pos))]

    y_prompt = xs[0].reshape(batch, seq, D_MODEL)
    y_sample = xs[1].reshape(dec_batch, dec_seq, D_MODEL)
    return (y_prompt, y_sample, states[0], states[1])
```

```python
import functools

import numpy as np
import jax
import jax.numpy as jnp
from jax import lax
from jax.experimental import pallas as pl
from jax.experimental.pallas import tpu as pltpu

D_MODEL = 1024
DEPTH = 4
GRID_W = 64
GLA_HEADS = 4
GLA_DK = 64
GLA_DV = 128
CHUNK = 64
GATE_TAU = 16.0
LR_RANK = 16
FNO_GROUPS = 4
FNO_GROUP_W = 128
D_FF = 4 * D_MODEL
EPS = 1e-6
QK_W = GLA_HEADS * GLA_DK
V_W = GLA_HEADS * GLA_DV
FNO_W = FNO_GROUPS * FNO_GROUP_W
MAIN_W = 2 * QK_W + 2 * V_W
LR_W = 2 * LR_RANK
D_IN = MAIN_W + LR_W + FNO_W
HEAD_PAIRS = GLA_HEADS // 2
PAIR_K = 2 * GLA_DK
PAIR_V = 2 * GLA_DV
N_LEVELS = 6
N_GAPS = N_LEVELS + 2
SCAN_UNROLL = 8
FAST_STEP_LIMIT = 1.0

VMEM_LIMIT = 48 * 1024 * 1024
ROW_TILE = 512
POST_HEAD_ROWS = 128
MOD_ROWS = 8
FFT_UNROLL = 16

BF16 = jnp.bfloat16
F32 = jnp.float32


def _gap_matrices():
    c = CHUNK
    m_all = np.zeros((N_GAPS, c, c), np.float32)
    for p in range(c):
        m_all[0, p, :p + 1] = 1.0
        m_all[1, p, p + 1:] = 1.0
    for lv in range(N_LEVELS):
        m = c >> (lv + 1)
        for p in range(c):
            base = (p // (2 * m)) * 2 * m
            ref = base + m - 1
            if p > ref:
                m_all[2 + lv, p, ref + 1:p + 1] = 1.0
            else:
                m_all[2 + lv, p, p + 1:ref + 1] = 1.0
    fwd = m_all.reshape(N_GAPS * c, c)
    bwd = m_all[:, ::-1, ::-1].reshape(N_GAPS * c, c)
    lvl = np.full((c, c), N_LEVELS + 1, np.int32)
    for t in range(c):
        for s in range(t + 1):
            if s == t:
                lvl[t, s] = 0
            else:
                top = (t ^ s).bit_length() - 1
                lvl[t, s] = N_LEVELS - top
    lvl_f = np.concatenate([lvl, lvl], axis=1)
    lvl_b = np.concatenate([lvl[::-1, ::-1], lvl[::-1, ::-1]], axis=1)
    return fwd, bwd, lvl_f, lvl_b


def _dft_tables(n):
    idx = np.arange(n)
    ang = 2.0 * np.pi * ((idx[:, None] * idx[None, :]) % n) / n
    s = 1.0 / np.sqrt(n)
    return (np.cos(ang) * s).astype(np.float32), (np.sin(ang) * s).astype(np.float32)


def _silu(x):
    return x * (1.0 / (1.0 + jnp.exp(-x)))


def _rms(x, g):
    return x * lax.rsqrt(jnp.mean(x * x, axis=-1, keepdims=True) + EPS) * g


def _dot(a, b):
    return jnp.dot(a, b, preferred_element_type=F32)


def _dot_nt(a, b):
    return lax.dot_general(a, b, (((1,), (1,)), ((), ())), preferred_element_type=F32)


def _dot_tn(a, b):
    return lax.dot_general(a, b, (((0,), (0,)), ((), ())), preferred_element_type=F32)


def _params(n_grid):
    return pltpu.CompilerParams(dimension_semantics=("arbitrary",) * n_grid,
                                vmem_limit_bytes=VMEM_LIMIT)


def _mod_kernel(c_ref, w_ref, b_ref, o_ref):
    s = _silu(c_ref[...])
    o_ref[...] = jnp.dot(s, w_ref[...], preferred_element_type=F32,
                         precision=lax.Precision.HIGHEST) + b_ref[...]


def _modulation(cvec, w_ada, b_ada):
    tn = 3072
    return pl.pallas_call(
        _mod_kernel,
        grid=(DEPTH, 6 * D_MODEL // tn),
        in_specs=[pl.BlockSpec((MOD_ROWS, D_MODEL), lambda l, j: (0, 0)),
                  pl.BlockSpec((None, D_MODEL, tn), lambda l, j: (l, 0, j)),
                  pl.BlockSpec((None, 1, tn), lambda l, j: (l, 0, j))],
        out_specs=pl.BlockSpec((None, MOD_ROWS, tn), lambda l, j: (l, 0, j)),
        out_shape=jax.ShapeDtypeStruct((DEPTH, MOD_ROWS, 6 * D_MODEL), F32),
        compiler_params=_params(2),
        name="modulation",
    )(cvec, w_ada, b_ada.reshape(DEPTH, 1, 6 * D_MODEL))


def _pos_table():
    nf = D_MODEL // 4
    omega = 1.0 / (10000.0 ** (np.arange(nf, dtype=np.float64) / nf))
    idx = np.arange(GRID_W, dtype=np.float64)[:, None]
    return np.concatenate([np.sin(idx * omega), np.cos(idx * omega)], axis=1).astype(np.float32)


def _pos_tile(tab_ref, tile):
    grid_rows = ROW_TILE // GRID_W
    r0 = pl.multiple_of(tile * grid_rows, grid_rows)
    t = tab_ref[pl.ds(r0, grid_rows), :]
    by_row = jnp.concatenate([jnp.broadcast_to(t[i:i + 1], (GRID_W, D_MODEL // 2))
                              for i in range(grid_rows)], axis=0)
    by_col = jnp.concatenate([tab_ref[...]] * grid_rows, axis=0)
    return jnp.concatenate([by_row, by_col], axis=1)


def _premix_kernel(*refs, pos_tiles):
    x_ref, mod_ref, gpre_ref, w_ref, wa2_ref, ba2_ref, cs_ref = refs[:7]
    refs = refs[7:]
    if pos_tiles:
        tab_ref, refs = refs[0], refs[1:]
    q_ref, k_ref, v_ref, g_ref, la_ref, uc_ref, us_ref, wbf_ref = refs

    @pl.when(pl.program_id(0) == 0)
    def _():
        wbf_ref[...] = w_ref[...].astype(BF16)

    x = x_ref[...]
    if pos_tiles:
        x = x + _pos_tile(tab_ref, pl.program_id(0) % pos_tiles)
    sh = mod_ref[:, 0:D_MODEL]
    sc = mod_ref[:, D_MODEL:2 * D_MODEL]
    h = (_rms(x, gpre_ref[...]) * (1.0 + sc) + sh).astype(BF16)
    u = _dot_nt(h, wbf_ref[MAIN_W + LR_W:, :]).astype(BF16)
    lr = _dot_nt(h, wbf_ref[MAIN_W:MAIN_W + LR_W, :]).astype(BF16)
    q_ref[...] = _dot_nt(h, wbf_ref[0:QK_W, :]) * (GLA_DK ** -0.5)
    k_ref[...] = _dot_nt(h, wbf_ref[QK_W:2 * QK_W, :])
    v_ref[...] = _dot_nt(h, wbf_ref[2 * QK_W:2 * QK_W + V_W, :]).astype(BF16)
    g_ref[...] = _dot_nt(h, wbf_ref[2 * QK_W + V_W:MAIN_W, :]).astype(BF16)
    logit = _dot(lr, wa2_ref[...]) + ba2_ref[...]
    softplus_neg = jnp.maximum(-logit, 0.0) + jnp.log(1.0 + jnp.exp(-jnp.abs(logit)))
    la_ref[...] = softplus_neg * (-1.0 / GATE_TAU)
    for grp in range(FNO_GROUPS):
        sl = slice(grp * FNO_GROUP_W, (grp + 1) * FNO_GROUP_W)
        ucs = _dot(u[:, sl], cs_ref[...])
        uc_ref[:, sl] = ucs[:, :FNO_GROUP_W].astype(uc_ref.dtype)
        us_ref[:, sl] = ucs[:, FNO_GROUP_W:].astype(us_ref.dtype)


def _premix(layer, x, mod, mod_sel, g_pre, w_t, wa2, ba2, cs_w, fno_dtype, pos=None):
    t = x.shape[0]
    pos_args, pos_specs, pos_tiles = [], [], 0
    if pos is not None:
        pos_args, pos_tiles = [pos[0]], pos[1]
        pos_specs = [pl.BlockSpec(pos[0].shape, lambda i: (0, 0))]
    row = lambda w: pl.BlockSpec((ROW_TILE, w), lambda i: (i, 0))
    lay = lambda *shape: pl.BlockSpec((None,) + shape, lambda i: (layer,) + (0,) * len(shape))
    out = lambda w, dt: jax.ShapeDtypeStruct((t, w), dt)
    return pl.pallas_call(
        functools.partial(_premix_kernel, pos_tiles=pos_tiles),
        grid=(t // ROW_TILE,),
        in_specs=[row(D_MODEL),
                  pl.BlockSpec((None, None, 1, 6 * D_MODEL), lambda i: (layer, mod_sel(i), 0, 0)),
                  lay(1, D_MODEL),
                  pl.BlockSpec((None, D_IN, D_MODEL), lambda i: (layer, 0, 0), pipeline_mode=pl.Buffered(1)),
                  lay(LR_W, 2 * QK_W), lay(1, 2 * QK_W),
                  pl.BlockSpec((FNO_GROUP_W, 2 * FNO_GROUP_W), lambda i: (0, 0)), *pos_specs],
        out_specs=[row(QK_W), row(QK_W), row(V_W), row(V_W), row(2 * QK_W), row(FNO_W), row(FNO_W)],
        out_shape=[out(QK_W, F32), out(QK_W, F32), out(V_W, BF16), out(V_W, BF16),
                   out(2 * QK_W, F32), out(FNO_W, fno_dtype), out(FNO_W, fno_dtype)],
        scratch_shapes=[pltpu.VMEM((D_IN, D_MODEL), BF16)],
        compiler_params=_params(1),
        name="premix",
    )(x, mod, g_pre, w_t, wa2, ba2, cs_w, *pos_args)


def _gla_chunk(r0, forward, q_ref, k_ref, v_ref, la_ref, gap_ref, lvl_ref, st_ref):
    rows = pl.ds(r0, CHUNK)
    lane = lax.broadcasted_iota(jnp.int32, (CHUNK, PAIR_K), 1)
    head0 = lane < GLA_DK
    la = la_ref[rows, :]
    la_hi = la.astype(BF16)
    la_lo = (la - la_hi.astype(F32)).astype(BF16)
    gaps = _dot(gap_ref[...], jnp.concatenate([la_hi, la_lo], axis=1))
    decay = jnp.exp(gaps[:, :PAIR_K] + gaps[:, PAIR_K:])
    blk = lambda j: decay[j * CHUNK:(j + 1) * CHUNK]
    q = q_ref[rows, :]
    k = k_ref[rows, :]
    k_h = (jnp.where(head0, k, 0.0), jnp.where(head0, 0.0, k))
    v = v_ref[rows, :]
    lvl = lvl_ref[...]

    a = jnp.zeros((CHUNK, 2 * CHUNK), F32)
    for lv in range(N_LEVELS + 1):
        if lv == 0:
            ql = q.astype(BF16)
            kl = jnp.concatenate([k_h[0].astype(BF16), k_h[1].astype(BF16)], axis=0)
        else:
            d = blk(1 + lv)
            ql = (q * d).astype(BF16)
            kl = jnp.concatenate([(k_h[0] * d).astype(BF16), (k_h[1] * d).astype(BF16)], axis=0)
        a = jnp.where(lvl == lv, _dot_nt(ql, kl), a)

    q_in = (q * blk(0)).astype(BF16)
    inter = _dot_nt(q_in, _state_per_head(st_ref[...]))
    vlane = lax.broadcasted_iota(jnp.int32, (CHUNK, PAIR_V), 1)
    zero = jnp.zeros_like(v)
    v_blk = jnp.concatenate([jnp.where(vlane < GLA_DV, v, zero),
                             jnp.where(vlane < GLA_DV, zero, v)], axis=0)
    o = inter + _dot(a.astype(BF16), v_blk)

    d_out = blk(1)
    total = blk(0)[CHUNK - 1:CHUNK] if forward else blk(0)[0:1]
    k_out = jnp.concatenate([(k_h[0] * d_out).astype(BF16), (k_h[1] * d_out).astype(BF16)], axis=0)
    st_ref[...] = st_ref[...] * total + _dot_tn(_stack_heads(v), k_out)
    return o


def _state_per_head(st):
    lane = lax.broadcasted_iota(jnp.int32, (1, PAIR_K), 1)
    keep0 = (lane < GLA_DK).astype(BF16)
    st = st.astype(BF16)
    return jnp.concatenate([st * keep0, st * (1.0 - keep0).astype(BF16)], axis=0)


def _stack_heads(v):
    return jnp.concatenate([v[:, :GLA_DV], v[:, GLA_DV:]], axis=0)


def _gla_fast_step(jobs, fwd_refs, bwd_refs):
    refs = [fwd_refs if fw else bwd_refs for _, fw, _ in jobs]
    rows = [pl.ds(r0, CHUNK) for r0, _, _ in jobs]
    lane = lax.broadcasted_iota(jnp.int32, (CHUNK, PAIR_K), 1)
    head0 = lane < GLA_DK
    vlane = lax.broadcasted_iota(jnp.int32, (CHUNK, PAIR_V), 1) < GLA_DV

    la = [r[3][rw, :] for r, rw in zip(refs, rows)]
    la_hi = [x.astype(BF16) for x in la]
    la_lo = [(x - h.astype(F32)).astype(BF16) for x, h in zip(la, la_hi)]
    cum = [_dot(r[4][0:CHUNK, :], jnp.concatenate([h, l], axis=1))
           for r, h, l in zip(refs, la_hi, la_lo)]
    b = [x[:, :PAIR_K] + x[:, PAIR_K:] for x in cum]
    b_end = [x[CHUNK - 1:CHUNK] if fw else x[0:1] for x, (_, fw, _) in zip(b, jobs)]
    q = [r[0][rw, :] for r, rw in zip(refs, rows)]
    k = [r[1][rw, :] for r, rw in zip(refs, rows)]
    v = [r[2][rw, :] for r, rw in zip(refs, rows)]
    qd = [(x * jnp.exp(y)).astype(BF16) for x, y in zip(q, b)]
    kd = [x * jnp.exp(-y) for x, y in zip(k, b)]
    ko = [x * jnp.exp(e - y) for x, y, e in zip(k, b, b_end)]
    total = [jnp.exp(e) for e in b_end]
    kd_cat = [jnp.concatenate([jnp.where(head0, x, 0.0).astype(BF16),
                               jnp.where(head0, 0.0, x).astype(BF16)], axis=0) for x in kd]
    score = [_dot_nt(x, y) for x, y in zip(qd, kd_cat)]
    a = [jnp.where(r[5][...] <= N_LEVELS, s, 0.0).astype(BF16) for r, s in zip(refs, score)]
    v_blk = [jnp.concatenate([jnp.where(vlane, x, jnp.zeros_like(x)),
                              jnp.where(vlane, jnp.zeros_like(x), x)], axis=0) for x in v]
    intra = [_dot(x, y) for x, y in zip(a, v_blk)]
    ko_cat = [jnp.concatenate([jnp.where(head0, x, 0.0).astype(BF16),
                               jnp.where(head0, 0.0, x).astype(BF16)], axis=0) for x in ko]
    upd = [_dot_tn(_stack_heads(x), y) for x, y in zip(v, ko_cat)]

    outs = [None] * len(jobs)
    for scan in sorted({(fw, slot) for _, fw, slot in jobs}):
        st_ref = (fwd_refs if scan[0] else bwd_refs)[6]
        st = st_ref[scan[1]]
        for j, (_, fw, slot) in enumerate(jobs):
            if (fw, slot) != scan:
                continue
            outs[j] = _dot_nt(qd[j], _state_per_head(st)) + intra[j]
            st = st * total[j] + upd[j]
        st_ref[scan[1]] = st
    return outs


def _gla_kernel(*refs, seq_len, n_seq, has_state, emit_state, unroll, par, layer, all_layers, n_alias):
    (q_ref, k_ref, v_ref, laf_ref, lab_ref, g_ref, gh_ref,
     gapf_ref, gapb_ref, lvlf_ref, lvlb_ref) = refs[:11]
    pos = 11
    if has_state:
        s0f_ref, s0b_ref = refs[pos:pos + 2]
        pos += 2
    pos += n_alias
    o_ref = refs[pos]
    pos += 1
    if emit_state:
        sf_ref, sb_ref = refs[pos:pos + 2]
        pos += 2
        if all_layers:
            sf_ref[...] = jnp.zeros_like(sf_ref)
            sb_ref[...] = jnp.zeros_like(sb_ref)
    acc_ref, stf_ref, stb_ref = refs[pos:pos + 3]
    n_chunks = seq_len // CHUNK
    half_iters = n_chunks // (2 * unroll)
    fwd_refs = (q_ref, k_ref, v_ref, laf_ref, gapf_ref, lvlf_ref, stf_ref)
    bwd_refs = (q_ref, k_ref, v_ref, lab_ref, gapb_ref, lvlb_ref, stb_ref)

    def finish(o, rows):
        g = g_ref[rows, :].astype(F32)
        for hd in range(2):
            sl = slice(hd * GLA_DV, (hd + 1) * GLA_DV)
            oh = _rms(o[:, sl], gh_ref[:, sl]) * _silu(g[:, sl])
            o_ref[rows, sl] = oh.astype(o_ref.dtype)

    def seq_body(sg, carry, fast):
        base = sg * (par * seq_len)
        if has_state:
            stf_ref[0] = s0f_ref[...]
            stb_ref[0] = s0b_ref[...]
        else:
            stf_ref[...] = jnp.zeros_like(stf_ref)
            stb_ref[...] = jnp.zeros_like(stb_ref)

        def step(it, second_visit):
            jobs = []
            for slot in range(par):
                for u in range(unroll):
                    cf = it * unroll + u
                    for forward, ci in ((True, cf), (False, n_chunks - 1 - cf)):
                        r0 = base + slot * seq_len + ci * CHUNK
                        jobs.append((pl.multiple_of(r0, CHUNK), forward, slot))
            if fast:
                outs = _gla_fast_step(jobs, fwd_refs, bwd_refs)
            else:
                outs = [_gla_chunk(r0, fw, *(fwd_refs if fw else bwd_refs)[:6],
                                   (stf_ref if fw else stb_ref).at[slot]) for r0, fw, slot in jobs]
            for (r0, _, _), o in zip(jobs, outs):
                rows = pl.ds(r0, CHUNK)
                if second_visit:
                    finish(o + acc_ref[rows, :], rows)
                else:
                    acc_ref[rows, :] = o

        def first(i, c):
            step(i, False)
            return c

        def second(i, c):
            step(half_iters + i, True)
            return c

        lax.fori_loop(0, half_iters, first, 0)
        lax.fori_loop(0, half_iters, second, 0)
        if emit_state:
            for s_out, st_ref in ((sf_ref, stf_ref), (sb_ref, stb_ref)):
                for slot in range(par):
                    for hd in range(2):
                        s_fin = st_ref[slot].T[hd * GLA_DK:(hd + 1) * GLA_DK, :]
                        if all_layers:
                            s_out[sg * par + slot, layer, hd] = s_fin
                        else:
                            s_out[sg * par + slot, hd] = s_fin
        return carry

    weakest = jnp.minimum(jnp.min(laf_ref[...]), jnp.min(lab_ref[...]))
    is_fast = weakest >= -FAST_STEP_LIMIT

    @pl.when(is_fast)
    def _():
        lax.fori_loop(0, n_seq // par, functools.partial(seq_body, fast=True), 0)

    @pl.when(jnp.logical_not(is_fast))
    def _():
        lax.fori_loop(0, n_seq // par, functools.partial(seq_body, fast=False), 0)


def _gla(layer, q, k, v, la, g, g_head, consts, seq_len, n_seq_total, n_seq, s0=None,
         emit_state=False, st_prev=None):
    gapf, gapb, lvlf, lvlb = consts
    rows = seq_len * n_seq
    n_steps = n_seq_total // n_seq
    rspec = lambda w, off=0: pl.BlockSpec((rows, w), lambda i, j: (i, j + off))
    const = lambda a: pl.BlockSpec(a.shape, lambda i, j: (0,) * a.ndim)
    in_specs = [rspec(PAIR_K), rspec(PAIR_K), rspec(PAIR_V), rspec(PAIR_K), rspec(PAIR_K, HEAD_PAIRS),
                rspec(PAIR_V),
                pl.BlockSpec((None, 1, PAIR_V), lambda i, j: (layer, 0, j)),
                const(gapf), const(gapb), const(lvlf), const(lvlb)]
    args = [q, k, v, la, la, g, g_head, gapf, gapb, lvlf, lvlb]
    if s0 is not None:
        sspec = pl.BlockSpec((None, None, None, GLA_DV, PAIR_K), lambda i, j: (i, layer, j, 0, 0))
        in_specs += [sspec, sspec]
        args += list(s0)
    out_specs = [rspec(PAIR_V)]
    out_shape = [jax.ShapeDtypeStruct((n_seq_total * seq_len, V_W), BF16)]
    aliases = {}
    if emit_state:
        if st_prev is None:
            st_spec = pl.BlockSpec((n_seq, DEPTH, 2, GLA_DK, GLA_DV), lambda i, j: (i, 0, j, 0, 0))
        else:
            st_spec = pl.BlockSpec((n_seq, None, 2, GLA_DK, GLA_DV), lambda i, j: (i, layer, j, 0, 0))
            for n, a in enumerate(st_prev):
                aliases[len(args)] = 1 + n
                in_specs.append(pl.BlockSpec(memory_space=pl.ANY))
                args.append(a)
        st_shape = jax.ShapeDtypeStruct((n_seq_total, DEPTH, GLA_HEADS, GLA_DK, GLA_DV), F32)
        out_specs += [st_spec, st_spec]
        out_shape += [st_shape, st_shape]
    unroll = min(SCAN_UNROLL, seq_len // (2 * CHUNK))
    par = 1 if s0 is not None else min(n_seq, SCAN_UNROLL // unroll)
    kern = functools.partial(_gla_kernel, seq_len=seq_len, n_seq=n_seq, has_state=s0 is not None,
                             emit_state=emit_state, unroll=unroll, par=par, layer=layer,
                             all_layers=emit_state and st_prev is None, n_alias=len(aliases))
    return pl.pallas_call(
        kern,
        grid=(n_steps, HEAD_PAIRS),
        in_specs=in_specs,
        out_specs=out_specs,
        out_shape=out_shape,
        input_output_aliases=aliases,
        scratch_shapes=[pltpu.VMEM((rows, PAIR_V), F32), pltpu.VMEM((par, GLA_DV, PAIR_K), F32),
                        pltpu.VMEM((par, GLA_DV, PAIR_K), F32)],
        compiler_params=_params(2),
        name="gla_state" if emit_state else "gla",
    )(*args)


def _fft_tables(n_side):
    c, s = _dft_tables(n_side)
    l1 = np.block([[c, -s], [-s, -c]])
    l2 = np.concatenate([c, s], axis=1)
    idx = np.arange(n_side)
    ang = 2.0 * np.pi * (idx[:, None] * idx[None, :]) / (n_side * n_side)
    lanes = np.ones((1, 1, FNO_GROUP_W), np.float32)
    tc = np.cos(ang).astype(np.float32)[:, :, None] * lanes
    ts = np.sin(ang).astype(np.float32)[:, :, None] * lanes
    return l1, l2, tc, ts


def _fft_kernel(uc_ref, us_ref, l1_ref, l2_ref, tc_ref, ts_ref, o_ref, zr_ref, zi_ref, *, n_side):
    def stage1(n2, carry):
        col = pl.ds(n2, n_side, stride=n_side)
        x = jnp.concatenate([uc_ref[col, :].astype(BF16), us_ref[col, :].astype(BF16)], axis=0)
        z = _dot(l1_ref[...], x)
        zr, zi = z[:n_side], z[n_side:]
        tc, ts = tc_ref[n2], ts_ref[n2]
        out = pl.ds(pl.multiple_of(n2 * n_side, n_side), n_side)
        zr_ref[out, :] = zr * tc + zi * ts
        zi_ref[out, :] = zi * tc - zr * ts
        return carry

    def stage2(k1, carry):
        col = pl.ds(k1, n_side, stride=n_side)
        z = jnp.concatenate([zr_ref[col, :].astype(BF16), zi_ref[col, :].astype(BF16)], axis=0)
        o_ref[col, :] = _dot(l2_ref[...], z)
        return carry

    lax.fori_loop(0, n_side, stage1, 0, unroll=FFT_UNROLL)
    lax.fori_loop(0, n_side, stage2, 0, unroll=FFT_UNROLL)


def _fnet_fft(uc, us, seq_len, n_seq_total):
    n_side = GRID_W
    l1, l2, tc, ts = _fft_tables(n_side)
    as_bf16 = lambda a: jnp.asarray(a, F32).astype(BF16)
    blk = pl.BlockSpec((seq_len, FNO_GROUP_W), lambda b, g: (b, g))
    const = lambda a: pl.BlockSpec(a.shape, lambda b, g: (0,) * a.ndim)
    return pl.pallas_call(
        functools.partial(_fft_kernel, n_side=n_side),
        grid=(n_seq_total, FNO_GROUPS),
        in_specs=[blk, blk, const(l1), const(l2), const(tc), const(ts)],
        out_specs=blk,
        out_shape=jax.ShapeDtypeStruct((n_seq_total * seq_len, FNO_W), F32),
        scratch_shapes=[pltpu.VMEM((seq_len, FNO_GROUP_W), F32), pltpu.VMEM((seq_len, FNO_GROUP_W), F32)],
        compiler_params=_params(2),
        name="fnet_fft",
    )(uc, us, as_bf16(l1), as_bf16(l2), jnp.asarray(tc), jnp.asarray(ts))


def _post_kernel(*refs, dft_len, pos_tiles):
    if dft_len:
        x_ref, o_ref, uc_ref, us_ref, cos_ref, nsin_ref = refs[:6]
        refs = refs[6:]
        yf = jnp.concatenate(
            [(_dot(cos_ref[...], uc_ref[r0:r0 + dft_len, :]) +
              _dot(nsin_ref[...], us_ref[r0:r0 + dft_len, :])).astype(BF16)
             for r0 in range(0, ROW_TILE, dft_len)], axis=0)
    else:
        x_ref, o_ref, yf_ref = refs[:3]
        refs = refs[3:]
        yf = yf_ref[...].astype(BF16)
    x_in = x_ref[...]
    if pos_tiles:
        x_in = x_in + _pos_tile(refs[0], pl.program_id(0) % pos_tiles)
        refs = refs[1:]
    mod_ref, gpm_ref, gpre_ref, gpost_ref, wout_ref, w1_ref, w2_ref, out_ref = refs
    gt_m = mod_ref[:, 2 * D_MODEL:3 * D_MODEL]
    sh_f = mod_ref[:, 3 * D_MODEL:4 * D_MODEL]
    sc_f = mod_ref[:, 4 * D_MODEL:5 * D_MODEL]
    gt_f = mod_ref[:, 5 * D_MODEL:6 * D_MODEL]
    x_blk, h_blk = [], []
    for r0 in range(0, ROW_TILE, POST_HEAD_ROWS):
        rows = slice(r0, r0 + POST_HEAD_ROWS)
        y = _dot(o_ref[rows, :], wout_ref[0:V_W, :]) + _dot(yf[rows, :], wout_ref[V_W:, :])
        x_blk.append(x_in[rows, :] + gt_m * _rms(y, gpm_ref[...]))
        h_blk.append((_rms(x_blk[-1], gpre_ref[...]) * (1.0 + sc_f) + sh_f).astype(BF16))
    x = jnp.concatenate(x_blk, axis=0)
    h = jnp.concatenate(h_blk, axis=0)
    f = jnp.zeros((ROW_TILE, D_MODEL), F32)
    n_split = 4
    wf = D_FF // n_split
    for j in range(n_split):
        a = jnp.maximum(_dot(h, w1_ref[:, j * wf:(j + 1) * wf]), 0.0)
        f = f + _dot((a * a).astype(BF16), w2_ref[j * wf:(j + 1) * wf, :])
    out_ref[...] = x + gt_f * _rms(f, gpost_ref[...])


def _post(layer, x, o, fno, mod, mod_sel, g_post_mix, g_pre_mlp, g_post_mlp, w_out, w_fc1, w_fc2,
          pos=None):
    t = x.shape[0]
    row = lambda w: pl.BlockSpec((ROW_TILE, w), lambda i: (i, 0))
    lay = lambda *shape: pl.BlockSpec((None,) + shape, lambda i: (layer,) + (0,) * len(shape),
                                      pipeline_mode=pl.Buffered(1))
    dft_len = fno[2].shape[0] if len(fno) == 4 else 0
    fno_specs = [row(FNO_W)] * min(len(fno), 2) + [
        pl.BlockSpec((dft_len, dft_len), lambda i: (0, 0))] * (len(fno) - min(len(fno), 2))
    return pl.pallas_call(
        functools.partial(_post_kernel, dft_len=dft_len, pos_tiles=pos[1] if pos else 0),
        grid=(t // ROW_TILE,),
        in_specs=[row(D_MODEL), row(V_W), *fno_specs,
                  *([pl.BlockSpec(pos[0].shape, lambda i: (0, 0))] if pos else []),
                  pl.BlockSpec((None, None, 1, 6 * D_MODEL), lambda i: (layer, mod_sel(i), 0, 0)),
                  lay(1, D_MODEL), lay(1, D_MODEL), lay(1, D_MODEL),
                  lay(V_W + FNO_W, D_MODEL), lay(D_MODEL, D_FF), lay(D_FF, D_MODEL)],
        out_specs=row(D_MODEL),
        out_shape=jax.ShapeDtypeStruct((t, D_MODEL), F32),
        compiler_params=_params(1),
        name="post",
    )(x, o, *fno, *([pos[0]] if pos else []), mod, g_post_mix, g_pre_mlp, g_post_mlp, w_out, w_fc1, w_fc2)


def _state_to_kernel(s):
    b, l = s.shape[:2]
    st = s.reshape(b, l, HEAD_PAIRS, 2, GLA_DK, GLA_DV)
    return jnp.transpose(st, (0, 1, 2, 5, 3, 4)).reshape(b, l, HEAD_PAIRS, GLA_DV, PAIR_K)


def kernel(x_prompt, x_sample, state_gla_fwd, state_gla_bwd, c, c_ctx, w_in, w_out, w_a2, b_a2,
           g_head, w_ada, b_ada, g_pre_mix, g_post_mix, g_pre_mlp, g_post_mlp, w_fc1, w_fc2):
    batch, seq, _ = x_prompt.shape
    dec_batch, dec_seq, _ = x_sample.shape
    t_ctx = batch * seq
    t_lat = dec_batch * dec_seq
    assert dec_seq == GRID_W * GRID_W and seq % CHUNK == 0 and ROW_TILE % seq == 0
    lat_tiles = dec_seq // ROW_TILE
    mod_sel = (lambda i: 0, lambda i: 1 + i // lat_tiles)

    za = jnp.zeros((DEPTH, LR_RANK, QK_W), F32)
    wa2 = jnp.concatenate([jnp.concatenate([w_a2[:, 0], za], axis=-1),
                           jnp.concatenate([za, w_a2[:, 1]], axis=-1)], axis=1).astype(BF16)
    ba2 = b_a2.reshape(DEPTH, 1, 2 * QK_W)
    w_t = jnp.swapaxes(w_in, 1, 2)
    w_out_b = w_out.astype(BF16)
    w_fc1_b = w_fc1.astype(BF16)
    w_fc2_b = w_fc2.astype(BF16)
    vec = lambda a: a.reshape(DEPTH, 1, -1)
    g_head_r = g_head.reshape(DEPTH, 1, V_W)

    gapf, gapb, lvlf, lvlb = _gap_matrices()
    as_bf16 = lambda a: jnp.asarray(a, F32).astype(BF16)
    consts = (as_bf16(gapf), as_bf16(gapb), jnp.asarray(lvlf), jnp.asarray(lvlb))
    cw, sw = _dft_tables(FNO_GROUP_W)
    cs_w = as_bf16(np.concatenate([cw, sw], axis=1))
    cn, sn = _dft_tables(seq)
    cos_ctx, nsin_ctx = as_bf16(cn), as_bf16(-sn)
    pos = (jnp.asarray(_pos_table()), lat_tiles)

    cvec = jnp.zeros((MOD_ROWS, D_MODEL), F32).at[0].set(c_ctx).at[1:1 + dec_batch].set(c)
    mod = _modulation(cvec, w_ada, b_ada)[:, :1 + dec_batch, None, :]

    xs = [x_prompt.reshape(t_ctx, D_MODEL), x_sample.reshape(t_lat, D_MODEL)]
    s0 = (_state_to_kernel(state_gla_fwd), _state_to_kernel(state_gla_bwd))

    states = None
    for l in range(DEPTH):
        lat_pos = pos if l == 0 else None
        pre = [_premix(l, x, mod, sel, vec(g_pre_mix), w_t, wa2, ba2, cs_w, dt, p)
               for x, sel, dt, p in zip(xs, mod_sel, (BF16, F32), (None, lat_pos))]
        q, k, v, g, la, uc, us = pre[0]
        o_ctx, *states = _gla(l, q, k, v, la, g, g_head_r, consts, seq, batch, 4, emit_state=True,
                              st_prev=states)
        fno_ctx = (uc, us, cos_ctx, nsin_ctx)
        q, k, v, g, la, uc, us = pre[1]
        (o_lat,) = _gla(l, q, k, v, la, g, g_head_r, consts, dec_seq, dec_batch, 1, s0=s0)
        fno_lat = (_fnet_fft(uc, us, dec_seq, dec_batch),)
        xs = [_post(l, x, o, fno, mod, sel, vec(g_post_mix), vec(g_pre_mlp), vec(g_post_mlp),
                    w_out_b, w_fc1_b, w_fc2_b, p)
              for x, o, fno, sel, p in zip(xs, (o_ctx, o_lat), (fno_ctx, fno_lat), mod_sel, (None, lat_pos))]

    y_prompt = xs[0].reshape(batch, seq, D_MODEL)
    y_sample = xs[1].reshape(dec_batch, dec_seq, D_MODEL)
    return (y_prompt, y_sample, states[0], states[1])
```

```python
import functools

import numpy as np
import jax
import jax.numpy as jnp
from jax import lax
from jax.experimental import pallas as pl
from jax.experimental.pallas import tpu as pltpu

D_MODEL = 1024
DEPTH = 4
GRID_W = 64
GLA_HEADS = 4
GLA_DK = 64
GLA_DV = 128
CHUNK = 64
GATE_TAU = 16.0
LR_RANK = 16
FNO_GROUPS = 4
FNO_GROUP_W = 128
D_FF = 4 * D_MODEL
EPS = 1e-6
QK_W = GLA_HEADS * GLA_DK
V_W = GLA_HEADS * GLA_DV
FNO_W = FNO_GROUPS * FNO_GROUP_W
MAIN_W = 2 * QK_W + 2 * V_W
LR_W = 2 * LR_RANK
D_IN = MAIN_W + LR_W + FNO_W
HEAD_PAIRS = GLA_HEADS // 2
PAIR_K = 2 * GLA_DK
PAIR_V = 2 * GLA_DV
N_LEVELS = 6
N_GAPS = N_LEVELS + 2
SCAN_UNROLL = 8
FAST_STEP_LIMIT = 1.0

VMEM_LIMIT = 48 * 1024 * 1024
ROW_TILE = 512
POST_HEAD_ROWS = 128
MOD_ROWS = 8
FFT_PITCH = 72
FFT_UNROLL = 16

BF16 = jnp.bfloat16
F32 = jnp.float32


def _gap_matrices():
    c = CHUNK
    m_all = np.zeros((N_GAPS, c, c), np.float32)
    for p in range(c):
        m_all[0, p, :p + 1] = 1.0
        m_all[1, p, p + 1:] = 1.0
    for lv in range(N_LEVELS):
        m = c >> (lv + 1)
        for p in range(c):
            base = (p // (2 * m)) * 2 * m
            ref = base + m - 1
            if p > ref:
                m_all[2 + lv, p, ref + 1:p + 1] = 1.0
            else:
                m_all[2 + lv, p, p + 1:ref + 1] = 1.0
    fwd = m_all.reshape(N_GAPS * c, c)
    bwd = m_all[:, ::-1, ::-1].reshape(N_GAPS * c, c)
    lvl = np.full((c, c), N_LEVELS + 1, np.int32)
    for t in range(c):
        for s in range(t + 1):
            if s == t:
                lvl[t, s] = 0
            else:
                top = (t ^ s).bit_length() - 1
                lvl[t, s] = N_LEVELS - top
    lvl_f = np.concatenate([lvl, lvl], axis=1)
    lvl_b = np.concatenate([lvl[::-1, ::-1], lvl[::-1, ::-1]], axis=1)
    return fwd, bwd, lvl_f, lvl_b


def _dft_tables(n):
    idx = np.arange(n)
    ang = 2.0 * np.pi * ((idx[:, None] * idx[None, :]) % n) / n
    s = 1.0 / np.sqrt(n)
    return (np.cos(ang) * s).astype(np.float32), (np.sin(ang) * s).astype(np.float32)


def _silu(x):
    return x * (1.0 / (1.0 + jnp.exp(-x)))


def _rms(x, g):
    return x * lax.rsqrt(jnp.mean(x * x, axis=-1, keepdims=True) + EPS) * g


def _dot(a, b):
    return jnp.dot(a, b, preferred_element_type=F32)


def _dot_nt(a, b):
    return lax.dot_general(a, b, (((1,), (1,)), ((), ())), preferred_element_type=F32)


def _dot_tn(a, b):
    return lax.dot_general(a, b, (((0,), (0,)), ((), ())), preferred_element_type=F32)


def _params(n_grid):
    return pltpu.CompilerParams(dimension_semantics=("arbitrary",) * n_grid,
                                vmem_limit_bytes=VMEM_LIMIT)


def _mod_kernel(c_ref, w_ref, b_ref, o_ref):
    s = _silu(c_ref[...])
    o_ref[...] = jnp.dot(s, w_ref[...], preferred_element_type=F32,
                         precision=lax.Precision.HIGHEST) + b_ref[...]


def _modulation(cvec, w_ada, b_ada):
    tn = 3072
    return pl.pallas_call(
        _mod_kernel,
        grid=(DEPTH, 6 * D_MODEL // tn),
        in_specs=[pl.BlockSpec((MOD_ROWS, D_MODEL), lambda l, j: (0, 0)),
                  pl.BlockSpec((None, D_MODEL, tn), lambda l, j: (l, 0, j)),
                  pl.BlockSpec((None, 1, tn), lambda l, j: (l, 0, j))],
        out_specs=pl.BlockSpec((None, MOD_ROWS, tn), lambda l, j: (l, 0, j)),
        out_shape=jax.ShapeDtypeStruct((DEPTH, MOD_ROWS, 6 * D_MODEL), F32),
        compiler_params=_params(2),
        name="modulation",
    )(cvec, w_ada, b_ada.reshape(DEPTH, 1, 6 * D_MODEL))


def _pos_table():
    nf = D_MODEL // 4
    omega = 1.0 / (10000.0 ** (np.arange(nf, dtype=np.float64) / nf))
    idx = np.arange(GRID_W, dtype=np.float64)[:, None]
    return np.concatenate([np.sin(idx * omega), np.cos(idx * omega)], axis=1).astype(np.float32)


def _pos_tile(tab_ref, tile):
    grid_rows = ROW_TILE // GRID_W
    r0 = pl.multiple_of(tile * grid_rows, grid_rows)
    t = tab_ref[pl.ds(r0, grid_rows), :]
    by_row = jnp.concatenate([jnp.broadcast_to(t[i:i + 1], (GRID_W, D_MODEL // 2))
                              for i in range(grid_rows)], axis=0)
    by_col = jnp.concatenate([tab_ref[...]] * grid_rows, axis=0)
    return jnp.concatenate([by_row, by_col], axis=1)


def _premix_kernel(*refs, pos_tiles):
    x_ref, mod_ref, gpre_ref, w_ref, wa2_ref, ba2_ref, cs_ref = refs[:7]
    refs = refs[7:]
    if pos_tiles:
        tab_ref, refs = refs[0], refs[1:]
    q_ref, k_ref, v_ref, g_ref, la_ref, uc_ref, us_ref, wbf_ref = refs

    @pl.when(pl.program_id(0) == 0)
    def _():
        wbf_ref[...] = w_ref[...].astype(BF16)

    x = x_ref[...]
    if pos_tiles:
        x = x + _pos_tile(tab_ref, pl.program_id(0) % pos_tiles)
    sh = mod_ref[:, 0:D_MODEL]
    sc = mod_ref[:, D_MODEL:2 * D_MODEL]
    h = (_rms(x, gpre_ref[...]) * (1.0 + sc) + sh).astype(BF16)
    u = _dot_nt(h, wbf_ref[MAIN_W + LR_W:, :]).astype(BF16)
    lr = _dot_nt(h, wbf_ref[MAIN_W:MAIN_W + LR_W, :]).astype(BF16)
    q_ref[...] = _dot_nt(h, wbf_ref[0:QK_W, :]) * (GLA_DK ** -0.5)
    k_ref[...] = _dot_nt(h, wbf_ref[QK_W:2 * QK_W, :])
    v_ref[...] = _dot_nt(h, wbf_ref[2 * QK_W:2 * QK_W + V_W, :]).astype(BF16)
    g_ref[...] = _dot_nt(h, wbf_ref[2 * QK_W + V_W:MAIN_W, :]).astype(BF16)
    logit = _dot(lr, wa2_ref[...]) + ba2_ref[...]
    softplus_neg = jnp.maximum(-logit, 0.0) + jnp.log(1.0 + jnp.exp(-jnp.abs(logit)))
    la_ref[...] = softplus_neg * (-1.0 / GATE_TAU)
    for grp in range(FNO_GROUPS):
        sl = slice(grp * FNO_GROUP_W, (grp + 1) * FNO_GROUP_W)
        ucs = _dot(u[:, sl], cs_ref[...])
        uc_ref[:, sl] = ucs[:, :FNO_GROUP_W].astype(uc_ref.dtype)
        us_ref[:, sl] = ucs[:, FNO_GROUP_W:].astype(us_ref.dtype)


def _premix(layer, x, mod, mod_sel, g_pre, w_t, wa2, ba2, cs_w, fno_dtype, pos=None):
    t = x.shape[0]
    pos_args, pos_specs, pos_tiles = [], [], 0
    if pos is not None:
        pos_args, pos_tiles = [pos[0]], pos[1]
        pos_specs = [pl.BlockSpec(pos[0].shape, lambda i: (0, 0))]
    row = lambda w: pl.BlockSpec((ROW_TILE, w), lambda i: (i, 0))
    lay = lambda *shape: pl.BlockSpec((None,) + shape, lambda i: (layer,) + (0,) * len(shape))
    out = lambda w, dt: jax.ShapeDtypeStruct((t, w), dt)
    return pl.pallas_call(
        functools.partial(_premix_kernel, pos_tiles=pos_tiles),
        grid=(t // ROW_TILE,),
        in_specs=[row(D_MODEL),
                  pl.BlockSpec((None, None, 1, 6 * D_MODEL), lambda i: (layer, mod_sel(i), 0, 0)),
                  lay(1, D_MODEL),
                  pl.BlockSpec((None, D_IN, D_MODEL), lambda i: (layer, 0, 0), pipeline_mode=pl.Buffered(1)),
                  lay(LR_W, 2 * QK_W), lay(1, 2 * QK_W),
                  pl.BlockSpec((FNO_GROUP_W, 2 * FNO_GROUP_W), lambda i: (0, 0)), *pos_specs],
        out_specs=[row(QK_W), row(QK_W), row(V_W), row(V_W), row(2 * QK_W), row(FNO_W), row(FNO_W)],
        out_shape=[out(QK_W, F32), out(QK_W, F32), out(V_W, BF16), out(V_W, BF16),
                   out(2 * QK_W, F32), out(FNO_W, fno_dtype), out(FNO_W, fno_dtype)],
        scratch_shapes=[pltpu.VMEM((D_IN, D_MODEL), BF16)],
        compiler_params=_params(1),
        name="premix",
    )(x, mod, g_pre, w_t, wa2, ba2, cs_w, *pos_args)


def _gla_chunk(r0, forward, q_ref, k_ref, v_ref, la_ref, gap_ref, lvl_ref, st_ref):
    rows = pl.ds(r0, CHUNK)
    lane = lax.broadcasted_iota(jnp.int32, (CHUNK, PAIR_K), 1)
    head0 = lane < GLA_DK
    la = la_ref[rows, :]
    la_hi = la.astype(BF16)
    la_lo = (la - la_hi.astype(F32)).astype(BF16)
    gaps = _dot(gap_ref[...], jnp.concatenate([la_hi, la_lo], axis=1))
    decay = jnp.exp(gaps[:, :PAIR_K] + gaps[:, PAIR_K:])
    blk = lambda j: decay[j * CHUNK:(j + 1) * CHUNK]
    q = q_ref[rows, :]
    k = k_ref[rows, :]
    k_h = (jnp.where(head0, k, 0.0), jnp.where(head0, 0.0, k))
    v = v_ref[rows, :]
    lvl = lvl_ref[...]

    a = jnp.zeros((CHUNK, 2 * CHUNK), F32)
    for lv in range(N_LEVELS + 1):
        if lv == 0:
            ql = q.astype(BF16)
            kl = jnp.concatenate([k_h[0].astype(BF16), k_h[1].astype(BF16)], axis=0)
        else:
            d = blk(1 + lv)
            ql = (q * d).astype(BF16)
            kl = jnp.concatenate([(k_h[0] * d).astype(BF16), (k_h[1] * d).astype(BF16)], axis=0)
        a = jnp.where(lvl == lv, _dot_nt(ql, kl), a)

    q_in = (q * blk(0)).astype(BF16)
    inter = _dot_nt(q_in, _state_per_head(st_ref[...]))
    vlane = lax.broadcasted_iota(jnp.int32, (CHUNK, PAIR_V), 1)
    zero = jnp.zeros_like(v)
    v_blk = jnp.concatenate([jnp.where(vlane < GLA_DV, v, zero),
                             jnp.where(vlane < GLA_DV, zero, v)], axis=0)
    o = inter + _dot(a.astype(BF16), v_blk)

    d_out = blk(1)
    total = blk(0)[CHUNK - 1:CHUNK] if forward else blk(0)[0:1]
    k_out = jnp.concatenate([(k_h[0] * d_out).astype(BF16), (k_h[1] * d_out).astype(BF16)], axis=0)
    st_ref[...] = st_ref[...] * total + _dot_tn(_stack_heads(v), k_out)
    return o


def _state_per_head(st):
    lane = lax.broadcasted_iota(jnp.int32, (1, PAIR_K), 1)
    keep0 = (lane < GLA_DK).astype(BF16)
    st = st.astype(BF16)
    return jnp.concatenate([st * keep0, st * (1.0 - keep0).astype(BF16)], axis=0)


def _stack_heads(v):
    return jnp.concatenate([v[:, :GLA_DV], v[:, GLA_DV:]], axis=0)


def _gla_fast_step(jobs, fwd_refs, bwd_refs):
    refs = [fwd_refs if fw else bwd_refs for _, fw, _ in jobs]
    rows = [pl.ds(r0, CHUNK) for r0, _, _ in jobs]
    lane = lax.broadcasted_iota(jnp.int32, (CHUNK, PAIR_K), 1)
    head0 = lane < GLA_DK
    vlane = lax.broadcasted_iota(jnp.int32, (CHUNK, PAIR_V), 1) < GLA_DV

    la = [r[3][rw, :] for r, rw in zip(refs, rows)]
    la_hi = [x.astype(BF16) for x in la]
    la_lo = [(x - h.astype(F32)).astype(BF16) for x, h in zip(la, la_hi)]
    cum = [_dot(r[4][0:CHUNK, :], jnp.concatenate([h, l], axis=1))
           for r, h, l in zip(refs, la_hi, la_lo)]
    b = [x[:, :PAIR_K] + x[:, PAIR_K:] for x in cum]
    b_end = [x[CHUNK - 1:CHUNK] if fw else x[0:1] for x, (_, fw, _) in zip(b, jobs)]
    q = [r[0][rw, :] for r, rw in zip(refs, rows)]
    k = [r[1][rw, :] for r, rw in zip(refs, rows)]
    v = [r[2][rw, :] for r, rw in zip(refs, rows)]
    qd = [(x * jnp.exp(y)).astype(BF16) for x, y in zip(q, b)]
    kd = [x * jnp.exp(-y) for x, y in zip(k, b)]
    ko = [x * jnp.exp(e - y) for x, y, e in zip(k, b, b_end)]
    total = [jnp.exp(e) for e in b_end]
    kd_cat = [jnp.concatenate([jnp.where(head0, x, 0.0).astype(BF16),
                               jnp.where(head0, 0.0, x).astype(BF16)], axis=0) for x in kd]
    score = [_dot_nt(x, y) for x, y in zip(qd, kd_cat)]
    a = [jnp.where(r[5][...] <= N_LEVELS, s, 0.0).astype(BF16) for r, s in zip(refs, score)]
    v_blk = [jnp.concatenate([jnp.where(vlane, x, jnp.zeros_like(x)),
                              jnp.where(vlane, jnp.zeros_like(x), x)], axis=0) for x in v]
    intra = [_dot(x, y) for x, y in zip(a, v_blk)]
    ko_cat = [jnp.concatenate([jnp.where(head0, x, 0.0).astype(BF16),
                               jnp.where(head0, 0.0, x).astype(BF16)], axis=0) for x in ko]
    upd = [_dot_tn(_stack_heads(x), y) for x, y in zip(v, ko_cat)]

    outs = [None] * len(jobs)
    for scan in sorted({(fw, slot) for _, fw, slot in jobs}):
        st_ref = (fwd_refs if scan[0] else bwd_refs)[6]
        st = st_ref[scan[1]]
        for j, (_, fw, slot) in enumerate(jobs):
            if (fw, slot) != scan:
                continue
            outs[j] = _dot_nt(qd[j], _state_per_head(st)) + intra[j]
            st = st * total[j] + upd[j]
        st_ref[scan[1]] = st
    return outs


def _gla_kernel(*refs, seq_len, n_seq, has_state, emit_state, unroll, par, layer, all_layers, n_alias):
    (q_ref, k_ref, v_ref, laf_ref, lab_ref, g_ref, gh_ref,
     gapf_ref, gapb_ref, lvlf_ref, lvlb_ref) = refs[:11]
    pos = 11
    if has_state:
        s0f_ref, s0b_ref = refs[pos:pos + 2]
        pos += 2
    pos += n_alias
    o_ref = refs[pos]
    pos += 1
    if emit_state:
        sf_ref, sb_ref = refs[pos:pos + 2]
        pos += 2
        if all_layers:
            sf_ref[...] = jnp.zeros_like(sf_ref)
            sb_ref[...] = jnp.zeros_like(sb_ref)
    acc_ref, stf_ref, stb_ref = refs[pos:pos + 3]
    n_chunks = seq_len // CHUNK
    half_iters = n_chunks // (2 * unroll)
    fwd_refs = (q_ref, k_ref, v_ref, laf_ref, gapf_ref, lvlf_ref, stf_ref)
    bwd_refs = (q_ref, k_ref, v_ref, lab_ref, gapb_ref, lvlb_ref, stb_ref)

    def finish(o, rows):
        g = g_ref[rows, :].astype(F32)
        for hd in range(2):
            sl = slice(hd * GLA_DV, (hd + 1) * GLA_DV)
            oh = _rms(o[:, sl], gh_ref[:, sl]) * _silu(g[:, sl])
            o_ref[rows, sl] = oh.astype(o_ref.dtype)

    def seq_body(sg, carry, fast):
        base = sg * (par * seq_len)
        if has_state:
            stf_ref[0] = s0f_ref[...]
            stb_ref[0] = s0b_ref[...]
        else:
            stf_ref[...] = jnp.zeros_like(stf_ref)
            stb_ref[...] = jnp.zeros_like(stb_ref)

        def step(it, second_visit):
            jobs = []
            for slot in range(par):
                for u in range(unroll):
                    cf = it * unroll + u
                    for forward, ci in ((True, cf), (False, n_chunks - 1 - cf)):
                        r0 = base + slot * seq_len + ci * CHUNK
                        jobs.append((pl.multiple_of(r0, CHUNK), forward, slot))
            if fast:
                outs = _gla_fast_step(jobs, fwd_refs, bwd_refs)
            else:
                outs = [_gla_chunk(r0, fw, *(fwd_refs if fw else bwd_refs)[:6],
                                   (stf_ref if fw else stb_ref).at[slot]) for r0, fw, slot in jobs]
            for (r0, _, _), o in zip(jobs, outs):
                rows = pl.ds(r0, CHUNK)
                if second_visit:
                    finish(o + acc_ref[rows, :], rows)
                else:
                    acc_ref[rows, :] = o

        def first(i, c):
            step(i, False)
            return c

        def second(i, c):
            step(half_iters + i, True)
            return c

        lax.fori_loop(0, half_iters, first, 0)
        lax.fori_loop(0, half_iters, second, 0)
        if emit_state:
            for s_out, st_ref in ((sf_ref, stf_ref), (sb_ref, stb_ref)):
                for slot in range(par):
                    for hd in range(2):
                        s_fin = st_ref[slot].T[hd * GLA_DK:(hd + 1) * GLA_DK, :]
                        if all_layers:
                            s_out[sg * par + slot, layer, hd] = s_fin
                        else:
                            s_out[sg * par + slot, hd] = s_fin
        return carry

    weakest = jnp.minimum(jnp.min(laf_ref[...]), jnp.min(lab_ref[...]))
    is_fast = weakest >= -FAST_STEP_LIMIT

    @pl.when(is_fast)
    def _():
        lax.fori_loop(0, n_seq // par, functools.partial(seq_body, fast=True), 0)

    @pl.when(jnp.logical_not(is_fast))
    def _():
        lax.fori_loop(0, n_seq // par, functools.partial(seq_body, fast=False), 0)


def _gla(layer, q, k, v, la, g, g_head, consts, seq_len, n_seq_total, n_seq, s0=None,
         emit_state=False, st_prev=None):
    gapf, gapb, lvlf, lvlb = consts
    rows = seq_len * n_seq
    n_steps = n_seq_total // n_seq
    rspec = lambda w, off=0: pl.BlockSpec((rows, w), lambda i, j: (i, j + off))
    const = lambda a: pl.BlockSpec(a.shape, lambda i, j: (0,) * a.ndim)
    in_specs = [rspec(PAIR_K), rspec(PAIR_K), rspec(PAIR_V), rspec(PAIR_K), rspec(PAIR_K, HEAD_PAIRS),
                rspec(PAIR_V),
                pl.BlockSpec((None, 1, PAIR_V), lambda i, j: (layer, 0, j)),
                const(gapf), const(gapb), const(lvlf), const(lvlb)]
    args = [q, k, v, la, la, g, g_head, gapf, gapb, lvlf, lvlb]
    if s0 is not None:
        sspec = pl.BlockSpec((None, None, None, GLA_DV, PAIR_K), lambda i, j: (i, layer, j, 0, 0))
        in_specs += [sspec, sspec]
        args += list(s0)
    out_specs = [rspec(PAIR_V)]
    out_shape = [jax.ShapeDtypeStruct((n_seq_total * seq_len, V_W), BF16)]
    aliases = {}
    if emit_state:
        if st_prev is None:
            st_spec = pl.BlockSpec((n_seq, DEPTH, 2, GLA_DK, GLA_DV), lambda i, j: (i, 0, j, 0, 0))
        else:
            st_spec = pl.BlockSpec((n_seq, None, 2, GLA_DK, GLA_DV), lambda i, j: (i, layer, j, 0, 0))
            for n, a in enumerate(st_prev):
                aliases[len(args)] = 1 + n
                in_specs.append(pl.BlockSpec(memory_space=pl.ANY))
                args.append(a)
        st_shape = jax.ShapeDtypeStruct((n_seq_total, DEPTH, GLA_HEADS, GLA_DK, GLA_DV), F32)
        out_specs += [st_spec, st_spec]
        out_shape += [st_shape, st_shape]
    unroll = min(SCAN_UNROLL, seq_len // (2 * CHUNK))
    par = 1 if s0 is not None else min(n_seq, SCAN_UNROLL // unroll)
    kern = functools.partial(_gla_kernel, seq_len=seq_len, n_seq=n_seq, has_state=s0 is not None,
                             emit_state=emit_state, unroll=unroll, par=par, layer=layer,
                             all_layers=emit_state and st_prev is None, n_alias=len(aliases))
    return pl.pallas_call(
        kern,
        grid=(n_steps, HEAD_PAIRS),
        in_specs=in_specs,
        out_specs=out_specs,
        out_shape=out_shape,
        input_output_aliases=aliases,
        scratch_shapes=[pltpu.VMEM((rows, PAIR_V), F32), pltpu.VMEM((par, GLA_DV, PAIR_K), F32),
                        pltpu.VMEM((par, GLA_DV, PAIR_K), F32)],
        compiler_params=_params(2),
        name="gla_state" if emit_state else "gla",
    )(*args)


def _fft_tables(n_side):
    c, s = _dft_tables(n_side)
    l1 = np.block([[c, -s], [-s, -c]])
    l2 = np.concatenate([c, s], axis=1)
    idx = np.arange(n_side)
    ang = 2.0 * np.pi * (idx[:, None] * idx[None, :]) / (n_side * n_side)
    lanes = np.ones((1, 1, FNO_GROUP_W), np.float32)
    tc = np.cos(ang).astype(np.float32)[:, :, None] * lanes
    ts = np.sin(ang).astype(np.float32)[:, :, None] * lanes
    return l1, l2, tc, ts


def _fft_kernel(uc_ref, us_ref, l1_ref, l2_ref, tc_ref, ts_ref, o_ref,
                xr_ref, xi_ref, zr_ref, zi_ref, y_ref, *, n_side):
    for r in range(n_side):
        src, dst = pl.ds(r * n_side, n_side), pl.ds(r * FFT_PITCH, n_side)
        xr_ref[dst, :] = uc_ref[src, :]
        xi_ref[dst, :] = us_ref[src, :]

    def stage1(n2, carry):
        col = pl.ds(n2, n_side, stride=FFT_PITCH)
        x = jnp.concatenate([xr_ref[col, :].astype(BF16), xi_ref[col, :].astype(BF16)], axis=0)
        z = _dot(l1_ref[...], x)
        zr, zi = z[:n_side], z[n_side:]
        tc, ts = tc_ref[n2], ts_ref[n2]
        out = pl.ds(pl.multiple_of(n2 * FFT_PITCH, 8), n_side)
        zr_ref[out, :] = zr * tc + zi * ts
        zi_ref[out, :] = zi * tc - zr * ts
        return carry

    def stage2(k1, carry):
        col = pl.ds(k1, n_side, stride=FFT_PITCH)
        z = jnp.concatenate([zr_ref[col, :].astype(BF16), zi_ref[col, :].astype(BF16)], axis=0)
        y_ref[col, :] = _dot(l2_ref[...], z)
        return carry

    lax.fori_loop(0, n_side, stage1, 0, unroll=FFT_UNROLL)
    lax.fori_loop(0, n_side, stage2, 0, unroll=FFT_UNROLL)
    for r in range(n_side):
        o_ref[pl.ds(r * n_side, n_side), :] = y_ref[pl.ds(r * FFT_PITCH, n_side), :]


def _fnet_fft(uc, us, seq_len, n_seq_total):
    n_side = GRID_W
    l1, l2, tc, ts = _fft_tables(n_side)
    as_bf16 = lambda a: jnp.asarray(a, F32).astype(BF16)
    blk = pl.BlockSpec((seq_len, FNO_GROUP_W), lambda b, g: (b, g))
    const = lambda a: pl.BlockSpec(a.shape, lambda b, g: (0,) * a.ndim)
    return pl.pallas_call(
        functools.partial(_fft_kernel, n_side=n_side),
        grid=(n_seq_total, FNO_GROUPS),
        in_specs=[blk, blk, const(l1), const(l2), const(tc), const(ts)],
        out_specs=blk,
        out_shape=jax.ShapeDtypeStruct((n_seq_total * seq_len, FNO_W), F32),
        scratch_shapes=[pltpu.VMEM((n_side * FFT_PITCH, FNO_GROUP_W), F32)] * 5,
        compiler_params=_params(2),
        name="fnet_fft",
    )(uc, us, as_bf16(l1), as_bf16(l2), jnp.asarray(tc), jnp.asarray(ts))


def _post_kernel(*refs, dft_len, pos_tiles):
    if dft_len:
        x_ref, o_ref, uc_ref, us_ref, cos_ref, nsin_ref = refs[:6]
        refs = refs[6:]
        yf = jnp.concatenate(
            [(_dot(cos_ref[...], uc_ref[r0:r0 + dft_len, :]) +
              _dot(nsin_ref[...], us_ref[r0:r0 + dft_len, :])).astype(BF16)
             for r0 in range(0, ROW_TILE, dft_len)], axis=0)
    else:
        x_ref, o_ref, yf_ref = refs[:3]
        refs = refs[3:]
        yf = yf_ref[...].astype(BF16)
    x_in = x_ref[...]
    if pos_tiles:
        x_in = x_in + _pos_tile(refs[0], pl.program_id(0) % pos_tiles)
        refs = refs[1:]
    mod_ref, gpm_ref, gpre_ref, gpost_ref, wout_ref, w1_ref, w2_ref, out_ref = refs
    gt_m = mod_ref[:, 2 * D_MODEL:3 * D_MODEL]
    sh_f = mod_ref[:, 3 * D_MODEL:4 * D_MODEL]
    sc_f = mod_ref[:, 4 * D_MODEL:5 * D_MODEL]
    gt_f = mod_ref[:, 5 * D_MODEL:6 * D_MODEL]
    x_blk, h_blk = [], []
    for r0 in range(0, ROW_TILE, POST_HEAD_ROWS):
        rows = slice(r0, r0 + POST_HEAD_ROWS)
        y = _dot(o_ref[rows, :], wout_ref[0:V_W, :]) + _dot(yf[rows, :], wout_ref[V_W:, :])
        x_blk.append(x_in[rows, :] + gt_m * _rms(y, gpm_ref[...]))
        h_blk.append((_rms(x_blk[-1], gpre_ref[...]) * (1.0 + sc_f) + sh_f).astype(BF16))
    x = jnp.concatenate(x_blk, axis=0)
    h = jnp.concatenate(h_blk, axis=0)
    f = jnp.zeros((ROW_TILE, D_MODEL), F32)
    n_split = 4
    wf = D_FF // n_split
    for j in range(n_split):
        a = jnp.maximum(_dot(h, w1_ref[:, j * wf:(j + 1) * wf]), 0.0)
        f = f + _dot((a * a).astype(BF16), w2_ref[j * wf:(j + 1) * wf, :])
    out_ref[...] = x + gt_f * _rms(f, gpost_ref[...])


def _post(layer, x, o, fno, mod, mod_sel, g_post_mix, g_pre_mlp, g_post_mlp, w_out, w_fc1, w_fc2,
          pos=None):
    t = x.shape[0]
    row = lambda w: pl.BlockSpec((ROW_TILE, w), lambda i: (i, 0))
    lay = lambda *shape: pl.BlockSpec((None,) + shape, lambda i: (layer,) + (0,) * len(shape),
                                      pipeline_mode=pl.Buffered(1))
    dft_len = fno[2].shape[0] if len(fno) == 4 else 0
    fno_specs = [row(FNO_W)] * min(len(fno), 2) + [
        pl.BlockSpec((dft_len, dft_len), lambda i: (0, 0))] * (len(fno) - min(len(fno), 2))
    return pl.pallas_call(
        functools.partial(_post_kernel, dft_len=dft_len, pos_tiles=pos[1] if pos else 0),
        grid=(t // ROW_TILE,),
        in_specs=[row(D_MODEL), row(V_W), *fno_specs,
                  *([pl.BlockSpec(pos[0].shape, lambda i: (0, 0))] if pos else []),
                  pl.BlockSpec((None, None, 1, 6 * D_MODEL), lambda i: (layer, mod_sel(i), 0, 0)),
                  lay(1, D_MODEL), lay(1, D_MODEL), lay(1, D_MODEL),
                  lay(V_W + FNO_W, D_MODEL), lay(D_MODEL, D_FF), lay(D_FF, D_MODEL)],
        out_specs=row(D_MODEL),
        out_shape=jax.ShapeDtypeStruct((t, D_MODEL), F32),
        compiler_params=_params(1),
        name="post",
    )(x, o, *fno, *([pos[0]] if pos else []), mod, g_post_mix, g_pre_mlp, g_post_mlp, w_out, w_fc1, w_fc2)


def _state_to_kernel(s):
    b, l = s.shape[:2]
    st = s.reshape(b, l, HEAD_PAIRS, 2, GLA_DK, GLA_DV)
    return jnp.transpose(st, (0, 1, 2, 5, 3, 4)).reshape(b, l, HEAD_PAIRS, GLA_DV, PAIR_K)


def kernel(x_prompt, x_sample, state_gla_fwd, state_gla_bwd, c, c_ctx, w_in, w_out, w_a2, b_a2,
           g_head, w_ada, b_ada, g_pre_mix, g_post_mix, g_pre_mlp, g_post_mlp, w_fc1, w_fc2):
    batch, seq, _ = x_prompt.shape
    dec_batch, dec_seq, _ = x_sample.shape
    t_ctx = batch * seq
    t_lat = dec_batch * dec_seq
    assert dec_seq == GRID_W * GRID_W and seq % CHUNK == 0 and ROW_TILE % seq == 0
    lat_tiles = dec_seq // ROW_TILE
    mod_sel = (lambda i: 0, lambda i: 1 + i // lat_tiles)

    za = jnp.zeros((DEPTH, LR_RANK, QK_W), F32)
    wa2 = jnp.concatenate([jnp.concatenate([w_a2[:, 0], za], axis=-1),
                           jnp.concatenate([za, w_a2[:, 1]], axis=-1)], axis=1).astype(BF16)
    ba2 = b_a2.reshape(DEPTH, 1, 2 * QK_W)
    w_t = jnp.swapaxes(w_in, 1, 2)
    w_out_b = w_out.astype(BF16)
    w_fc1_b = w_fc1.astype(BF16)
    w_fc2_b = w_fc2.astype(BF16)
    vec = lambda a: a.reshape(DEPTH, 1, -1)
    g_head_r = g_head.reshape(DEPTH, 1, V_W)

    gapf, gapb, lvlf, lvlb = _gap_matrices()
    as_bf16 = lambda a: jnp.asarray(a, F32).astype(BF16)
    consts = (as_bf16(gapf), as_bf16(gapb), jnp.asarray(lvlf), jnp.asarray(lvlb))
    cw, sw = _dft_tables(FNO_GROUP_W)
    cs_w = as_bf16(np.concatenate([cw, sw], axis=1))
    cn, sn = _dft_tables(seq)
    cos_ctx, nsin_ctx = as_bf16(cn), as_bf16(-sn)
    pos = (jnp.asarray(_pos_table()), lat_tiles)

    cvec = jnp.zeros((MOD_ROWS, D_MODEL), F32).at[0].set(c_ctx).at[1:1 + dec_batch].set(c)
    mod = _modulation(cvec, w_ada, b_ada)[:, :1 + dec_batch, None, :]

    xs = [x_prompt.reshape(t_ctx, D_MODEL), x_sample.reshape(t_lat, D_MODEL)]
    s0 = (_state_to_kernel(state_gla_fwd), _state_to_kernel(state_gla_bwd))

    states = None
    for l in range(DEPTH):
        lat_pos = pos if l == 0 else None
        pre = [_premix(l, x, mod, sel, vec(g_pre_mix), w_t, wa2, ba2, cs_w, dt, p)
               for x, sel, dt, p in zip(xs, mod_sel, (BF16, F32), (None, lat_pos))]
        q, k, v, g, la, uc, us = pre[0]
        o_ctx, *states = _gla(l, q, k, v, la, g, g_head_r, consts, seq, batch, 4, emit_state=True,
                              st_prev=states)
        fno_ctx = (uc, us, cos_ctx, nsin_ctx)
        q, k, v, g, la, uc, us = pre[1]
        (o_lat,) = _gla(l, q, k, v, la, g, g_head_r, consts, dec_seq, dec_batch, 1, s0=s0)
        fno_lat = (_fnet_fft(uc, us, dec_seq, dec_batch),)
        xs = [_post(l, x, o, fno, mod, sel, vec(g_post_mix), vec(g_pre_mlp), vec(g_post_mlp),
                    w_out_b, w_fc1_b, w_fc2_b, p)
              for x, o, fno, sel, p in zip(xs, (o_ctx, o_lat), (fno_ctx, fno_lat), mod_sel, (None, lat_pos))]

    y_prompt = xs[0].reshape(batch, seq, D_MODEL)
    y_sample = xs[1].reshape(dec_batch, dec_seq, D_MODEL)
    return (y_prompt, y_sample, states[0], states[1])
```

```python
import functools

import numpy as np
import jax
import jax.numpy as jnp
from jax import lax
from jax.experimental import pallas as pl
from jax.experimental.pallas import tpu as pltpu

D_MODEL = 1024
DEPTH = 4
GRID_W = 64
GLA_HEADS = 4
GLA_DK = 64
GLA_DV = 128
CHUNK = 64
GATE_TAU = 16.0
LR_RANK = 16
FNO_GROUPS = 4
FNO_GROUP_W = 128
D_FF = 4 * D_MODEL
EPS = 1e-6
QK_W = GLA_HEADS * GLA_DK
V_W = GLA_HEADS * GLA_DV
FNO_W = FNO_GROUPS * FNO_GROUP_W
MAIN_W = 2 * QK_W + 2 * V_W
LR_W = 2 * LR_RANK
D_IN = MAIN_W + LR_W + FNO_W
HEAD_PAIRS = GLA_HEADS // 2
PAIR_K = 2 * GLA_DK
PAIR_V = 2 * GLA_DV
N_LEVELS = 6
N_GAPS = N_LEVELS + 2
SCAN_UNROLL = 8
FAST_STEP_LIMIT = 1.0

VMEM_LIMIT = 48 * 1024 * 1024
ROW_TILE = 512
POST_HEAD_ROWS = 128
MOD_ROWS = 8
FFT_PITCH = 72
FFT_UNROLL = 16

BF16 = jnp.bfloat16
F32 = jnp.float32


def _gap_matrices():
    c = CHUNK
    m_all = np.zeros((N_GAPS, c, c), np.float32)
    for p in range(c):
        m_all[0, p, :p + 1] = 1.0
        m_all[1, p, p + 1:] = 1.0
    for lv in range(N_LEVELS):
        m = c >> (lv + 1)
        for p in range(c):
            base = (p // (2 * m)) * 2 * m
            ref = base + m - 1
            if p > ref:
                m_all[2 + lv, p, ref + 1:p + 1] = 1.0
            else:
                m_all[2 + lv, p, p + 1:ref + 1] = 1.0
    fwd = m_all.reshape(N_GAPS * c, c)
    bwd = m_all[:, ::-1, ::-1].reshape(N_GAPS * c, c)
    lvl = np.full((c, c), N_LEVELS + 1, np.int32)
    for t in range(c):
        for s in range(t + 1):
            if s == t:
                lvl[t, s] = 0
            else:
                top = (t ^ s).bit_length() - 1
                lvl[t, s] = N_LEVELS - top
    lvl_f = np.concatenate([lvl, lvl], axis=1)
    lvl_b = np.concatenate([lvl[::-1, ::-1], lvl[::-1, ::-1]], axis=1)
    return fwd, bwd, lvl_f, lvl_b


def _dft_tables(n):
    idx = np.arange(n)
    ang = 2.0 * np.pi * ((idx[:, None] * idx[None, :]) % n) / n
    s = 1.0 / np.sqrt(n)
    return (np.cos(ang) * s).astype(np.float32), (np.sin(ang) * s).astype(np.float32)


def _silu(x):
    return x * (1.0 / (1.0 + jnp.exp(-x)))


def _rms(x, g):
    return x * lax.rsqrt(jnp.mean(x * x, axis=-1, keepdims=True) + EPS) * g


def _dot(a, b):
    return jnp.dot(a, b, preferred_element_type=F32)


def _dot_nt(a, b):
    return lax.dot_general(a, b, (((1,), (1,)), ((), ())), preferred_element_type=F32)


def _dot_tn(a, b):
    return lax.dot_general(a, b, (((0,), (0,)), ((), ())), preferred_element_type=F32)


def _params(n_grid):
    return pltpu.CompilerParams(dimension_semantics=("arbitrary",) * n_grid,
                                vmem_limit_bytes=VMEM_LIMIT)


def _mod_kernel(c_ref, w_ref, b_ref, o_ref):
    s = _silu(c_ref[...])
    w = w_ref[...]
    s_hi, w_hi = s.astype(BF16), w.astype(BF16)
    s_lo = (s - s_hi.astype(F32)).astype(BF16)
    w_lo = (w - w_hi.astype(F32)).astype(BF16)
    main = _dot(jnp.concatenate([s_hi, s_lo], axis=0), w_hi)
    o_ref[...] = main[:MOD_ROWS] + main[MOD_ROWS:] + _dot(s_hi, w_lo) + b_ref[...]


def _modulation(cvec, w_ada, b_ada):
    tn = 3072
    return pl.pallas_call(
        _mod_kernel,
        grid=(DEPTH, 6 * D_MODEL // tn),
        in_specs=[pl.BlockSpec((MOD_ROWS, D_MODEL), lambda l, j: (0, 0)),
                  pl.BlockSpec((None, D_MODEL, tn), lambda l, j: (l, 0, j)),
                  pl.BlockSpec((None, 1, tn), lambda l, j: (l, 0, j))],
        out_specs=pl.BlockSpec((None, MOD_ROWS, tn), lambda l, j: (l, 0, j)),
        out_shape=jax.ShapeDtypeStruct((DEPTH, MOD_ROWS, 6 * D_MODEL), F32),
        compiler_params=_params(2),
        name="modulation",
    )(cvec, w_ada, b_ada.reshape(DEPTH, 1, 6 * D_MODEL))


def _pos_table():
    nf = D_MODEL // 4
    omega = 1.0 / (10000.0 ** (np.arange(nf, dtype=np.float64) / nf))
    idx = np.arange(GRID_W, dtype=np.float64)[:, None]
    return np.concatenate([np.sin(idx * omega), np.cos(idx * omega)], axis=1).astype(np.float32)


def _pos_tile(tab_ref, tile):
    grid_rows = ROW_TILE // GRID_W
    r0 = pl.multiple_of(tile * grid_rows, grid_rows)
    t = tab_ref[pl.ds(r0, grid_rows), :]
    by_row = jnp.concatenate([jnp.broadcast_to(t[i:i + 1], (GRID_W, D_MODEL // 2))
                              for i in range(grid_rows)], axis=0)
    by_col = jnp.concatenate([tab_ref[...]] * grid_rows, axis=0)
    return jnp.concatenate([by_row, by_col], axis=1)


def _premix_kernel(*refs, pos_tiles):
    x_ref, mod_ref, gpre_ref, w_ref, wa2_ref, ba2_ref, cs_ref = refs[:7]
    refs = refs[7:]
    if pos_tiles:
        tab_ref, refs = refs[0], refs[1:]
    q_ref, k_ref, v_ref, g_ref, la_ref, uc_ref, us_ref, wbf_ref = refs

    @pl.when(pl.program_id(0) == 0)
    def _():
        wbf_ref[...] = w_ref[...].astype(BF16)

    x = x_ref[...]
    if pos_tiles:
        x = x + _pos_tile(tab_ref, pl.program_id(0) % pos_tiles)
    sh = mod_ref[:, 0:D_MODEL]
    sc = mod_ref[:, D_MODEL:2 * D_MODEL]
    h = (_rms(x, gpre_ref[...]) * (1.0 + sc) + sh).astype(BF16)
    u = _dot_nt(h, wbf_ref[MAIN_W + LR_W:, :]).astype(BF16)
    lr = _dot_nt(h, wbf_ref[MAIN_W:MAIN_W + LR_W, :]).astype(BF16)
    q_ref[...] = _dot_nt(h, wbf_ref[0:QK_W, :]) * (GLA_DK ** -0.5)
    k_ref[...] = _dot_nt(h, wbf_ref[QK_W:2 * QK_W, :])
    v_ref[...] = _dot_nt(h, wbf_ref[2 * QK_W:2 * QK_W + V_W, :]).astype(BF16)
    g_ref[...] = _dot_nt(h, wbf_ref[2 * QK_W + V_W:MAIN_W, :]).astype(BF16)
    logit = _dot(lr, wa2_ref[...]) + ba2_ref[...]
    softplus_neg = jnp.maximum(-logit, 0.0) + jnp.log(1.0 + jnp.exp(-jnp.abs(logit)))
    la_ref[...] = softplus_neg * (-1.0 / GATE_TAU)
    for grp in range(FNO_GROUPS):
        sl = slice(grp * FNO_GROUP_W, (grp + 1) * FNO_GROUP_W)
        ucs = _dot(u[:, sl], cs_ref[...])
        uc_ref[:, sl] = ucs[:, :FNO_GROUP_W].astype(uc_ref.dtype)
        us_ref[:, sl] = ucs[:, FNO_GROUP_W:].astype(us_ref.dtype)


def _premix(layer, x, mod, mod_sel, g_pre, w_t, wa2, ba2, cs_w, fno_dtype, pos=None):
    t = x.shape[0]
    pos_args, pos_specs, pos_tiles = [], [], 0
    if pos is not None:
        pos_args, pos_tiles = [pos[0]], pos[1]
        pos_specs = [pl.BlockSpec(pos[0].shape, lambda i: (0, 0))]
    row = lambda w: pl.BlockSpec((ROW_TILE, w), lambda i: (i, 0))
    lay = lambda *shape: pl.BlockSpec((None,) + shape, lambda i: (layer,) + (0,) * len(shape))
    out = lambda w, dt: jax.ShapeDtypeStruct((t, w), dt)
    return pl.pallas_call(
        functools.partial(_premix_kernel, pos_tiles=pos_tiles),
        grid=(t // ROW_TILE,),
        in_specs=[row(D_MODEL),
                  pl.BlockSpec((None, None, 1, 6 * D_MODEL), lambda i: (layer, mod_sel(i), 0, 0)),
                  lay(1, D_MODEL),
                  pl.BlockSpec((None, D_IN, D_MODEL), lambda i: (layer, 0, 0), pipeline_mode=pl.Buffered(1)),
                  lay(LR_W, 2 * QK_W), lay(1, 2 * QK_W),
                  pl.BlockSpec((FNO_GROUP_W, 2 * FNO_GROUP_W), lambda i: (0, 0)), *pos_specs],
        out_specs=[row(QK_W), row(QK_W), row(V_W), row(V_W), row(2 * QK_W), row(FNO_W), row(FNO_W)],
        out_shape=[out(QK_W, F32), out(QK_W, F32), out(V_W, BF16), out(V_W, BF16),
                   out(2 * QK_W, F32), out(FNO_W, fno_dtype), out(FNO_W, fno_dtype)],
        scratch_shapes=[pltpu.VMEM((D_IN, D_MODEL), BF16)],
        compiler_params=_params(1),
        name="premix",
    )(x, mod, g_pre, w_t, wa2, ba2, cs_w, *pos_args)


def _gla_chunk(r0, forward, q_ref, k_ref, v_ref, la_ref, gap_ref, lvl_ref, st_ref):
    rows = pl.ds(r0, CHUNK)
    lane = lax.broadcasted_iota(jnp.int32, (CHUNK, PAIR_K), 1)
    head0 = lane < GLA_DK
    la = la_ref[rows, :]
    la_hi = la.astype(BF16)
    la_lo = (la - la_hi.astype(F32)).astype(BF16)
    gaps = _dot(gap_ref[...], jnp.concatenate([la_hi, la_lo], axis=1))
    decay = jnp.exp(gaps[:, :PAIR_K] + gaps[:, PAIR_K:])
    blk = lambda j: decay[j * CHUNK:(j + 1) * CHUNK]
    q = q_ref[rows, :]
    k = k_ref[rows, :]
    k_h = (jnp.where(head0, k, 0.0), jnp.where(head0, 0.0, k))
    v = v_ref[rows, :]
    lvl = lvl_ref[...]

    a = jnp.zeros((CHUNK, 2 * CHUNK), F32)
    for lv in range(N_LEVELS + 1):
        if lv == 0:
            ql = q.astype(BF16)
            kl = jnp.concatenate([k_h[0].astype(BF16), k_h[1].astype(BF16)], axis=0)
        else:
            d = blk(1 + lv)
            ql = (q * d).astype(BF16)
            kl = jnp.concatenate([(k_h[0] * d).astype(BF16), (k_h[1] * d).astype(BF16)], axis=0)
        a = jnp.where(lvl == lv, _dot_nt(ql, kl), a)

    q_in = (q * blk(0)).astype(BF16)
    inter = _dot_nt(q_in, _state_per_head(st_ref[...]))
    vlane = lax.broadcasted_iota(jnp.int32, (CHUNK, PAIR_V), 1)
    zero = jnp.zeros_like(v)
    v_blk = jnp.concatenate([jnp.where(vlane < GLA_DV, v, zero),
                             jnp.where(vlane < GLA_DV, zero, v)], axis=0)
    o = inter + _dot(a.astype(BF16), v_blk)

    d_out = blk(1)
    total = blk(0)[CHUNK - 1:CHUNK] if forward else blk(0)[0:1]
    k_out = jnp.concatenate([(k_h[0] * d_out).astype(BF16), (k_h[1] * d_out).astype(BF16)], axis=0)
    st_ref[...] = st_ref[...] * total + _dot_tn(_stack_heads(v), k_out)
    return o


def _state_per_head(st):
    lane = lax.broadcasted_iota(jnp.int32, (1, PAIR_K), 1)
    keep0 = (lane < GLA_DK).astype(BF16)
    st = st.astype(BF16)
    return jnp.concatenate([st * keep0, st * (1.0 - keep0).astype(BF16)], axis=0)


def _stack_heads(v):
    return jnp.concatenate([v[:, :GLA_DV], v[:, GLA_DV:]], axis=0)


def _gla_fast_step(jobs, fwd_refs, bwd_refs):
    refs = [fwd_refs if fw else bwd_refs for _, fw, _ in jobs]
    rows = [pl.ds(r0, CHUNK) for r0, _, _ in jobs]
    lane = lax.broadcasted_iota(jnp.int32, (CHUNK, PAIR_K), 1)
    head0 = lane < GLA_DK
    vlane = lax.broadcasted_iota(jnp.int32, (CHUNK, PAIR_V), 1) < GLA_DV

    la = [r[3][rw, :] for r, rw in zip(refs, rows)]
    la_hi = [x.astype(BF16) for x in la]
    la_lo = [(x - h.astype(F32)).astype(BF16) for x, h in zip(la, la_hi)]
    cum = [_dot(r[4][0:CHUNK, :], jnp.concatenate([h, l], axis=1))
           for r, h, l in zip(refs, la_hi, la_lo)]
    b = [x[:, :PAIR_K] + x[:, PAIR_K:] for x in cum]
    b_end = [x[CHUNK - 1:CHUNK] if fw else x[0:1] for x, (_, fw, _) in zip(b, jobs)]
    q = [r[0][rw, :] for r, rw in zip(refs, rows)]
    k = [r[1][rw, :] for r, rw in zip(refs, rows)]
    v = [r[2][rw, :] for r, rw in zip(refs, rows)]
    qd = [(x * jnp.exp(y)).astype(BF16) for x, y in zip(q, b)]
    kd = [x * jnp.exp(-y) for x, y in zip(k, b)]
    ko = [x * jnp.exp(e - y) for x, y, e in zip(k, b, b_end)]
    total = [jnp.exp(e) for e in b_end]
    kd_cat = [jnp.concatenate([jnp.where(head0, x, 0.0).astype(BF16),
                               jnp.where(head0, 0.0, x).astype(BF16)], axis=0) for x in kd]
    score = [_dot_nt(x, y) for x, y in zip(qd, kd_cat)]
    a = [jnp.where(r[5][...] <= N_LEVELS, s, 0.0).astype(BF16) for r, s in zip(refs, score)]
    v_blk = [jnp.concatenate([jnp.where(vlane, x, jnp.zeros_like(x)),
                              jnp.where(vlane, jnp.zeros_like(x), x)], axis=0) for x in v]
    intra = [_dot(x, y) for x, y in zip(a, v_blk)]
    ko_cat = [jnp.concatenate([jnp.where(head0, x, 0.0).astype(BF16),
                               jnp.where(head0, 0.0, x).astype(BF16)], axis=0) for x in ko]
    upd = [_dot_tn(_stack_heads(x), y) for x, y in zip(v, ko_cat)]

    outs = [None] * len(jobs)
    for scan in sorted({(fw, slot) for _, fw, slot in jobs}):
        st_ref = (fwd_refs if scan[0] else bwd_refs)[6]
        st = st_ref[scan[1]]
        for j, (_, fw, slot) in enumerate(jobs):
            if (fw, slot) != scan:
                continue
            outs[j] = _dot_nt(qd[j], _state_per_head(st)) + intra[j]
            st = st * total[j] + upd[j]
        st_ref[scan[1]] = st
    return outs


def _gla_kernel(*refs, seq_len, n_seq, has_state, emit_state, unroll, par, layer, all_layers, n_alias):
    (q_ref, k_ref, v_ref, laf_ref, lab_ref,
     gapf_ref, gapb_ref, lvlf_ref, lvlb_ref) = refs[:9]
    pos = 9
    if has_state:
        s0f_ref, s0b_ref = refs[pos:pos + 2]
        pos += 2
    pos += n_alias
    o_ref = refs[pos]
    pos += 1
    if emit_state:
        sf_ref, sb_ref = refs[pos:pos + 2]
        pos += 2
        if all_layers:
            sf_ref[...] = jnp.zeros_like(sf_ref)
            sb_ref[...] = jnp.zeros_like(sb_ref)
    acc_ref, stf_ref, stb_ref = refs[pos:pos + 3]
    n_chunks = seq_len // CHUNK
    half_iters = n_chunks // (2 * unroll)
    fwd_refs = (q_ref, k_ref, v_ref, laf_ref, gapf_ref, lvlf_ref, stf_ref)
    bwd_refs = (q_ref, k_ref, v_ref, lab_ref, gapb_ref, lvlb_ref, stb_ref)

    def finish(o, rows):
        o_ref[rows, :] = o.astype(o_ref.dtype)

    def seq_body(sg, carry, fast):
        base = sg * (par * seq_len)
        if has_state:
            stf_ref[0] = s0f_ref[...]
            stb_ref[0] = s0b_ref[...]
        else:
            stf_ref[...] = jnp.zeros_like(stf_ref)
            stb_ref[...] = jnp.zeros_like(stb_ref)

        def step(it, second_visit):
            jobs = []
            for slot in range(par):
                for u in range(unroll):
                    cf = it * unroll + u
                    for forward, ci in ((True, cf), (False, n_chunks - 1 - cf)):
                        r0 = base + slot * seq_len + ci * CHUNK
                        jobs.append((pl.multiple_of(r0, CHUNK), forward, slot))
            if fast:
                outs = _gla_fast_step(jobs, fwd_refs, bwd_refs)
            else:
                outs = [_gla_chunk(r0, fw, *(fwd_refs if fw else bwd_refs)[:6],
                                   (stf_ref if fw else stb_ref).at[slot]) for r0, fw, slot in jobs]
            for (r0, _, _), o in zip(jobs, outs):
                rows = pl.ds(r0, CHUNK)
                if second_visit:
                    finish(o + acc_ref[rows, :], rows)
                else:
                    acc_ref[rows, :] = o

        def first(i, c):
            step(i, False)
            return c

        def second(i, c):
            step(half_iters + i, True)
            return c

        lax.fori_loop(0, half_iters, first, 0)
        lax.fori_loop(0, half_iters, second, 0)
        if emit_state:
            for s_out, st_ref in ((sf_ref, stf_ref), (sb_ref, stb_ref)):
                for slot in range(par):
                    for hd in range(2):
                        s_fin = st_ref[slot].T[hd * GLA_DK:(hd + 1) * GLA_DK, :]
                        if all_layers:
                            s_out[sg * par + slot, layer, hd] = s_fin
                        else:
                            s_out[sg * par + slot, hd] = s_fin
        return carry

    weakest = jnp.minimum(jnp.min(laf_ref[...]), jnp.min(lab_ref[...]))
    is_fast = weakest >= -FAST_STEP_LIMIT

    @pl.when(is_fast)
    def _():
        lax.fori_loop(0, n_seq // par, functools.partial(seq_body, fast=True), 0)

    @pl.when(jnp.logical_not(is_fast))
    def _():
        lax.fori_loop(0, n_seq // par, functools.partial(seq_body, fast=False), 0)


def _gla(layer, q, k, v, la, consts, seq_len, n_seq_total, n_seq, s0=None,
         emit_state=False, st_prev=None):
    gapf, gapb, lvlf, lvlb = consts
    rows = seq_len * n_seq
    n_steps = n_seq_total // n_seq
    rspec = lambda w, off=0: pl.BlockSpec((rows, w), lambda i, j: (i, j + off))
    const = lambda a: pl.BlockSpec(a.shape, lambda i, j: (0,) * a.ndim)
    in_specs = [rspec(PAIR_K), rspec(PAIR_K), rspec(PAIR_V), rspec(PAIR_K), rspec(PAIR_K, HEAD_PAIRS),
                const(gapf), const(gapb), const(lvlf), const(lvlb)]
    args = [q, k, v, la, la, gapf, gapb, lvlf, lvlb]
    if s0 is not None:
        sspec = pl.BlockSpec((None, None, None, GLA_DV, PAIR_K), lambda i, j: (i, layer, j, 0, 0))
        in_specs += [sspec, sspec]
        args += list(s0)
    out_specs = [rspec(PAIR_V)]
    out_shape = [jax.ShapeDtypeStruct((n_seq_total * seq_len, V_W), BF16)]
    aliases = {}
    if emit_state:
        if st_prev is None:
            st_spec = pl.BlockSpec((n_seq, DEPTH, 2, GLA_DK, GLA_DV), lambda i, j: (i, 0, j, 0, 0))
        else:
            st_spec = pl.BlockSpec((n_seq, None, 2, GLA_DK, GLA_DV), lambda i, j: (i, layer, j, 0, 0))
            for n, a in enumerate(st_prev):
                aliases[len(args)] = 1 + n
                in_specs.append(pl.BlockSpec(memory_space=pl.ANY))
                args.append(a)
        st_shape = jax.ShapeDtypeStruct((n_seq_total, DEPTH, GLA_HEADS, GLA_DK, GLA_DV), F32)
        out_specs += [st_spec, st_spec]
        out_shape += [st_shape, st_shape]
    unroll = min(SCAN_UNROLL, seq_len // (2 * CHUNK))
    par = 1 if s0 is not None else min(n_seq, SCAN_UNROLL // unroll)
    kern = functools.partial(_gla_kernel, seq_len=seq_len, n_seq=n_seq, has_state=s0 is not None,
                             emit_state=emit_state, unroll=unroll, par=par, layer=layer,
                             all_layers=emit_state and st_prev is None, n_alias=len(aliases))
    return pl.pallas_call(
        kern,
        grid=(n_steps, HEAD_PAIRS),
        in_specs=in_specs,
        out_specs=out_specs,
        out_shape=out_shape,
        input_output_aliases=aliases,
        scratch_shapes=[pltpu.VMEM((rows, PAIR_V), F32), pltpu.VMEM((par, GLA_DV, PAIR_K), F32),
                        pltpu.VMEM((par, GLA_DV, PAIR_K), F32)],
        compiler_params=_params(2),
        name="gla_state" if emit_state else "gla",
    )(*args)


def _fft_tables(n_side):
    c, s = _dft_tables(n_side)
    l1 = np.block([[c, -s], [-s, -c]])
    l2 = np.concatenate([c, s], axis=1)
    idx = np.arange(n_side)
    ang = 2.0 * np.pi * (idx[:, None] * idx[None, :]) / (n_side * n_side)
    lanes = np.ones((1, 1, FNO_GROUP_W), np.float32)
    tc = np.cos(ang).astype(np.float32)[:, :, None] * lanes
    ts = np.sin(ang).astype(np.float32)[:, :, None] * lanes
    return l1, l2, tc, ts


def _fft_kernel(uc_ref, us_ref, l1_ref, l2_ref, tc_ref, ts_ref, o_ref,
                xr_ref, xi_ref, zr_ref, zi_ref, y_ref, *, n_side):
    for r in range(n_side):
        src, dst = pl.ds(r * n_side, n_side), pl.ds(r * FFT_PITCH, n_side)
        xr_ref[dst, :] = uc_ref[src, :]
        xi_ref[dst, :] = us_ref[src, :]

    def stage1(n2, carry):
        col = pl.ds(n2, n_side, stride=FFT_PITCH)
        x = jnp.concatenate([xr_ref[col, :].astype(BF16), xi_ref[col, :].astype(BF16)], axis=0)
        z = _dot(l1_ref[...], x)
        zr, zi = z[:n_side], z[n_side:]
        tc, ts = tc_ref[n2], ts_ref[n2]
        out = pl.ds(pl.multiple_of(n2 * FFT_PITCH, 8), n_side)
        zr_ref[out, :] = zr * tc + zi * ts
        zi_ref[out, :] = zi * tc - zr * ts
        return carry

    def stage2(k1, carry):
        col = pl.ds(k1, n_side, stride=FFT_PITCH)
        z = jnp.concatenate([zr_ref[col, :].astype(BF16), zi_ref[col, :].astype(BF16)], axis=0)
        y_ref[col, :] = _dot(l2_ref[...], z)
        return carry

    lax.fori_loop(0, n_side, stage1, 0, unroll=FFT_UNROLL)
    lax.fori_loop(0, n_side, stage2, 0, unroll=FFT_UNROLL)
    for r in range(n_side):
        o_ref[pl.ds(r * n_side, n_side), :] = y_ref[pl.ds(r * FFT_PITCH, n_side), :]


def _fnet_fft(uc, us, seq_len, n_seq_total):
    n_side = GRID_W
    l1, l2, tc, ts = _fft_tables(n_side)
    as_bf16 = lambda a: jnp.asarray(a, F32).astype(BF16)
    blk = pl.BlockSpec((seq_len, FNO_GROUP_W), lambda b, g: (b, g))
    const = lambda a: pl.BlockSpec(a.shape, lambda b, g: (0,) * a.ndim)
    return pl.pallas_call(
        functools.partial(_fft_kernel, n_side=n_side),
        grid=(n_seq_total, FNO_GROUPS),
        in_specs=[blk, blk, const(l1), const(l2), const(tc), const(ts)],
        out_specs=blk,
        out_shape=jax.ShapeDtypeStruct((n_seq_total * seq_len, FNO_W), F32),
        scratch_shapes=[pltpu.VMEM((n_side * FFT_PITCH, FNO_GROUP_W), F32)] * 5,
        compiler_params=_params(2),
        name="fnet_fft",
    )(uc, us, as_bf16(l1), as_bf16(l2), jnp.asarray(tc), jnp.asarray(ts))


def _post_kernel(*refs, dft_len, pos_tiles):
    x_ref, o_ref, g_ref, gh_ref = refs[:4]
    refs = refs[4:]
    if dft_len:
        uc_ref, us_ref, cos_ref, nsin_ref = refs[:4]
        refs = refs[4:]
        yf = jnp.concatenate(
            [(_dot(cos_ref[...], uc_ref[r0:r0 + dft_len, :]) +
              _dot(nsin_ref[...], us_ref[r0:r0 + dft_len, :])).astype(BF16)
             for r0 in range(0, ROW_TILE, dft_len)], axis=0)
    else:
        yf_ref, refs = refs[0], refs[1:]
        yf = yf_ref[...].astype(BF16)
    x_in = x_ref[...]
    if pos_tiles:
        x_in = x_in + _pos_tile(refs[0], pl.program_id(0) % pos_tiles)
        refs = refs[1:]
    mod_ref, gpm_ref, gpre_ref, gpost_ref, wout_ref, w1_ref, w2_ref, out_ref = refs
    gt_m = mod_ref[:, 2 * D_MODEL:3 * D_MODEL]
    sh_f = mod_ref[:, 3 * D_MODEL:4 * D_MODEL]
    sc_f = mod_ref[:, 4 * D_MODEL:5 * D_MODEL]
    gt_f = mod_ref[:, 5 * D_MODEL:6 * D_MODEL]
    x_blk, h_blk = [], []
    for r0 in range(0, ROW_TILE, POST_HEAD_ROWS):
        rows = slice(r0, r0 + POST_HEAD_ROWS)
        o_sum = o_ref[rows, :].astype(F32)
        gate = g_ref[rows, :].astype(F32)
        o = jnp.concatenate(
            [_rms(o_sum[:, sl], gh_ref[:, sl]) * _silu(gate[:, sl])
             for sl in (slice(hd * GLA_DV, (hd + 1) * GLA_DV) for hd in range(GLA_HEADS))], axis=1)
        y = _dot(o.astype(BF16), wout_ref[0:V_W, :]) + _dot(yf[rows, :], wout_ref[V_W:, :])
        x_blk.append(x_in[rows, :] + gt_m * _rms(y, gpm_ref[...]))
        h_blk.append((_rms(x_blk[-1], gpre_ref[...]) * (1.0 + sc_f) + sh_f).astype(BF16))
    x = jnp.concatenate(x_blk, axis=0)
    h = jnp.concatenate(h_blk, axis=0)
    f = jnp.zeros((ROW_TILE, D_MODEL), F32)
    n_split = 4
    wf = D_FF // n_split
    for j in range(n_split):
        a = jnp.maximum(_dot(h, w1_ref[:, j * wf:(j + 1) * wf]), 0.0)
        f = f + _dot((a * a).astype(BF16), w2_ref[j * wf:(j + 1) * wf, :])
    out_ref[...] = x + gt_f * _rms(f, gpost_ref[...])


def _post(layer, x, o, g, g_head, fno, mod, mod_sel, g_post_mix, g_pre_mlp, g_post_mlp, w_out, w_fc1, w_fc2,
          pos=None):
    t = x.shape[0]
    row = lambda w: pl.BlockSpec((ROW_TILE, w), lambda i: (i, 0))
    lay = lambda *shape: pl.BlockSpec((None,) + shape, lambda i: (layer,) + (0,) * len(shape),
                                      pipeline_mode=pl.Buffered(1))
    dft_len = fno[2].shape[0] if len(fno) == 4 else 0
    fno_specs = [row(FNO_W)] * min(len(fno), 2) + [
        pl.BlockSpec((dft_len, dft_len), lambda i: (0, 0))] * (len(fno) - min(len(fno), 2))
    return pl.pallas_call(
        functools.partial(_post_kernel, dft_len=dft_len, pos_tiles=pos[1] if pos else 0),
        grid=(t // ROW_TILE,),
        in_specs=[row(D_MODEL), row(V_W), row(V_W), lay(1, V_W), *fno_specs,
                  *([pl.BlockSpec(pos[0].shape, lambda i: (0, 0))] if pos else []),
                  pl.BlockSpec((None, None, 1, 6 * D_MODEL), lambda i: (layer, mod_sel(i), 0, 0)),
                  lay(1, D_MODEL), lay(1, D_MODEL), lay(1, D_MODEL),
                  lay(V_W + FNO_W, D_MODEL), lay(D_MODEL, D_FF), lay(D_FF, D_MODEL)],
        out_specs=row(D_MODEL),
        out_shape=jax.ShapeDtypeStruct((t, D_MODEL), F32),
        compiler_params=_params(1),
        name="post",
    )(x, o, g, g_head, *fno, *([pos[0]] if pos else []), mod, g_post_mix, g_pre_mlp, g_post_mlp,
      w_out, w_fc1, w_fc2)


def _state_to_kernel(s):
    b, l = s.shape[:2]
    st = s.reshape(b, l, HEAD_PAIRS, 2, GLA_DK, GLA_DV)
    return jnp.transpose(st, (0, 1, 2, 5, 3, 4)).reshape(b, l, HEAD_PAIRS, GLA_DV, PAIR_K)


def kernel(x_prompt, x_sample, state_gla_fwd, state_gla_bwd, c, c_ctx, w_in, w_out, w_a2, b_a2,
           g_head, w_ada, b_ada, g_pre_mix, g_post_mix, g_pre_mlp, g_post_mlp, w_fc1, w_fc2):
    batch, seq, _ = x_prompt.shape
    dec_batch, dec_seq, _ = x_sample.shape
    t_ctx = batch * seq
    t_lat = dec_batch * dec_seq
    assert dec_seq == GRID_W * GRID_W and seq % CHUNK == 0 and ROW_TILE % seq == 0
    lat_tiles = dec_seq // ROW_TILE
    mod_sel = (lambda i: 0, lambda i: 1 + i // lat_tiles)

    za = jnp.zeros((DEPTH, LR_RANK, QK_W), F32)
    wa2 = jnp.concatenate([jnp.concatenate([w_a2[:, 0], za], axis=-1),
                           jnp.concatenate([za, w_a2[:, 1]], axis=-1)], axis=1).astype(BF16)
    ba2 = b_a2.reshape(DEPTH, 1, 2 * QK_W)
    w_t = jnp.swapaxes(w_in, 1, 2)
    w_out_b = w_out.astype(BF16)
    w_fc1_b = w_fc1.astype(BF16)
    w_fc2_b = w_fc2.astype(BF16)
    vec = lambda a: a.reshape(DEPTH, 1, -1)
    g_head_r = g_head.reshape(DEPTH, 1, V_W)

    gapf, gapb, lvlf, lvlb = _gap_matrices()
    as_bf16 = lambda a: jnp.asarray(a, F32).astype(BF16)
    consts = (as_bf16(gapf), as_bf16(gapb), jnp.asarray(lvlf), jnp.asarray(lvlb))
    cw, sw = _dft_tables(FNO_GROUP_W)
    cs_w = as_bf16(np.concatenate([cw, sw], axis=1))
    cn, sn = _dft_tables(seq)
    cos_ctx, nsin_ctx = as_bf16(cn), as_bf16(-sn)
    pos = (jnp.asarray(_pos_table()), lat_tiles)

    cvec = jnp.zeros((MOD_ROWS, D_MODEL), F32).at[0].set(c_ctx).at[1:1 + dec_batch].set(c)
    mod = _modulation(cvec, w_ada, b_ada)[:, :1 + dec_batch, None, :]

    xs = [x_prompt.reshape(t_ctx, D_MODEL), x_sample.reshape(t_lat, D_MODEL)]
    s0 = (_state_to_kernel(state_gla_fwd), _state_to_kernel(state_gla_bwd))

    states = None
    for l in range(DEPTH):
        lat_pos = pos if l == 0 else None
        pre = [_premix(l, x, mod, sel, vec(g_pre_mix), w_t, wa2, ba2, cs_w, dt, p)
               for x, sel, dt, p in zip(xs, mod_sel, (BF16, F32), (None, lat_pos))]
        q, k, v, g_ctx, la, uc, us = pre[0]
        o_ctx, *states = _gla(l, q, k, v, la, consts, seq, batch, 8, emit_state=True, st_prev=states)
        fno_ctx = (uc, us, cos_ctx, nsin_ctx)
        q, k, v, g_lat, la, uc, us = pre[1]
        (o_lat,) = _gla(l, q, k, v, la, consts, dec_seq, dec_batch, 1, s0=s0)
        fno_lat = (_fnet_fft(uc, us, dec_seq, dec_batch),)
        xs = [_post(l, x, o, g, g_head_r, fno, mod, sel, vec(g_post_mix), vec(g_pre_mlp), vec(g_post_mlp),
                    w_out_b, w_fc1_b, w_fc2_b, p)
              for x, o, g, fno, sel, p in zip(xs, (o_ctx, o_lat), (g_ctx, g_lat), (fno_ctx, fno_lat),
                                              mod_sel, (None, lat_pos))]

    y_prompt = xs[0].reshape(batch, seq, D_MODEL)
    y_sample = xs[1].reshape(dec_batch, dec_seq, D_MODEL)
    return (y_prompt, y_sample, states[0], states[1])
```

```python
import functools

import numpy as np
import jax
import jax.numpy as jnp
from jax import lax
from jax.experimental import pallas as pl
from jax.experimental.pallas import tpu as pltpu

D_MODEL = 1024
DEPTH = 4
GRID_W = 64
GLA_HEADS = 4
GLA_DK = 64
GLA_DV = 128
CHUNK = 64
GATE_TAU = 16.0
LR_RANK = 16
FNO_GROUPS = 4
FNO_GROUP_W = 128
D_FF = 4 * D_MODEL
EPS = 1e-6
QK_W = GLA_HEADS * GLA_DK
V_W = GLA_HEADS * GLA_DV
FNO_W = FNO_GROUPS * FNO_GROUP_W
MAIN_W = 2 * QK_W + 2 * V_W
LR_W = 2 * LR_RANK
D_IN = MAIN_W + LR_W + FNO_W
HEAD_PAIRS = GLA_HEADS // 2
PAIR_K = 2 * GLA_DK
PAIR_V = 2 * GLA_DV
N_LEVELS = 6
N_GAPS = N_LEVELS + 2
SCAN_UNROLL = 8
FAST_STEP_LIMIT = 1.0

VMEM_LIMIT = 48 * 1024 * 1024
ROW_TILE = 512
PRE_STEP_TILES = 2
POST_STEP_TILES = 2
POST_VMEM_LIMIT = 58 * 1024 * 1024
POST_HEAD_ROWS = 128
MOD_ROWS = 8
FFT_PITCH = 72
FFT_UNROLL = 16

BF16 = jnp.bfloat16
F32 = jnp.float32


def _gap_matrices():
    c = CHUNK
    m_all = np.zeros((N_GAPS, c, c), np.float32)
    for p in range(c):
        m_all[0, p, :p + 1] = 1.0
        m_all[1, p, p + 1:] = 1.0
    for lv in range(N_LEVELS):
        m = c >> (lv + 1)
        for p in range(c):
            base = (p // (2 * m)) * 2 * m
            ref = base + m - 1
            if p > ref:
                m_all[2 + lv, p, ref + 1:p + 1] = 1.0
            else:
                m_all[2 + lv, p, p + 1:ref + 1] = 1.0
    fwd = m_all.reshape(N_GAPS * c, c)
    bwd = m_all[:, ::-1, ::-1].reshape(N_GAPS * c, c)
    lvl = np.full((c, c), N_LEVELS + 1, np.int32)
    for t in range(c):
        for s in range(t + 1):
            if s == t:
                lvl[t, s] = 0
            else:
                top = (t ^ s).bit_length() - 1
                lvl[t, s] = N_LEVELS - top
    lvl_f = np.concatenate([lvl, lvl], axis=1)
    lvl_b = np.concatenate([lvl[::-1, ::-1], lvl[::-1, ::-1]], axis=1)
    return fwd, bwd, lvl_f, lvl_b


def _dft_tables(n):
    idx = np.arange(n)
    ang = 2.0 * np.pi * ((idx[:, None] * idx[None, :]) % n) / n
    s = 1.0 / np.sqrt(n)
    return (np.cos(ang) * s).astype(np.float32), (np.sin(ang) * s).astype(np.float32)


def _silu(x):
    return x * (1.0 / (1.0 + jnp.exp(-x)))


def _rms(x, g):
    return x * lax.rsqrt(jnp.mean(x * x, axis=-1, keepdims=True) + EPS) * g


def _dot(a, b):
    return jnp.dot(a, b, preferred_element_type=F32)


def _dot_nt(a, b):
    return lax.dot_general(a, b, (((1,), (1,)), ((), ())), preferred_element_type=F32)


def _dot_tn(a, b):
    return lax.dot_general(a, b, (((0,), (0,)), ((), ())), preferred_element_type=F32)


def _params(n_grid):
    return pltpu.CompilerParams(dimension_semantics=("arbitrary",) * n_grid,
                                vmem_limit_bytes=VMEM_LIMIT)


def _mod_kernel(c_ref, w_ref, b_ref, o_ref):
    s = _silu(c_ref[...])
    w = w_ref[...]
    s_hi, w_hi = s.astype(BF16), w.astype(BF16)
    s_lo = (s - s_hi.astype(F32)).astype(BF16)
    w_lo = (w - w_hi.astype(F32)).astype(BF16)
    main = _dot(jnp.concatenate([s_hi, s_lo], axis=0), w_hi)
    o_ref[...] = main[:MOD_ROWS] + main[MOD_ROWS:] + _dot(s_hi, w_lo) + b_ref[...]


def _modulation(cvec, w_ada, b_ada):
    tn = 3072
    return pl.pallas_call(
        _mod_kernel,
        grid=(DEPTH, 6 * D_MODEL // tn),
        in_specs=[pl.BlockSpec((MOD_ROWS, D_MODEL), lambda l, j: (0, 0)),
                  pl.BlockSpec((None, D_MODEL, tn), lambda l, j: (l, 0, j)),
                  pl.BlockSpec((None, 1, tn), lambda l, j: (l, 0, j))],
        out_specs=pl.BlockSpec((None, MOD_ROWS, tn), lambda l, j: (l, 0, j)),
        out_shape=jax.ShapeDtypeStruct((DEPTH, MOD_ROWS, 6 * D_MODEL), F32),
        compiler_params=_params(2),
        name="modulation",
    )(cvec, w_ada, b_ada.reshape(DEPTH, 1, 6 * D_MODEL))


def _pos_table():
    nf = D_MODEL // 4
    omega = 1.0 / (10000.0 ** (np.arange(nf, dtype=np.float64) / nf))
    idx = np.arange(GRID_W, dtype=np.float64)[:, None]
    return np.concatenate([np.sin(idx * omega), np.cos(idx * omega)], axis=1).astype(np.float32)


def _pos_tile(tab_ref, tile):
    grid_rows = ROW_TILE // GRID_W
    r0 = pl.multiple_of(tile * grid_rows, grid_rows)
    t = tab_ref[pl.ds(r0, grid_rows), :]
    by_row = jnp.concatenate([jnp.broadcast_to(t[i:i + 1], (GRID_W, D_MODEL // 2))
                              for i in range(grid_rows)], axis=0)
    by_col = jnp.concatenate([tab_ref[...]] * grid_rows, axis=0)
    return jnp.concatenate([by_row, by_col], axis=1)


def _premix_kernel(*refs, pos_tiles):
    x_ref, mod_ref, gpre_ref, w_ref, wa2_ref, ba2_ref, cs_ref = refs[:7]
    refs = refs[7:]
    if pos_tiles:
        tab_ref, refs = refs[0], refs[1:]
    q_ref, k_ref, v_ref, g_ref, la_ref, uc_ref, us_ref, wbf_ref = refs

    @pl.when(pl.program_id(0) == 0)
    def _():
        wbf_ref[...] = w_ref[...].astype(BF16)

    sh = mod_ref[:, 0:D_MODEL]
    sc = mod_ref[:, D_MODEL:2 * D_MODEL]
    for sub in range(PRE_STEP_TILES):
        rows = slice(sub * ROW_TILE, (sub + 1) * ROW_TILE)
        x = x_ref[rows, :]
        if pos_tiles:
            x = x + _pos_tile(tab_ref, (pl.program_id(0) * PRE_STEP_TILES + sub) % pos_tiles)
        h = (_rms(x, gpre_ref[...]) * (1.0 + sc) + sh).astype(BF16)
        u = _dot_nt(h, wbf_ref[MAIN_W + LR_W:, :]).astype(BF16)
        lr = _dot_nt(h, wbf_ref[MAIN_W:MAIN_W + LR_W, :]).astype(BF16)
        q_ref[rows, :] = _dot_nt(h, wbf_ref[0:QK_W, :]) * (GLA_DK ** -0.5)
        k_ref[rows, :] = _dot_nt(h, wbf_ref[QK_W:2 * QK_W, :])
        v_ref[rows, :] = _dot_nt(h, wbf_ref[2 * QK_W:2 * QK_W + V_W, :]).astype(BF16)
        g_ref[rows, :] = _dot_nt(h, wbf_ref[2 * QK_W + V_W:MAIN_W, :]).astype(BF16)
        logit = _dot(lr, wa2_ref[...]) + ba2_ref[...]
        softplus_neg = jnp.maximum(-logit, 0.0) + jnp.log(1.0 + jnp.exp(-jnp.abs(logit)))
        la_ref[rows, :] = softplus_neg * (-1.0 / GATE_TAU)
        for grp in range(FNO_GROUPS):
            sl = slice(grp * FNO_GROUP_W, (grp + 1) * FNO_GROUP_W)
            ucs = _dot(u[:, sl], cs_ref[...])
            uc_ref[rows, sl] = ucs[:, :FNO_GROUP_W].astype(uc_ref.dtype)
            us_ref[rows, sl] = ucs[:, FNO_GROUP_W:].astype(us_ref.dtype)


def _premix(layer, x, mod, mod_sel, g_pre, w_t, wa2, ba2, cs_w, fno_dtype, pos=None):
    t = x.shape[0]
    pos_args, pos_specs, pos_tiles = [], [], 0
    if pos is not None:
        pos_args, pos_tiles = [pos[0]], pos[1]
        pos_specs = [pl.BlockSpec(pos[0].shape, lambda i: (0, 0))]
    step_rows = ROW_TILE * PRE_STEP_TILES
    row = lambda w: pl.BlockSpec((step_rows, w), lambda i: (i, 0))
    lay = lambda *shape: pl.BlockSpec((None,) + shape, lambda i: (layer,) + (0,) * len(shape))
    out = lambda w, dt: jax.ShapeDtypeStruct((t, w), dt)
    return pl.pallas_call(
        functools.partial(_premix_kernel, pos_tiles=pos_tiles),
        grid=(t // step_rows,),
        in_specs=[row(D_MODEL),
                  pl.BlockSpec((None, None, 1, 6 * D_MODEL),
                               lambda i: (layer, mod_sel(i * PRE_STEP_TILES), 0, 0)),
                  lay(1, D_MODEL),
                  pl.BlockSpec((None, D_IN, D_MODEL), lambda i: (layer, 0, 0), pipeline_mode=pl.Buffered(1)),
                  lay(LR_W, 2 * QK_W), lay(1, 2 * QK_W),
                  pl.BlockSpec((FNO_GROUP_W, 2 * FNO_GROUP_W), lambda i: (0, 0)), *pos_specs],
        out_specs=[row(QK_W), row(QK_W), row(V_W), row(V_W), row(2 * QK_W), row(FNO_W), row(FNO_W)],
        out_shape=[out(QK_W, F32), out(QK_W, F32), out(V_W, BF16), out(V_W, BF16),
                   out(2 * QK_W, F32), out(FNO_W, fno_dtype), out(FNO_W, fno_dtype)],
        scratch_shapes=[pltpu.VMEM((D_IN, D_MODEL), BF16)],
        compiler_params=pltpu.CompilerParams(dimension_semantics=("arbitrary",),
                                             vmem_limit_bytes=POST_VMEM_LIMIT),
        name="premix",
    )(x, mod, g_pre, w_t, wa2, ba2, cs_w, *pos_args)


def _gla_chunk(r0, forward, q_ref, k_ref, v_ref, la_ref, gap_ref, lvl_ref, st_ref):
    rows = pl.ds(r0, CHUNK)
    lane = lax.broadcasted_iota(jnp.int32, (CHUNK, PAIR_K), 1)
    head0 = lane < GLA_DK
    la = la_ref[rows, :]
    la_hi = la.astype(BF16)
    la_lo = (la - la_hi.astype(F32)).astype(BF16)
    gaps = _dot(gap_ref[...], jnp.concatenate([la_hi, la_lo], axis=1))
    decay = jnp.exp(gaps[:, :PAIR_K] + gaps[:, PAIR_K:])
    blk = lambda j: decay[j * CHUNK:(j + 1) * CHUNK]
    q = q_ref[rows, :]
    k = k_ref[rows, :]
    k_h = (jnp.where(head0, k, 0.0), jnp.where(head0, 0.0, k))
    v = v_ref[rows, :]
    lvl = lvl_ref[...]

    a = jnp.zeros((CHUNK, 2 * CHUNK), F32)
    for lv in range(N_LEVELS + 1):
        if lv == 0:
            ql = q.astype(BF16)
            kl = jnp.concatenate([k_h[0].astype(BF16), k_h[1].astype(BF16)], axis=0)
        else:
            d = blk(1 + lv)
            ql = (q * d).astype(BF16)
            kl = jnp.concatenate([(k_h[0] * d).astype(BF16), (k_h[1] * d).astype(BF16)], axis=0)
        a = jnp.where(lvl == lv, _dot_nt(ql, kl), a)

    q_in = (q * blk(0)).astype(BF16)
    inter = _dot_nt(q_in, _state_per_head(st_ref[...]))
    vlane = lax.broadcasted_iota(jnp.int32, (CHUNK, PAIR_V), 1)
    zero = jnp.zeros_like(v)
    v_blk = jnp.concatenate([jnp.where(vlane < GLA_DV, v, zero),
                             jnp.where(vlane < GLA_DV, zero, v)], axis=0)
    o = inter + _dot(a.astype(BF16), v_blk)

    d_out = blk(1)
    total = blk(0)[CHUNK - 1:CHUNK] if forward else blk(0)[0:1]
    k_out = jnp.concatenate([(k_h[0] * d_out).astype(BF16), (k_h[1] * d_out).astype(BF16)], axis=0)
    st_ref[...] = st_ref[...] * total + _dot_tn(_stack_heads(v), k_out)
    return o


def _state_per_head(st):
    lane = lax.broadcasted_iota(jnp.int32, (1, PAIR_K), 1)
    keep0 = (lane < GLA_DK).astype(BF16)
    st = st.astype(BF16)
    return jnp.concatenate([st * keep0, st * (1.0 - keep0).astype(BF16)], axis=0)


def _stack_heads(v):
    return jnp.concatenate([v[:, :GLA_DV], v[:, GLA_DV:]], axis=0)


def _gla_fast_step(jobs, fwd_refs, bwd_refs):
    refs = [fwd_refs if fw else bwd_refs for _, fw, _ in jobs]
    rows = [pl.ds(r0, CHUNK) for r0, _, _ in jobs]
    lane = lax.broadcasted_iota(jnp.int32, (CHUNK, PAIR_K), 1)
    head0 = lane < GLA_DK
    vlane = lax.broadcasted_iota(jnp.int32, (CHUNK, PAIR_V), 1) < GLA_DV

    la = [r[3][rw, :] for r, rw in zip(refs, rows)]
    la_hi = [x.astype(BF16) for x in la]
    la_lo = [(x - h.astype(F32)).astype(BF16) for x, h in zip(la, la_hi)]
    cum = [_dot(r[4][0:CHUNK, :], jnp.concatenate([h, l], axis=1))
           for r, h, l in zip(refs, la_hi, la_lo)]
    b = [x[:, :PAIR_K] + x[:, PAIR_K:] for x in cum]
    b_end = [x[CHUNK - 1:CHUNK] if fw else x[0:1] for x, (_, fw, _) in zip(b, jobs)]
    q = [r[0][rw, :] for r, rw in zip(refs, rows)]
    k = [r[1][rw, :] for r, rw in zip(refs, rows)]
    v = [r[2][rw, :] for r, rw in zip(refs, rows)]
    qd = [(x * jnp.exp(y)).astype(BF16) for x, y in zip(q, b)]
    kd = [x * jnp.exp(-y) for x, y in zip(k, b)]
    ko = [x * jnp.exp(e - y) for x, y, e in zip(k, b, b_end)]
    total = [jnp.exp(e) for e in b_end]
    kd_cat = [jnp.concatenate([jnp.where(head0, x, 0.0).astype(BF16),
                               jnp.where(head0, 0.0, x).astype(BF16)], axis=0) for x in kd]
    score = [_dot_nt(x, y) for x, y in zip(qd, kd_cat)]
    a = [jnp.where(r[5][...] <= N_LEVELS, s, 0.0).astype(BF16) for r, s in zip(refs, score)]
    v_blk = [jnp.concatenate([jnp.where(vlane, x, jnp.zeros_like(x)),
                              jnp.where(vlane, jnp.zeros_like(x), x)], axis=0) for x in v]
    intra = [_dot(x, y) for x, y in zip(a, v_blk)]
    ko_cat = [jnp.concatenate([jnp.where(head0, x, 0.0).astype(BF16),
                               jnp.where(head0, 0.0, x).astype(BF16)], axis=0) for x in ko]
    upd = [_dot_tn(_stack_heads(x), y) for x, y in zip(v, ko_cat)]

    outs = [None] * len(jobs)
    for scan in sorted({(fw, slot) for _, fw, slot in jobs}):
        st_ref = (fwd_refs if scan[0] else bwd_refs)[6]
        st = st_ref[scan[1]]
        for j, (_, fw, slot) in enumerate(jobs):
            if (fw, slot) != scan:
                continue
            outs[j] = _dot_nt(qd[j], _state_per_head(st)) + intra[j]
            st = st * total[j] + upd[j]
        st_ref[scan[1]] = st
    return outs


def _gla_kernel(*refs, seq_len, n_seq, has_state, emit_state, unroll, par, layer, all_layers, n_alias):
    (q_ref, k_ref, v_ref, laf_ref, lab_ref,
     gapf_ref, gapb_ref, lvlf_ref, lvlb_ref) = refs[:9]
    pos = 9
    if has_state:
        s0f_ref, s0b_ref = refs[pos:pos + 2]
        pos += 2
    pos += n_alias
    o_ref = refs[pos]
    pos += 1
    if emit_state:
        sf_ref, sb_ref = refs[pos:pos + 2]
        pos += 2
        if all_layers:
            sf_ref[...] = jnp.zeros_like(sf_ref)
            sb_ref[...] = jnp.zeros_like(sb_ref)
    acc_ref, stf_ref, stb_ref = refs[pos:pos + 3]
    n_chunks = seq_len // CHUNK
    half_iters = n_chunks // (2 * unroll)
    fwd_refs = (q_ref, k_ref, v_ref, laf_ref, gapf_ref, lvlf_ref, stf_ref)
    bwd_refs = (q_ref, k_ref, v_ref, lab_ref, gapb_ref, lvlb_ref, stb_ref)

    def finish(o, rows):
        o_ref[rows, :] = o.astype(o_ref.dtype)

    def seq_body(sg, carry, fast):
        base = sg * (par * seq_len)
        if has_state:
            stf_ref[0] = s0f_ref[...]
            stb_ref[0] = s0b_ref[...]
        else:
            stf_ref[...] = jnp.zeros_like(stf_ref)
            stb_ref[...] = jnp.zeros_like(stb_ref)

        def step(it, second_visit):
            jobs = []
            for slot in range(par):
                for u in range(unroll):
                    cf = it * unroll + u
                    for forward, ci in ((True, cf), (False, n_chunks - 1 - cf)):
                        r0 = base + slot * seq_len + ci * CHUNK
                        jobs.append((pl.multiple_of(r0, CHUNK), forward, slot))
            if fast:
                outs = _gla_fast_step(jobs, fwd_refs, bwd_refs)
            else:
                outs = [_gla_chunk(r0, fw, *(fwd_refs if fw else bwd_refs)[:6],
                                   (stf_ref if fw else stb_ref).at[slot]) for r0, fw, slot in jobs]
            for (r0, _, _), o in zip(jobs, outs):
                rows = pl.ds(r0, CHUNK)
                if second_visit:
                    finish(o + acc_ref[rows, :], rows)
                else:
                    acc_ref[rows, :] = o

        def first(i, c):
            step(i, False)
            return c

        def second(i, c):
            step(half_iters + i, True)
            return c

        lax.fori_loop(0, half_iters, first, 0)
        lax.fori_loop(0, half_iters, second, 0)
        if emit_state:
            for s_out, st_ref in ((sf_ref, stf_ref), (sb_ref, stb_ref)):
                for slot in range(par):
                    for hd in range(2):
                        s_fin = st_ref[slot].T[hd * GLA_DK:(hd + 1) * GLA_DK, :]
                        if all_layers:
                            s_out[sg * par + slot, layer, hd] = s_fin
                        else:
                            s_out[sg * par + slot, hd] = s_fin
        return carry

    weakest = jnp.minimum(jnp.min(laf_ref[...]), jnp.min(lab_ref[...]))
    is_fast = weakest >= -FAST_STEP_LIMIT

    @pl.when(is_fast)
    def _():
        lax.fori_loop(0, n_seq // par, functools.partial(seq_body, fast=True), 0)

    @pl.when(jnp.logical_not(is_fast))
    def _():
        lax.fori_loop(0, n_seq // par, functools.partial(seq_body, fast=False), 0)


def _gla(layer, q, k, v, la, consts, seq_len, n_seq_total, n_seq, s0=None,
         emit_state=False, st_prev=None):
    gapf, gapb, lvlf, lvlb = consts
    rows = seq_len * n_seq
    n_steps = n_seq_total // n_seq
    rspec = lambda w, off=0: pl.BlockSpec((rows, w), lambda i, j: (i, j + off))
    const = lambda a: pl.BlockSpec(a.shape, lambda i, j: (0,) * a.ndim)
    in_specs = [rspec(PAIR_K), rspec(PAIR_K), rspec(PAIR_V), rspec(PAIR_K), rspec(PAIR_K, HEAD_PAIRS),
                const(gapf), const(gapb), const(lvlf), const(lvlb)]
    args = [q, k, v, la, la, gapf, gapb, lvlf, lvlb]
    if s0 is not None:
        sspec = pl.BlockSpec((None, None, None, GLA_DV, PAIR_K), lambda i, j: (i, layer, j, 0, 0))
        in_specs += [sspec, sspec]
        args += list(s0)
    out_specs = [rspec(PAIR_V)]
    out_shape = [jax.ShapeDtypeStruct((n_seq_total * seq_len, V_W), BF16)]
    aliases = {}
    if emit_state:
        if st_prev is None:
            st_spec = pl.BlockSpec((n_seq, DEPTH, 2, GLA_DK, GLA_DV), lambda i, j: (i, 0, j, 0, 0))
        else:
            st_spec = pl.BlockSpec((n_seq, None, 2, GLA_DK, GLA_DV), lambda i, j: (i, layer, j, 0, 0))
            for n, a in enumerate(st_prev):
                aliases[len(args)] = 1 + n
                in_specs.append(pl.BlockSpec(memory_space=pl.ANY))
                args.append(a)
        st_shape = jax.ShapeDtypeStruct((n_seq_total, DEPTH, GLA_HEADS, GLA_DK, GLA_DV), F32)
        out_specs += [st_spec, st_spec]
        out_shape += [st_shape, st_shape]
    unroll = min(SCAN_UNROLL, seq_len // (2 * CHUNK))
    par = 1 if s0 is not None else min(n_seq, SCAN_UNROLL // unroll)
    kern = functools.partial(_gla_kernel, seq_len=seq_len, n_seq=n_seq, has_state=s0 is not None,
                             emit_state=emit_state, unroll=unroll, par=par, layer=layer,
                             all_layers=emit_state and st_prev is None, n_alias=len(aliases))
    return pl.pallas_call(
        kern,
        grid=(n_steps, HEAD_PAIRS),
        in_specs=in_specs,
        out_specs=out_specs,
        out_shape=out_shape,
        input_output_aliases=aliases,
        scratch_shapes=[pltpu.VMEM((rows, PAIR_V), F32), pltpu.VMEM((par, GLA_DV, PAIR_K), F32),
                        pltpu.VMEM((par, GLA_DV, PAIR_K), F32)],
        compiler_params=_params(2),
        name="gla_state" if emit_state else "gla",
    )(*args)


def _fft_tables(n_side):
    c, s = _dft_tables(n_side)
    l1 = np.block([[c, -s], [-s, -c]])
    l2 = np.concatenate([c, s], axis=1)
    idx = np.arange(n_side)
    ang = 2.0 * np.pi * (idx[:, None] * idx[None, :]) / (n_side * n_side)
    lanes = np.ones((1, 1, FNO_GROUP_W), np.float32)
    tc = np.cos(ang).astype(np.float32)[:, :, None] * lanes
    ts = np.sin(ang).astype(np.float32)[:, :, None] * lanes
    return l1, l2, tc, ts


def _fft_kernel(uc_ref, us_ref, l1_ref, l2_ref, tc_ref, ts_ref, o_ref,
                xr_ref, xi_ref, zr_ref, zi_ref, y_ref, *, n_side):
    for r in range(n_side):
        src, dst = pl.ds(r * n_side, n_side), pl.ds(r * FFT_PITCH, n_side)
        xr_ref[dst, :] = uc_ref[src, :]
        xi_ref[dst, :] = us_ref[src, :]

    def stage1(n2, carry):
        col = pl.ds(n2, n_side, stride=FFT_PITCH)
        x = jnp.concatenate([xr_ref[col, :].astype(BF16), xi_ref[col, :].astype(BF16)], axis=0)
        z = _dot(l1_ref[...], x)
        zr, zi = z[:n_side], z[n_side:]
        tc, ts = tc_ref[n2], ts_ref[n2]
        out = pl.ds(pl.multiple_of(n2 * FFT_PITCH, 8), n_side)
        zr_ref[out, :] = zr * tc + zi * ts
        zi_ref[out, :] = zi * tc - zr * ts
        return carry

    def stage2(k1, carry):
        col = pl.ds(k1, n_side, stride=FFT_PITCH)
        z = jnp.concatenate([zr_ref[col, :].astype(BF16), zi_ref[col, :].astype(BF16)], axis=0)
        y_ref[col, :] = _dot(l2_ref[...], z)
        return carry

    lax.fori_loop(0, n_side, stage1, 0, unroll=FFT_UNROLL)
    lax.fori_loop(0, n_side, stage2, 0, unroll=FFT_UNROLL)
    for r in range(n_side):
        o_ref[pl.ds(r * n_side, n_side), :] = y_ref[pl.ds(r * FFT_PITCH, n_side), :]


def _fnet_fft(uc, us, seq_len, n_seq_total):
    n_side = GRID_W
    l1, l2, tc, ts = _fft_tables(n_side)
    as_bf16 = lambda a: jnp.asarray(a, F32).astype(BF16)
    blk = pl.BlockSpec((seq_len, FNO_GROUP_W), lambda b, g: (b, g))
    const = lambda a: pl.BlockSpec(a.shape, lambda b, g: (0,) * a.ndim)
    return pl.pallas_call(
        functools.partial(_fft_kernel, n_side=n_side),
        grid=(n_seq_total, FNO_GROUPS),
        in_specs=[blk, blk, const(l1), const(l2), const(tc), const(ts)],
        out_specs=blk,
        out_shape=jax.ShapeDtypeStruct((n_seq_total * seq_len, FNO_W), F32),
        scratch_shapes=[pltpu.VMEM((n_side * FFT_PITCH, FNO_GROUP_W), F32)] * 5,
        compiler_params=_params(2),
        name="fnet_fft",
    )(uc, us, as_bf16(l1), as_bf16(l2), jnp.asarray(tc), jnp.asarray(ts))


def _post_kernel(*refs, dft_len, pos_tiles):
    x_ref, o_ref, g_ref, gh_ref = refs[:4]
    refs = refs[4:]
    if dft_len:
        uc_ref, us_ref, cos_ref, nsin_ref = refs[:4]
        refs = refs[4:]
    else:
        yf_ref, refs = refs[0], refs[1:]
    if pos_tiles:
        tab_ref, refs = refs[0], refs[1:]
    mod_ref, gpm_ref, gpre_ref, gpost_ref, wout_ref, w1_ref, w2_ref, out_ref = refs
    gt_m = mod_ref[:, 2 * D_MODEL:3 * D_MODEL]
    sh_f = mod_ref[:, 3 * D_MODEL:4 * D_MODEL]
    sc_f = mod_ref[:, 4 * D_MODEL:5 * D_MODEL]
    gt_f = mod_ref[:, 5 * D_MODEL:6 * D_MODEL]
    for sub in range(POST_STEP_TILES):
        base = sub * ROW_TILE
        if dft_len:
            yf = jnp.concatenate(
                [(_dot(cos_ref[...], uc_ref[r0:r0 + dft_len, :]) +
                  _dot(nsin_ref[...], us_ref[r0:r0 + dft_len, :])).astype(BF16)
                 for r0 in range(base, base + ROW_TILE, dft_len)], axis=0)
        else:
            yf = yf_ref[base:base + ROW_TILE, :].astype(BF16)
        x_in = x_ref[base:base + ROW_TILE, :]
        if pos_tiles:
            x_in = x_in + _pos_tile(tab_ref, (pl.program_id(0) * POST_STEP_TILES + sub) % pos_tiles)
        x_blk, h_blk = [], []
        for r0 in range(0, ROW_TILE, POST_HEAD_ROWS):
            rows = slice(base + r0, base + r0 + POST_HEAD_ROWS)
            o_sum = o_ref[rows, :].astype(F32)
            gate = g_ref[rows, :].astype(F32)
            o = jnp.concatenate(
                [_rms(o_sum[:, sl], gh_ref[:, sl]) * _silu(gate[:, sl])
                 for sl in (slice(hd * GLA_DV, (hd + 1) * GLA_DV) for hd in range(GLA_HEADS))], axis=1)
            y = (_dot(o.astype(BF16), wout_ref[0:V_W, :]) +
                 _dot(yf[r0:r0 + POST_HEAD_ROWS, :], wout_ref[V_W:, :]))
            x_blk.append(x_in[r0:r0 + POST_HEAD_ROWS, :] + gt_m * _rms(y, gpm_ref[...]))
            h_blk.append((_rms(x_blk[-1], gpre_ref[...]) * (1.0 + sc_f) + sh_f).astype(BF16))
        x = jnp.concatenate(x_blk, axis=0)
        h = jnp.concatenate(h_blk, axis=0)
        f = jnp.zeros((ROW_TILE, D_MODEL), F32)
        n_split = 4
        wf = D_FF // n_split
        for j in range(n_split):
            a = jnp.maximum(_dot(h, w1_ref[:, j * wf:(j + 1) * wf]), 0.0)
            f = f + _dot((a * a).astype(BF16), w2_ref[j * wf:(j + 1) * wf, :])
        out_ref[base:base + ROW_TILE, :] = x + gt_f * _rms(f, gpost_ref[...])


def _post(layer, x, o, g, g_head, fno, mod, mod_sel, g_post_mix, g_pre_mlp, g_post_mlp, w_out, w_fc1, w_fc2,
          pos=None):
    t = x.shape[0]
    step_rows = ROW_TILE * POST_STEP_TILES
    row = lambda w: pl.BlockSpec((step_rows, w), lambda i: (i, 0))
    lay = lambda *shape: pl.BlockSpec((None,) + shape, lambda i: (layer,) + (0,) * len(shape),
                                      pipeline_mode=pl.Buffered(1))
    dft_len = fno[2].shape[0] if len(fno) == 4 else 0
    fno_specs = [row(FNO_W)] * min(len(fno), 2) + [
        pl.BlockSpec((dft_len, dft_len), lambda i: (0, 0))] * (len(fno) - min(len(fno), 2))
    return pl.pallas_call(
        functools.partial(_post_kernel, dft_len=dft_len, pos_tiles=pos[1] if pos else 0),
        grid=(t // step_rows,),
        in_specs=[row(D_MODEL), row(V_W), row(V_W), lay(1, V_W), *fno_specs,
                  *([pl.BlockSpec(pos[0].shape, lambda i: (0, 0))] if pos else []),
                  pl.BlockSpec((None, None, 1, 6 * D_MODEL),
                               lambda i: (layer, mod_sel(i * POST_STEP_TILES), 0, 0)),
                  lay(1, D_MODEL), lay(1, D_MODEL), lay(1, D_MODEL),
                  lay(V_W + FNO_W, D_MODEL), lay(D_MODEL, D_FF), lay(D_FF, D_MODEL)],
        out_specs=row(D_MODEL),
        out_shape=jax.ShapeDtypeStruct((t, D_MODEL), F32),
        compiler_params=pltpu.CompilerParams(dimension_semantics=("arbitrary",),
                                             vmem_limit_bytes=POST_VMEM_LIMIT),
        name="post",
    )(x, o, g, g_head, *fno, *([pos[0]] if pos else []), mod, g_post_mix, g_pre_mlp, g_post_mlp,
      w_out, w_fc1, w_fc2)


def _state_to_kernel(s):
    b, l = s.shape[:2]
    st = s.reshape(b, l, HEAD_PAIRS, 2, GLA_DK, GLA_DV)
    return jnp.transpose(st, (0, 1, 2, 5, 3, 4)).reshape(b, l, HEAD_PAIRS, GLA_DV, PAIR_K)


def kernel(x_prompt, x_sample, state_gla_fwd, state_gla_bwd, c, c_ctx, w_in, w_out, w_a2, b_a2,
           g_head, w_ada, b_ada, g_pre_mix, g_post_mix, g_pre_mlp, g_post_mlp, w_fc1, w_fc2):
    batch, seq, _ = x_prompt.shape
    dec_batch, dec_seq, _ = x_sample.shape
    t_ctx = batch * seq
    t_lat = dec_batch * dec_seq
    assert dec_seq == GRID_W * GRID_W and seq % CHUNK == 0 and ROW_TILE % seq == 0
    assert (dec_seq // ROW_TILE) % POST_STEP_TILES == 0 and (dec_seq // ROW_TILE) % PRE_STEP_TILES == 0
    lat_tiles = dec_seq // ROW_TILE
    mod_sel = (lambda i: 0, lambda i: 1 + i // lat_tiles)

    za = jnp.zeros((DEPTH, LR_RANK, QK_W), F32)
    wa2 = jnp.concatenate([jnp.concatenate([w_a2[:, 0], za], axis=-1),
                           jnp.concatenate([za, w_a2[:, 1]], axis=-1)], axis=1).astype(BF16)
    ba2 = b_a2.reshape(DEPTH, 1, 2 * QK_W)
    w_t = jnp.swapaxes(w_in, 1, 2)
    w_out_b = w_out.astype(BF16)
    w_fc1_b = w_fc1.astype(BF16)
    w_fc2_b = w_fc2.astype(BF16)
    vec = lambda a: a.reshape(DEPTH, 1, -1)
    g_head_r = g_head.reshape(DEPTH, 1, V_W)

    gapf, gapb, lvlf, lvlb = _gap_matrices()
    as_bf16 = lambda a: jnp.asarray(a, F32).astype(BF16)
    consts = (as_bf16(gapf), as_bf16(gapb), jnp.asarray(lvlf), jnp.asarray(lvlb))
    cw, sw = _dft_tables(FNO_GROUP_W)
    cs_w = as_bf16(np.concatenate([cw, sw], axis=1))
    cn, sn = _dft_tables(seq)
    cos_ctx, nsin_ctx = as_bf16(cn), as_bf16(-sn)
    pos = (jnp.asarray(_pos_table()), lat_tiles)

    cvec = jnp.zeros((MOD_ROWS, D_MODEL), F32).at[0].set(c_ctx).at[1:1 + dec_batch].set(c)
    mod = _modulation(cvec, w_ada, b_ada)[:, :1 + dec_batch, None, :]

    xs = [x_prompt.reshape(t_ctx, D_MODEL), x_sample.reshape(t_lat, D_MODEL)]
    s0 = (_state_to_kernel(state_gla_fwd), _state_to_kernel(state_gla_bwd))

    states = None
    for l in range(DEPTH):
        lat_pos = pos if l == 0 else None
        pre = [_premix(l, x, mod, sel, vec(g_pre_mix), w_t, wa2, ba2, cs_w, dt, p)
               for x, sel, dt, p in zip(xs, mod_sel, (BF16, F32), (None, lat_pos))]
        q, k, v, g_ctx, la, uc, us = pre[0]
        o_ctx, *states = _gla(l, q, k, v, la, consts, seq, batch, 8, emit_state=True, st_prev=states)
        fno_ctx = (uc, us, cos_ctx, nsin_ctx)
        q, k, v, g_lat, la, uc, us = pre[1]
        (o_lat,) = _gla(l, q, k, v, la, consts, dec_seq, dec_batch, 1, s0=s0)
        fno_lat = (_fnet_fft(uc, us, dec_seq, dec_batch),)
        xs = [_post(l, x, o, g, g_head_r, fno, mod, sel, vec(g_post_mix), vec(g_pre_mlp), vec(g_post_mlp),
                    w_out_b, w_fc1_b, w_fc2_b, p)
              for x, o, g, fno, sel, p in zip(xs, (o_ctx, o_lat), (g_ctx, g_lat), (fno_ctx, fno_lat),
                                              mod_sel, (None, lat_pos))]

    y_prompt = xs[0].reshape(batch, seq, D_MODEL)
    y_sample = xs[1].reshape(dec_batch, dec_seq, D_MODEL)
    return (y_prompt, y_sample, states[0], states[1])
```

```python
import functools

import numpy as np
import jax
import jax.numpy as jnp
from jax import lax
from jax.experimental import pallas as pl
from jax.experimental.pallas import tpu as pltpu

D_MODEL = 1024
DEPTH = 4
GRID_W = 64
GLA_HEADS = 4
GLA_DK = 64
GLA_DV = 128
CHUNK = 64
GATE_TAU = 16.0
LR_RANK = 16
FNO_GROUPS = 4
FNO_GROUP_W = 128
D_FF = 4 * D_MODEL
EPS = 1e-6
QK_W = GLA_HEADS * GLA_DK
V_W = GLA_HEADS * GLA_DV
FNO_W = FNO_GROUPS * FNO_GROUP_W
MAIN_W = 2 * QK_W + 2 * V_W
LR_W = 2 * LR_RANK
D_IN = MAIN_W + LR_W + FNO_W
HEAD_PAIRS = GLA_HEADS // 2
PAIR_K = 2 * GLA_DK
PAIR_V = 2 * GLA_DV
N_LEVELS = 6
N_GAPS = N_LEVELS + 2
SCAN_UNROLL = 8
FAST_STEP_LIMIT = 1.0

VMEM_LIMIT = 48 * 1024 * 1024
ROW_TILE = 512
POST_HEAD_ROWS = 128
MOD_ROWS = 8
FFT_PITCH = 72
FFT_UNROLL = 16

BF16 = jnp.bfloat16
F32 = jnp.float32


def _gap_matrices():
    c = CHUNK
    m_all = np.zeros((N_GAPS, c, c), np.float32)
    for p in range(c):
        m_all[0, p, :p + 1] = 1.0
        m_all[1, p, p + 1:] = 1.0
    for lv in range(N_LEVELS):
        m = c >> (lv + 1)
        for p in range(c):
            base = (p // (2 * m)) * 2 * m
            ref = base + m - 1
            if p > ref:
                m_all[2 + lv, p, ref + 1:p + 1] = 1.0
            else:
                m_all[2 + lv, p, p + 1:ref + 1] = 1.0
    fwd = m_all.reshape(N_GAPS * c, c)
    bwd = m_all[:, ::-1, ::-1].reshape(N_GAPS * c, c)
    lvl = np.full((c, c), N_LEVELS + 1, np.int32)
    for t in range(c):
        for s in range(t + 1):
            if s == t:
                lvl[t, s] = 0
            else:
                top = (t ^ s).bit_length() - 1
                lvl[t, s] = N_LEVELS - top
    lvl_f = np.concatenate([lvl, lvl], axis=1)
    lvl_b = np.concatenate([lvl[::-1, ::-1], lvl[::-1, ::-1]], axis=1)
    return fwd, bwd, lvl_f, lvl_b


def _dft_tables(n):
    idx = np.arange(n)
    ang = 2.0 * np.pi * ((idx[:, None] * idx[None, :]) % n) / n
    s = 1.0 / np.sqrt(n)
    return (np.cos(ang) * s).astype(np.float32), (np.sin(ang) * s).astype(np.float32)


def _silu(x):
    return x * (1.0 / (1.0 + jnp.exp(-x)))


def _rms(x, g):
    return x * lax.rsqrt(jnp.mean(x * x, axis=-1, keepdims=True) + EPS) * g


def _dot(a, b):
    return jnp.dot(a, b, preferred_element_type=F32)


def _dot_nt(a, b):
    return lax.dot_general(a, b, (((1,), (1,)), ((), ())), preferred_element_type=F32)


def _dot_tn(a, b):
    return lax.dot_general(a, b, (((0,), (0,)), ((), ())), preferred_element_type=F32)


def _params(n_grid):
    return pltpu.CompilerParams(dimension_semantics=("arbitrary",) * n_grid,
                                vmem_limit_bytes=VMEM_LIMIT)


def _mod_kernel(c_ref, w_ref, b_ref, o_ref):
    s = _silu(c_ref[...])
    w = w_ref[...]
    s_hi, w_hi = s.astype(BF16), w.astype(BF16)
    s_lo = (s - s_hi.astype(F32)).astype(BF16)
    w_lo = (w - w_hi.astype(F32)).astype(BF16)
    main = _dot(jnp.concatenate([s_hi, s_lo], axis=0), w_hi)
    o_ref[...] = main[:MOD_ROWS] + main[MOD_ROWS:] + _dot(s_hi, w_lo) + b_ref[...]


def _modulation(cvec, w_ada, b_ada):
    tn = 3072
    return pl.pallas_call(
        _mod_kernel,
        grid=(DEPTH, 6 * D_MODEL // tn),
        in_specs=[pl.BlockSpec((MOD_ROWS, D_MODEL), lambda l, j: (0, 0)),
                  pl.BlockSpec((None, D_MODEL, tn), lambda l, j: (l, 0, j)),
                  pl.BlockSpec((None, 1, tn), lambda l, j: (l, 0, j))],
        out_specs=pl.BlockSpec((None, MOD_ROWS, tn), lambda l, j: (l, 0, j)),
        out_shape=jax.ShapeDtypeStruct((DEPTH, MOD_ROWS, 6 * D_MODEL), F32),
        compiler_params=_params(2),
        name="modulation",
    )(cvec, w_ada, b_ada.reshape(DEPTH, 1, 6 * D_MODEL))


def _pos_table():
    nf = D_MODEL // 4
    omega = 1.0 / (10000.0 ** (np.arange(nf, dtype=np.float64) / nf))
    idx = np.arange(GRID_W, dtype=np.float64)[:, None]
    return np.concatenate([np.sin(idx * omega), np.cos(idx * omega)], axis=1).astype(np.float32)


def _pos_tile(tab_ref, tile):
    grid_rows = ROW_TILE // GRID_W
    r0 = pl.multiple_of(tile * grid_rows, grid_rows)
    t = tab_ref[pl.ds(r0, grid_rows), :]
    by_row = jnp.concatenate([jnp.broadcast_to(t[i:i + 1], (GRID_W, D_MODEL // 2))
                              for i in range(grid_rows)], axis=0)
    by_col = jnp.concatenate([tab_ref[...]] * grid_rows, axis=0)
    return jnp.concatenate([by_row, by_col], axis=1)


def _premix_kernel(*refs, pos_tiles):
    x_ref, mod_ref, gpre_ref, w_ref, wa2_ref, ba2_ref, cs_ref = refs[:7]
    refs = refs[7:]
    if pos_tiles:
        tab_ref, refs = refs[0], refs[1:]
    q_ref, k_ref, v_ref, g_ref, la_ref, uc_ref, us_ref, wbf_ref = refs

    @pl.when(pl.program_id(0) == 0)
    def _():
        wbf_ref[...] = w_ref[...].astype(BF16)

    x = x_ref[...]
    if pos_tiles:
        x = x + _pos_tile(tab_ref, pl.program_id(0) % pos_tiles)
    sh = mod_ref[:, 0:D_MODEL]
    sc = mod_ref[:, D_MODEL:2 * D_MODEL]
    h = (_rms(x, gpre_ref[...]) * (1.0 + sc) + sh).astype(BF16)
    u = _dot_nt(h, wbf_ref[MAIN_W + LR_W:, :]).astype(BF16)
    lr = _dot_nt(h, wbf_ref[MAIN_W:MAIN_W + LR_W, :]).astype(BF16)
    q_ref[...] = _dot_nt(h, wbf_ref[0:QK_W, :]) * (GLA_DK ** -0.5)
    k_ref[...] = _dot_nt(h, wbf_ref[QK_W:2 * QK_W, :])
    v_ref[...] = _dot_nt(h, wbf_ref[2 * QK_W:2 * QK_W + V_W, :]).astype(BF16)
    g_ref[...] = _dot_nt(h, wbf_ref[2 * QK_W + V_W:MAIN_W, :]).astype(BF16)
    logit = _dot(lr, wa2_ref[...]) + ba2_ref[...]
    softplus_neg = jnp.maximum(-logit, 0.0) + jnp.log(1.0 + jnp.exp(-jnp.abs(logit)))
    la_ref[...] = softplus_neg * (-1.0 / GATE_TAU)
    for grp in range(FNO_GROUPS):
        sl = slice(grp * FNO_GROUP_W, (grp + 1) * FNO_GROUP_W)
        ucs = _dot(u[:, sl], cs_ref[...])
        uc_ref[:, sl] = ucs[:, :FNO_GROUP_W].astype(uc_ref.dtype)
        us_ref[:, sl] = ucs[:, FNO_GROUP_W:].astype(us_ref.dtype)


def _premix(layer, x, mod, mod_sel, g_pre, w_t, wa2, ba2, cs_w, fno_dtype, pos=None):
    t = x.shape[0]
    pos_args, pos_specs, pos_tiles = [], [], 0
    if pos is not None:
        pos_args, pos_tiles = [pos[0]], pos[1]
        pos_specs = [pl.BlockSpec(pos[0].shape, lambda i: (0, 0))]
    row = lambda w: pl.BlockSpec((ROW_TILE, w), lambda i: (i, 0))
    lay = lambda *shape: pl.BlockSpec((None,) + shape, lambda i: (layer,) + (0,) * len(shape))
    out = lambda w, dt: jax.ShapeDtypeStruct((t, w), dt)
    return pl.pallas_call(
        functools.partial(_premix_kernel, pos_tiles=pos_tiles),
        grid=(t // ROW_TILE,),
        in_specs=[row(D_MODEL),
                  pl.BlockSpec((None, None, 1, 6 * D_MODEL), lambda i: (layer, mod_sel(i), 0, 0)),
                  lay(1, D_MODEL),
                  pl.BlockSpec((None, D_IN, D_MODEL), lambda i: (layer, 0, 0), pipeline_mode=pl.Buffered(1)),
                  lay(LR_W, 2 * QK_W), lay(1, 2 * QK_W),
                  pl.BlockSpec((FNO_GROUP_W, 2 * FNO_GROUP_W), lambda i: (0, 0)), *pos_specs],
        out_specs=[row(QK_W), row(QK_W), row(V_W), row(V_W), row(2 * QK_W), row(FNO_W), row(FNO_W)],
        out_shape=[out(QK_W, F32), out(QK_W, F32), out(V_W, BF16), out(V_W, BF16),
                   out(2 * QK_W, F32), out(FNO_W, fno_dtype), out(FNO_W, fno_dtype)],
        scratch_shapes=[pltpu.VMEM((D_IN, D_MODEL), BF16)],
        compiler_params=_params(1),
        name="premix",
    )(x, mod, g_pre, w_t, wa2, ba2, cs_w, *pos_args)


def _gla_chunk(r0, forward, q_ref, k_ref, v_ref, la_ref, gap_ref, lvl_ref, st_ref):
    rows = pl.ds(r0, CHUNK)
    lane = lax.broadcasted_iota(jnp.int32, (CHUNK, PAIR_K), 1)
    head0 = lane < GLA_DK
    la = la_ref[rows, :]
    la_hi = la.astype(BF16)
    la_lo = (la - la_hi.astype(F32)).astype(BF16)
    gaps = _dot(gap_ref[...], jnp.concatenate([la_hi, la_lo], axis=1))
    decay = jnp.exp(gaps[:, :PAIR_K] + gaps[:, PAIR_K:])
    blk = lambda j: decay[j * CHUNK:(j + 1) * CHUNK]
    q = q_ref[rows, :]
    k = k_ref[rows, :]
    k_h = (jnp.where(head0, k, 0.0), jnp.where(head0, 0.0, k))
    v = v_ref[rows, :]
    lvl = lvl_ref[...]

    a = jnp.zeros((CHUNK, 2 * CHUNK), F32)
    for lv in range(N_LEVELS + 1):
        if lv == 0:
            ql = q.astype(BF16)
            kl = jnp.concatenate([k_h[0].astype(BF16), k_h[1].astype(BF16)], axis=0)
        else:
            d = blk(1 + lv)
            ql = (q * d).astype(BF16)
            kl = jnp.concatenate([(k_h[0] * d).astype(BF16), (k_h[1] * d).astype(BF16)], axis=0)
        a = jnp.where(lvl == lv, _dot_nt(ql, kl), a)

    q_in = (q * blk(0)).astype(BF16)
    inter = _dot_nt(q_in, _state_per_head(st_ref[...]))
    vlane = lax.broadcasted_iota(jnp.int32, (CHUNK, PAIR_V), 1)
    zero = jnp.zeros_like(v)
    v_blk = jnp.concatenate([jnp.where(vlane < GLA_DV, v, zero),
                             jnp.where(vlane < GLA_DV, zero, v)], axis=0)
    o = inter + _dot(a.astype(BF16), v_blk)

    d_out = blk(1)
    total = blk(0)[CHUNK - 1:CHUNK] if forward else blk(0)[0:1]
    k_out = jnp.concatenate([(k_h[0] * d_out).astype(BF16), (k_h[1] * d_out).astype(BF16)], axis=0)
    st_ref[...] = st_ref[...] * total + _dot_tn(_stack_heads(v), k_out)
    return o


def _state_per_head(st):
    lane = lax.broadcasted_iota(jnp.int32, (1, PAIR_K), 1)
    keep0 = (lane < GLA_DK).astype(BF16)
    st = st.astype(BF16)
    return jnp.concatenate([st * keep0, st * (1.0 - keep0).astype(BF16)], axis=0)


def _stack_heads(v):
    return jnp.concatenate([v[:, :GLA_DV], v[:, GLA_DV:]], axis=0)


def _gla_fast_step(jobs, fwd_refs, bwd_refs):
    refs = [fwd_refs if fw else bwd_refs for _, fw, _ in jobs]
    rows = [pl.ds(r0, CHUNK) for r0, _, _ in jobs]
    lane = lax.broadcasted_iota(jnp.int32, (CHUNK, PAIR_K), 1)
    head0 = lane < GLA_DK
    vlane = lax.broadcasted_iota(jnp.int32, (CHUNK, PAIR_V), 1) < GLA_DV

    la = [r[3][rw, :] for r, rw in zip(refs, rows)]
    la_hi = [x.astype(BF16) for x in la]
    la_lo = [(x - h.astype(F32)).astype(BF16) for x, h in zip(la, la_hi)]
    cum = [_dot(r[4][0:CHUNK, :], jnp.concatenate([h, l], axis=1))
           for r, h, l in zip(refs, la_hi, la_lo)]
    b = [x[:, :PAIR_K] + x[:, PAIR_K:] for x in cum]
    b_end = [x[CHUNK - 1:CHUNK] if fw else x[0:1] for x, (_, fw, _) in zip(b, jobs)]
    q = [r[0][rw, :] for r, rw in zip(refs, rows)]
    k = [r[1][rw, :] for r, rw in zip(refs, rows)]
    v = [r[2][rw, :] for r, rw in zip(refs, rows)]
    qd = [(x * jnp.exp(y)).astype(BF16) for x, y in zip(q, b)]
    kd = [x * jnp.exp(-y) for x, y in zip(k, b)]
    ko = [x * jnp.exp(e - y) for x, y, e in zip(k, b, b_end)]
    total = [jnp.exp(e) for e in b_end]
    kd_cat = [jnp.concatenate([jnp.where(head0, x, 0.0).astype(BF16),
                               jnp.where(head0, 0.0, x).astype(BF16)], axis=0) for x in kd]
    score = [_dot_nt(x, y) for x, y in zip(qd, kd_cat)]
    a = [jnp.where(r[5][...] <= N_LEVELS, s, 0.0).astype(BF16) for r, s in zip(refs, score)]
    v_blk = [jnp.concatenate([jnp.where(vlane, x, jnp.zeros_like(x)),
                              jnp.where(vlane, jnp.zeros_like(x), x)], axis=0) for x in v]
    intra = [_dot(x, y) for x, y in zip(a, v_blk)]
    ko_cat = [jnp.concatenate([jnp.where(head0, x, 0.0).astype(BF16),
                               jnp.where(head0, 0.0, x).astype(BF16)], axis=0) for x in ko]
    upd = [_dot_tn(_stack_heads(x), y) for x, y in zip(v, ko_cat)]

    outs = [None] * len(jobs)
    for scan in sorted({(fw, slot) for _, fw, slot in jobs}):
        st_ref = (fwd_refs if scan[0] else bwd_refs)[6]
        st = st_ref[scan[1]]
        for j, (_, fw, slot) in enumerate(jobs):
            if (fw, slot) != scan:
                continue
            outs[j] = _dot_nt(qd[j], _state_per_head(st)) + intra[j]
            st = st * total[j] + upd[j]
        st_ref[scan[1]] = st
    return outs


def _gla_kernel(*refs, seq_len, n_seq, has_state, emit_state, unroll, par, layer, all_layers, n_alias):
    (q_ref, k_ref, v_ref, laf_ref, lab_ref,
     gapf_ref, gapb_ref, lvlf_ref, lvlb_ref) = refs[:9]
    pos = 9
    if has_state:
        s0f_ref, s0b_ref = refs[pos:pos + 2]
        pos += 2
    pos += n_alias
    o_ref = refs[pos]
    pos += 1
    if emit_state:
        sf_ref, sb_ref = refs[pos:pos + 2]
        pos += 2
        if all_layers:
            sf_ref[...] = jnp.zeros_like(sf_ref)
            sb_ref[...] = jnp.zeros_like(sb_ref)
    acc_ref, stf_ref, stb_ref = refs[pos:pos + 3]
    n_chunks = seq_len // CHUNK
    half_iters = n_chunks // (2 * unroll)
    fwd_refs = (q_ref, k_ref, v_ref, laf_ref, gapf_ref, lvlf_ref, stf_ref)
    bwd_refs = (q_ref, k_ref, v_ref, lab_ref, gapb_ref, lvlb_ref, stb_ref)

    def finish(o, rows):
        o_ref[rows, :] = o.astype(o_ref.dtype)

    def seq_body(sg, carry, fast):
        base = sg * (par * seq_len)
        if has_state:
            stf_ref[0] = s0f_ref[...]
            stb_ref[0] = s0b_ref[...]
        else:
            stf_ref[...] = jnp.zeros_like(stf_ref)
            stb_ref[...] = jnp.zeros_like(stb_ref)

        def step(it, second_visit):
            jobs = []
            for slot in range(par):
                for u in range(unroll):
                    cf = it * unroll + u
                    for forward, ci in ((True, cf), (False, n_chunks - 1 - cf)):
                        r0 = base + slot * seq_len + ci * CHUNK
                        jobs.append((pl.multiple_of(r0, CHUNK), forward, slot))
            if fast:
                outs = _gla_fast_step(jobs, fwd_refs, bwd_refs)
            else:
                outs = [_gla_chunk(r0, fw, *(fwd_refs if fw else bwd_refs)[:6],
                                   (stf_ref if fw else stb_ref).at[slot]) for r0, fw, slot in jobs]
            for (r0, _, _), o in zip(jobs, outs):
                rows = pl.ds(r0, CHUNK)
                if second_visit:
                    finish(o + acc_ref[rows, :], rows)
                else:
                    acc_ref[rows, :] = o

        def first(i, c):
            step(i, False)
            return c

        def second(i, c):
            step(half_iters + i, True)
            return c

        lax.fori_loop(0, half_iters, first, 0)
        lax.fori_loop(0, half_iters, second, 0)
        if emit_state:
            for s_out, st_ref in ((sf_ref, stf_ref), (sb_ref, stb_ref)):
                for slot in range(par):
                    for hd in range(2):
                        s_fin = st_ref[slot].T[hd * GLA_DK:(hd + 1) * GLA_DK, :]
                        if all_layers:
                            s_out[sg * par + slot, layer, hd] = s_fin
                        else:
                            s_out[sg * par + slot, hd] = s_fin
        return carry

    weakest = jnp.minimum(jnp.min(laf_ref[...]), jnp.min(lab_ref[...]))
    is_fast = weakest >= -FAST_STEP_LIMIT

    @pl.when(is_fast)
    def _():
        lax.fori_loop(0, n_seq // par, functools.partial(seq_body, fast=True), 0)

    @pl.when(jnp.logical_not(is_fast))
    def _():
        lax.fori_loop(0, n_seq // par, functools.partial(seq_body, fast=False), 0)


def _gla(layer, q, k, v, la, consts, seq_len, n_seq_total, n_seq, s0=None,
         emit_state=False, st_prev=None):
    gapf, gapb, lvlf, lvlb = consts
    rows = seq_len * n_seq
    n_steps = n_seq_total // n_seq
    rspec = lambda w, off=0: pl.BlockSpec((rows, w), lambda i, j: (i, j + off))
    const = lambda a: pl.BlockSpec(a.shape, lambda i, j: (0,) * a.ndim)
    in_specs = [rspec(PAIR_K), rspec(PAIR_K), rspec(PAIR_V), rspec(PAIR_K), rspec(PAIR_K, HEAD_PAIRS),
                const(gapf), const(gapb), const(lvlf), const(lvlb)]
    args = [q, k, v, la, la, gapf, gapb, lvlf, lvlb]
    if s0 is not None:
        sspec = pl.BlockSpec((None, None, None, GLA_DV, PAIR_K), lambda i, j: (i, layer, j, 0, 0))
        in_specs += [sspec, sspec]
        args += list(s0)
    out_specs = [rspec(PAIR_V)]
    out_shape = [jax.ShapeDtypeStruct((n_seq_total * seq_len, V_W), BF16)]
    aliases = {}
    if emit_state:
        if st_prev is None:
            st_spec = pl.BlockSpec((n_seq, DEPTH, 2, GLA_DK, GLA_DV), lambda i, j: (i, 0, j, 0, 0))
        else:
            st_spec = pl.BlockSpec((n_seq, None, 2, GLA_DK, GLA_DV), lambda i, j: (i, layer, j, 0, 0))
            for n, a in enumerate(st_prev):
                aliases[len(args)] = 1 + n
                in_specs.append(pl.BlockSpec(memory_space=pl.ANY))
                args.append(a)
        st_shape = jax.ShapeDtypeStruct((n_seq_total, DEPTH, GLA_HEADS, GLA_DK, GLA_DV), F32)
        out_specs += [st_spec, st_spec]
        out_shape += [st_shape, st_shape]
    unroll = min(SCAN_UNROLL, seq_len // (2 * CHUNK))
    par = 1 if s0 is not None else min(n_seq, SCAN_UNROLL // unroll)
    kern = functools.partial(_gla_kernel, seq_len=seq_len, n_seq=n_seq, has_state=s0 is not None,
                             emit_state=emit_state, unroll=unroll, par=par, layer=layer,
                             all_layers=emit_state and st_prev is None, n_alias=len(aliases))
    return pl.pallas_call(
        kern,
        grid=(n_steps, HEAD_PAIRS),
        in_specs=in_specs,
        out_specs=out_specs,
        out_shape=out_shape,
        input_output_aliases=aliases,
        scratch_shapes=[pltpu.VMEM((rows, PAIR_V), F32), pltpu.VMEM((par, GLA_DV, PAIR_K), F32),
                        pltpu.VMEM((par, GLA_DV, PAIR_K), F32)],
        compiler_params=_params(2),
        name="gla_state" if emit_state else "gla",
    )(*args)


def _fft_tables(n_side):
    c, s = _dft_tables(n_side)
    l1 = np.block([[c, -s], [-s, -c]])
    l2 = np.concatenate([c, s], axis=1)
    idx = np.arange(n_side)
    ang = 2.0 * np.pi * (idx[:, None] * idx[None, :]) / (n_side * n_side)
    lanes = np.ones((1, 1, FNO_GROUP_W), np.float32)
    tc = np.cos(ang).astype(np.float32)[:, :, None] * lanes
    ts = np.sin(ang).astype(np.float32)[:, :, None] * lanes
    return l1, l2, tc, ts


def _fft_kernel(uc_ref, us_ref, l1_ref, l2_ref, tc_ref, ts_ref, o_ref,
                xr_ref, xi_ref, zr_ref, zi_ref, y_ref, *, n_side):
    for r in range(n_side):
        src, dst = pl.ds(r * n_side, n_side), pl.ds(r * FFT_PITCH, n_side)
        xr_ref[dst, :] = uc_ref[src, :]
        xi_ref[dst, :] = us_ref[src, :]

    def stage1(n2, carry):
        col = pl.ds(n2, n_side, stride=FFT_PITCH)
        x = jnp.concatenate([xr_ref[col, :].astype(BF16), xi_ref[col, :].astype(BF16)], axis=0)
        z = _dot(l1_ref[...], x)
        zr, zi = z[:n_side], z[n_side:]
        tc, ts = tc_ref[n2], ts_ref[n2]
        out = pl.ds(pl.multiple_of(n2 * FFT_PITCH, 8), n_side)
        zr_ref[out, :] = zr * tc + zi * ts
        zi_ref[out, :] = zi * tc - zr * ts
        return carry

    def stage2(k1, carry):
        col = pl.ds(k1, n_side, stride=FFT_PITCH)
        z = jnp.concatenate([zr_ref[col, :].astype(BF16), zi_ref[col, :].astype(BF16)], axis=0)
        y_ref[col, :] = _dot(l2_ref[...], z)
        return carry

    lax.fori_loop(0, n_side, stage1, 0, unroll=FFT_UNROLL)
    lax.fori_loop(0, n_side, stage2, 0, unroll=FFT_UNROLL)
    for r in range(n_side):
        o_ref[pl.ds(r * n_side, n_side), :] = y_ref[pl.ds(r * FFT_PITCH, n_side), :]


def _fnet_fft(uc, us, seq_len, n_seq_total):
    n_side = GRID_W
    l1, l2, tc, ts = _fft_tables(n_side)
    as_bf16 = lambda a: jnp.asarray(a, F32).astype(BF16)
    blk = pl.BlockSpec((seq_len, FNO_GROUP_W), lambda b, g: (b, g))
    const = lambda a: pl.BlockSpec(a.shape, lambda b, g: (0,) * a.ndim)
    return pl.pallas_call(
        functools.partial(_fft_kernel, n_side=n_side),
        grid=(n_seq_total, FNO_GROUPS),
        in_specs=[blk, blk, const(l1), const(l2), const(tc), const(ts)],
        out_specs=blk,
        out_shape=jax.ShapeDtypeStruct((n_seq_total * seq_len, FNO_W), F32),
        scratch_shapes=[pltpu.VMEM((n_side * FFT_PITCH, FNO_GROUP_W), F32)] * 5,
        compiler_params=_params(2),
        name="fnet_fft",
    )(uc, us, as_bf16(l1), as_bf16(l2), jnp.asarray(tc), jnp.asarray(ts))


def _post_kernel(*refs, dft_len, pos_tiles):
    x_ref, o_ref, g_ref, gh_ref = refs[:4]
    refs = refs[4:]
    if dft_len:
        uc_ref, us_ref, cos_ref, nsin_ref = refs[:4]
        refs = refs[4:]
        yf = jnp.concatenate(
            [(_dot(cos_ref[...], uc_ref[r0:r0 + dft_len, :]) +
              _dot(nsin_ref[...], us_ref[r0:r0 + dft_len, :])).astype(BF16)
             for r0 in range(0, ROW_TILE, dft_len)], axis=0)
    else:
        yf_ref, refs = refs[0], refs[1:]
        yf = yf_ref[...].astype(BF16)
    x_in = x_ref[...]
    if pos_tiles:
        x_in = x_in + _pos_tile(refs[0], pl.program_id(0) % pos_tiles)
        refs = refs[1:]
    mod_ref, gpm_ref, gpre_ref, gpost_ref, wout_ref, w1_ref, w2_ref, out_ref = refs
    gt_m = mod_ref[:, 2 * D_MODEL:3 * D_MODEL]
    sh_f = mod_ref[:, 3 * D_MODEL:4 * D_MODEL]
    sc_f = mod_ref[:, 4 * D_MODEL:5 * D_MODEL]
    gt_f = mod_ref[:, 5 * D_MODEL:6 * D_MODEL]
    x_blk, h_blk = [], []
    for r0 in range(0, ROW_TILE, POST_HEAD_ROWS):
        rows = slice(r0, r0 + POST_HEAD_ROWS)
        o_sum = o_ref[rows, :].astype(F32)
        gate = g_ref[rows, :].astype(F32)
        o = jnp.concatenate(
            [_rms(o_sum[:, sl], gh_ref[:, sl]) * _silu(gate[:, sl])
             for sl in (slice(hd * GLA_DV, (hd + 1) * GLA_DV) for hd in range(GLA_HEADS))], axis=1)
        y = _dot(o.astype(BF16), wout_ref[0:V_W, :]) + _dot(yf[rows, :], wout_ref[V_W:, :])
        x_blk.append(x_in[rows, :] + gt_m * _rms(y, gpm_ref[...]))
        h_blk.append((_rms(x_blk[-1], gpre_ref[...]) * (1.0 + sc_f) + sh_f).astype(BF16))
    x = jnp.concatenate(x_blk, axis=0)
    h = jnp.concatenate(h_blk, axis=0)
    f = jnp.zeros((ROW_TILE, D_MODEL), F32)
    n_split = 4
    wf = D_FF // n_split
    for j in range(n_split):
        a = jnp.maximum(_dot(h, w1_ref[:, j * wf:(j + 1) * wf].astype(BF16)), 0.0)
        f = f + _dot((a * a).astype(BF16), w2_ref[j * wf:(j + 1) * wf, :])
    out_ref[...] = x + gt_f * _rms(f, gpost_ref[...])


def _post(layer, x, o, g, g_head, fno, mod, mod_sel, g_post_mix, g_pre_mlp, g_post_mlp, w_out, w_fc1, w_fc2,
          pos=None):
    t = x.shape[0]
    row = lambda w: pl.BlockSpec((ROW_TILE, w), lambda i: (i, 0))
    lay = lambda *shape: pl.BlockSpec((None,) + shape, lambda i: (layer,) + (0,) * len(shape),
                                      pipeline_mode=pl.Buffered(1))
    dft_len = fno[2].shape[0] if len(fno) == 4 else 0
    fno_specs = [row(FNO_W)] * min(len(fno), 2) + [
        pl.BlockSpec((dft_len, dft_len), lambda i: (0, 0))] * (len(fno) - min(len(fno), 2))
    return pl.pallas_call(
        functools.partial(_post_kernel, dft_len=dft_len, pos_tiles=pos[1] if pos else 0),
        grid=(t // ROW_TILE,),
        in_specs=[row(D_MODEL), row(V_W), row(V_W), lay(1, V_W), *fno_specs,
                  *([pl.BlockSpec(pos[0].shape, lambda i: (0, 0))] if pos else []),
                  pl.BlockSpec((None, None, 1, 6 * D_MODEL), lambda i: (layer, mod_sel(i), 0, 0)),
                  lay(1, D_MODEL), lay(1, D_MODEL), lay(1, D_MODEL),
                  lay(V_W + FNO_W, D_MODEL), lay(D_MODEL, D_FF), lay(D_FF, D_MODEL)],
        out_specs=row(D_MODEL),
        out_shape=jax.ShapeDtypeStruct((t, D_MODEL), F32),
        compiler_params=_params(1),
        name="post",
    )(x, o, g, g_head, *fno, *([pos[0]] if pos else []), mod, g_post_mix, g_pre_mlp, g_post_mlp,
      w_out, w_fc1, w_fc2)


def _state_to_kernel(s):
    b, l = s.shape[:2]
    st = s.reshape(b, l, HEAD_PAIRS, 2, GLA_DK, GLA_DV)
    return jnp.transpose(st, (0, 1, 2, 5, 3, 4)).reshape(b, l, HEAD_PAIRS, GLA_DV, PAIR_K)


def kernel(x_prompt, x_sample, state_gla_fwd, state_gla_bwd, c, c_ctx, w_in, w_out, w_a2, b_a2,
           g_head, w_ada, b_ada, g_pre_mix, g_post_mix, g_pre_mlp, g_post_mlp, w_fc1, w_fc2):
    batch, seq, _ = x_prompt.shape
    dec_batch, dec_seq, _ = x_sample.shape
    t_ctx = batch * seq
    t_lat = dec_batch * dec_seq
    assert dec_seq == GRID_W * GRID_W and seq % CHUNK == 0 and ROW_TILE % seq == 0
    lat_tiles = dec_seq // ROW_TILE
    mod_sel = (lambda i: 0, lambda i: 1 + i // lat_tiles)

    za = jnp.zeros((DEPTH, LR_RANK, QK_W), F32)
    wa2 = jnp.concatenate([jnp.concatenate([w_a2[:, 0], za], axis=-1),
                           jnp.concatenate([za, w_a2[:, 1]], axis=-1)], axis=1).astype(BF16)
    ba2 = b_a2.reshape(DEPTH, 1, 2 * QK_W)
    w_t = jnp.swapaxes(w_in, 1, 2)
    w_out_b = w_out.astype(BF16)
    w_fc2_b = w_fc2.astype(BF16)
    vec = lambda a: a.reshape(DEPTH, 1, -1)
    g_head_r = g_head.reshape(DEPTH, 1, V_W)

    gapf, gapb, lvlf, lvlb = _gap_matrices()
    as_bf16 = lambda a: jnp.asarray(a, F32).astype(BF16)
    consts = (as_bf16(gapf), as_bf16(gapb), jnp.asarray(lvlf), jnp.asarray(lvlb))
    cw, sw = _dft_tables(FNO_GROUP_W)
    cs_w = as_bf16(np.concatenate([cw, sw], axis=1))
    cn, sn = _dft_tables(seq)
    cos_ctx, nsin_ctx = as_bf16(cn), as_bf16(-sn)
    pos = (jnp.asarray(_pos_table()), lat_tiles)

    cvec = jnp.zeros((MOD_ROWS, D_MODEL), F32).at[0].set(c_ctx).at[1:1 + dec_batch].set(c)
    mod = _modulation(cvec, w_ada, b_ada)[:, :1 + dec_batch, None, :]

    xs = [x_prompt.reshape(t_ctx, D_MODEL), x_sample.reshape(t_lat, D_MODEL)]
    s0 = (_state_to_kernel(state_gla_fwd), _state_to_kernel(state_gla_bwd))

    states = None
    for l in range(DEPTH):
        lat_pos = pos if l == 0 else None
        pre = [_premix(l, x, mod, sel, vec(g_pre_mix), w_t, wa2, ba2, cs_w, dt, p)
               for x, sel, dt, p in zip(xs, mod_sel, (BF16, F32), (None, lat_pos))]
        q, k, v, g_ctx, la, uc, us = pre[0]
        o_ctx, *states = _gla(l, q, k, v, la, consts, seq, batch, 16, emit_state=True, st_prev=states)
        fno_ctx = (uc, us, cos_ctx, nsin_ctx)
        q, k, v, g_lat, la, uc, us = pre[1]
        (o_lat,) = _gla(l, q, k, v, la, consts, dec_seq, dec_batch, 1, s0=s0)
        fno_lat = (_fnet_fft(uc, us, dec_seq, dec_batch),)
        xs = [_post(l, x, o, g, g_head_r, fno, mod, sel, vec(g_post_mix), vec(g_pre_mlp), vec(g_post_mlp),
                    w_out_b, w_fc1, w_fc2_b, p)
              for x, o, g, fno, sel, p in zip(xs, (o_ctx, o_lat), (g_ctx, g_lat), (fno_ctx, fno_lat),
                                              mod_sel, (None, lat_pos))]

    y_prompt = xs[0].reshape(batch, seq, D_MODEL)
    y_sample = xs[1].reshape(dec_batch, dec_seq, D_MODEL)
    return (y_prompt, y_sample, states[0], states[1])
```
